```python
import jax, jax.numpy as jnp
from jax import lax
import numpy as np

D_MODEL = 1024
BATCH = 4
SEQ = 4096
DEPTH = 4

HEAD_DIM = 64
N_HEADS_TOTAL = D_MODEL // HEAD_DIM
N_MEM_HEADS = 4
N_MIX_HEADS = N_HEADS_TOTAL - N_MEM_HEADS
D_MIX = N_MIX_HEADS * HEAD_DIM
D_MEMQ = N_MEM_HEADS * HEAD_DIM
MEM_TOKENS = 256
D_FF = 2816
CONV_WIDTH = 3
ROPE_THETA = 10000.0
Q_BLOCK = 128
DILATED_BRANCHES = ((128, 1), (512, 4), (2048, 16))
N_MIXERS = 2
N_A_LAYERS = (DEPTH + 1) // 2
N_B_LAYERS = DEPTH // 2
FOX_IN = 3 * D_MIX + N_MIX_HEADS + D_MEMQ
DIL_IN = 3 * D_MIX + D_MEMQ
NORM_EPS = 1e-6
NEG = -1e30

kernel_name = 'hybrid_fox_dilated_memory_convffn'


def rmsnorm(x, g):
    xf = x.astype(jnp.float32)
    y = xf * lax.rsqrt(jnp.mean(xf * xf, axis=-1, keepdims=True) + NORM_EPS)
    return (y * g.astype(jnp.float32)).astype(x.dtype)


def split_heads(t, n_heads):
    b, s, _ = t.shape
    return t.reshape(b, s, n_heads, HEAD_DIM).transpose(0, 2, 1, 3)


def merge_heads(t):
    b, h, s, d = t.shape
    return t.transpose(0, 2, 1, 3).reshape(b, s, h * d)


def rope_tables(seq):
    inv = 1.0 / (ROPE_THETA ** (jnp.arange(0, HEAD_DIM, 2, dtype=jnp.float32) / HEAD_DIM))
    ang = jnp.arange(seq, dtype=jnp.float32)[:, None] * inv[None, :]
    return jnp.cos(ang), jnp.sin(ang)


def apply_rope(x, cos, sin):
    x1, x2 = jnp.split(x.astype(jnp.float32), 2, axis=-1)
    y = jnp.concatenate([x1 * cos - x2 * sin, x2 * cos + x1 * sin], axis=-1)
    return y.astype(x.dtype)


def fox_attention(q, k, v, log_f):
    b, h, s, dh = q.shape
    c = lax.cumsum(log_f, axis=2)
    nb = s // Q_BLOCK
    scale = dh ** -0.5
    qb = q.reshape(b, h, nb, Q_BLOCK, dh).transpose(2, 0, 1, 3, 4)
    cb = c.reshape(b, h, nb, Q_BLOCK).transpose(2, 0, 1, 3)
    starts = jnp.arange(nb, dtype=jnp.int32) * Q_BLOCK
    kpos = jnp.arange(s, dtype=jnp.int32)

    def one_block(args):
        qi, ci, st = args
        sc = jnp.einsum('bhqd,bhkd->bhqk', qi, k, preferred_element_type=jnp.float32) * scale
        sc = sc + ci[..., :, None] - c[..., None, :]
        qpos = st + jnp.arange(Q_BLOCK, dtype=jnp.int32)
        sc = jnp.where(kpos[None, :] <= qpos[:, None], sc, NEG)
        p = jax.nn.softmax(sc, axis=-1)
        return jnp.einsum('bhqk,bhkd->bhqd', p.astype(v.dtype), v)

    out = lax.map(one_block, (qb, cb, starts))
    return out.transpose(1, 2, 0, 3, 4).reshape(b, h, s, dh)


def dilated_branch(q, k, v, window, dilation):
    b, h, s, dh = q.shape
    L = window // dilation
    chunk = dilation * L
    sp = -(-s // chunk) * chunk
    n = sp // dilation
    nb = n // L
    scale = dh ** -0.5

    def strided(t):
        t = jnp.pad(t, ((0, 0), (0, 0), (0, sp - s), (0, 0)))
        t = t.reshape(b, h, n, dilation, dh).transpose(0, 1, 3, 2, 4)
        return t.reshape(b, h, dilation, nb, L, dh)

    def with_prev(t):
        prev = jnp.pad(t, ((0, 0), (0, 0), (0, 0), (1, 0), (0, 0), (0, 0)))[:, :, :, :-1]
        return jnp.concatenate([prev, t], axis=4)

    qs = strided(q)
    kb = with_prev(strided(k))
    vb = with_prev(strided(v))
    sc = jnp.einsum('bhrnqd,bhrnkd->bhrnqk', qs, kb, preferred_element_type=jnp.float32) * scale
    a = jnp.arange(L, dtype=jnp.int32)
    cidx = jnp.arange(2 * L, dtype=jnp.int32)
    blk = jnp.arange(nb, dtype=jnp.int32)
    dist = a[:, None] + L - cidx[None, :]
    valid = (dist >= 0) & (dist <= L)
    valid = valid[None] & ((blk[:, None, None] > 0) | (cidx[None, None, :] >= L))
    sc = jnp.where(valid, sc, NEG)
    m = jnp.max(sc, axis=-1, keepdims=True)
    e = jnp.exp(sc - m)
    den = jnp.sum(e, axis=-1, keepdims=True)
    o = jnp.einsum('bhrnqk,bhrnkd->bhrnqd', (e / den).astype(v.dtype), vb)
    lse = (m + jnp.log(den))[..., 0]
    o = o.reshape(b, h, dilation, n, dh).transpose(0, 1, 3, 2, 4).reshape(b, h, sp, dh)[:, :, :s]
    lse = lse.reshape(b, h, dilation, n).transpose(0, 1, 3, 2).reshape(b, h, sp)[:, :, :s]
    return o, lse


def dilated_attention(q, k, v):
    outs, lses = [], []
    for window, dilation in DILATED_BRANCHES:
        o, lse = dilated_branch(q, k, v, window, dilation)
        outs.append(o)
        lses.append(lse)
    wts = jax.nn.softmax(jnp.stack(lses, axis=0), axis=0)
    return jnp.einsum('gbhs,gbhsd->bhsd', wts.astype(v.dtype), jnp.stack(outs, axis=0))


def memory_attention(qm, mem_n, w_mem_kv):
    km, vm = jnp.split(mem_n @ w_mem_kv, 2, axis=-1)
    km = split_heads(km, N_MEM_HEADS)
    vm = split_heads(vm, N_MEM_HEADS)
    sc = jnp.einsum('bhqd,bhkd->bhqk', qm, km, preferred_element_type=jnp.float32) * (HEAD_DIM ** -0.5)
    p = jax.nn.softmax(sc, axis=-1)
    return jnp.einsum('bhqk,bhkd->bhqd', p.astype(vm.dtype), vm)


def conv_ffn(xn, w_up, conv_w, conv_b, w_down):
    u = xn @ w_up
    s = u.shape[1]
    up = jnp.pad(u, ((0, 0), (CONV_WIDTH - 1, 0), (0, 0)))
    c = conv_b
    for j in range(CONV_WIDTH):
        c = c + conv_w[j] * up[:, j:j + s]
    val, gate = jnp.split(c, 2, axis=-1)
    return (jax.nn.silu(gate) * val) @ w_down


def setup_inputs(seed: int = 0) -> dict:
    key = jax.random.key(seed)
    ks = jax.random.split(key, 16)
    f32 = jnp.float32
    nrm = lambda k, shape, scale: jax.random.normal(k, shape, f32) * scale
    forget_bias_init = 3.0
    return {
        'x': nrm(ks[0], (BATCH, SEQ, D_MODEL), 1.0),
        'mem': nrm(ks[1], (BATCH, MEM_TOKENS, D_MODEL), 1.0),
        'norm_mix': 1.0 + nrm(ks[2], (DEPTH, D_MODEL), 0.1),
        'norm_mem': 1.0 + nrm(ks[3], (DEPTH, D_MODEL), 0.1),
        'norm_ffn': 1.0 + nrm(ks[4], (DEPTH, D_MODEL), 0.1),
        'w_in_fox': nrm(ks[5], (N_A_LAYERS, D_MODEL, FOX_IN), D_MODEL ** -0.5),
        'b_forget': forget_bias_init + nrm(ks[6], (N_A_LAYERS, N_MIX_HEADS), 0.5),
        'w_in_dil': nrm(ks[7], (N_B_LAYERS, D_MODEL, DIL_IN), D_MODEL ** -0.5),
        'w_mem_kv': nrm(ks[8], (DEPTH, D_MODEL, 2 * D_MEMQ), D_MODEL ** -0.5),
        'w_out': nrm(ks[9], (DEPTH, D_MODEL, D_MODEL), D_MODEL ** -0.5),
        'w_up': nrm(ks[10], (DEPTH, D_MODEL, 2 * D_FF), D_MODEL ** -0.5),
        'conv_w': nrm(ks[11], (DEPTH, CONV_WIDTH, 2 * D_FF), CONV_WIDTH ** -0.5),
        'conv_b': nrm(ks[12], (DEPTH, 2 * D_FF), 0.02),
        'w_down': nrm(ks[13], (DEPTH, D_FF, D_MODEL), D_FF ** -0.5),
        'norm_final': 1.0 + nrm(ks[14], (D_MODEL,), 0.1),
    }


def reference(x, mem, norm_mix, norm_mem, norm_ffn, w_in_fox, b_forget, w_in_dil,
              w_mem_kv, w_out, w_up, conv_w, conv_b, w_down, norm_final):
    s = x.shape[1]
    cos, sin = rope_tables(s)
    h = x
    for layer in range(DEPTH):
        kind = layer % N_MIXERS
        slot = layer // N_MIXERS
        xn = rmsnorm(h, norm_mix[layer])
        mn = rmsnorm(mem, norm_mem[layer])
        if kind == 0:
            proj = xn @ w_in_fox[slot]
            q, k, v, f_logit, qm = jnp.split(
                proj, [D_MIX, 2 * D_MIX, 3 * D_MIX, 3 * D_MIX + N_MIX_HEADS], axis=-1)
            log_f = jax.nn.log_sigmoid(
                (f_logit + b_forget[slot]).astype(jnp.float32)).transpose(0, 2, 1)
            mix = fox_attention(split_heads(q, N_MIX_HEADS), split_heads(k, N_MIX_HEADS),
                                split_heads(v, N_MIX_HEADS), log_f)
        else:
            proj = xn @ w_in_dil[slot]
            q, k, v, qm = jnp.split(proj, [D_MIX, 2 * D_MIX, 3 * D_MIX], axis=-1)
            qh = apply_rope(split_heads(q, N_MIX_HEADS), cos, sin)
            kh = apply_rope(split_heads(k, N_MIX_HEADS), cos, sin)
            mix = dilated_attention(qh, kh, split_heads(v, N_MIX_HEADS))
        mem_out = memory_attention(split_heads(qm, N_MEM_HEADS), mn, w_mem_kv[layer])
        heads = jnp.concatenate([mix, mem_out], axis=1)
        h = h + merge_heads(heads) @ w_out[layer]
        h = h + conv_ffn(rmsnorm(h, norm_ffn[layer]), w_up[layer], conv_w[layer],
                         conv_b[layer], w_down[layer])
    return rmsnorm(h, norm_final)
```

```python
import functools

import jax
import jax.numpy as jnp
from jax import lax
from jax.experimental import pallas as pl
from jax.experimental.pallas import tpu as pltpu

D_MODEL = 1024
HEAD_DIM = 64
N_MEM_HEADS = 4
N_MIX_HEADS = 12
D_MIX = N_MIX_HEADS * HEAD_DIM
D_MEMQ = N_MEM_HEADS * HEAD_DIM
D_FF = 2816
CONV_WIDTH = 3
ROPE_THETA = 10000.0
DIL_BLOCK = 128
DILATIONS = (1, 4, 16)
NORM_EPS = 1e-6
NEG = -1e30
SCALE = HEAD_DIM ** -0.5

LANES = 128
HEAD_PAD = 16
N_PAIRS = D_MIX // LANES
VMEM_LIMIT = 48 * 1024 * 1024

ROW_TILE = 512
FOX_TILE = 512
CUM_TILE = FOX_TILE
FF_CHUNK = 256

_NT = (((1,), (1,)), ((), ()))


def _cparams(n_axes):
    return pltpu.CompilerParams(
        dimension_semantics=("arbitrary",) * n_axes,
        vmem_limit_bytes=VMEM_LIMIT)


def _rms(x, g):
    y = x * lax.rsqrt(jnp.mean(x * x, axis=-1, keepdims=True) + NORM_EPS)
    return y * g


def _lane_first_half():
    lane = lax.broadcasted_iota(jnp.int32, (1, LANES), 1)
    return lane < HEAD_DIM


def _in_proj_kernel(*refs, rope, forget):
    x_ref, g_ref, wq_ref, wk_ref, wv_ref, wm_ref = refs[:6]
    pos = 6
    if forget:
        wf_ref = refs[pos]; pos += 1
    if rope:
        cos_ref, sin_ref = refs[pos:pos + 2]; pos += 2
    q_ref, k_ref, v_ref, qm_ref = refs[pos:pos + 4]; pos += 4
    if forget:
        f_ref = refs[pos]

    xn = _rms(x_ref[...], g_ref[...]).astype(jnp.bfloat16)

    def rotate(t):
        lane = lax.broadcasted_iota(jnp.int32, (1, LANES), 1)
        first = (lane % HEAD_DIM) < (HEAD_DIM // 2)
        partner = jnp.where(first, pltpu.roll(t, LANES - HEAD_DIM // 2, 1),
                            pltpu.roll(t, HEAD_DIM // 2, 1))
        return t * cos_ref[...] + partner * sin_ref[...]

    q = jnp.dot(xn, wq_ref[...], preferred_element_type=jnp.float32)
    k = jnp.dot(xn, wk_ref[...], preferred_element_type=jnp.float32)
    if rope:
        for p in range(N_PAIRS):
            cols = slice(p * LANES, (p + 1) * LANES)
            q_ref[:, cols] = (rotate(q[:, cols]) * SCALE).astype(jnp.bfloat16)
            k_ref[:, cols] = rotate(k[:, cols]).astype(jnp.bfloat16)
    else:
        q_ref[...] = (q * SCALE).astype(jnp.bfloat16)
        k_ref[...] = k.astype(jnp.bfloat16)
    v_ref[...] = jnp.dot(xn, wv_ref[...],
                         preferred_element_type=jnp.float32).astype(jnp.bfloat16)
    qm = jnp.dot(xn, wm_ref[...], preferred_element_type=jnp.float32)
    qm_ref[...] = (qm * SCALE).astype(jnp.bfloat16)
    if forget:
        f_ref[...] = lax.dot_general(wf_ref[...], xn, _NT,
                                     preferred_element_type=jnp.float32)


def _in_proj(h, g, wq, wk, wv, wm, wf_t=None, rope_tabs=None, *, seq):
    t = h.shape[0]
    tm = ROW_TILE
    forget = wf_t is not None
    rope = rope_tabs is not None
    row = lambda i: (i, 0)
    fixed = lambda i: (0, 0)
    in_specs = [pl.BlockSpec((tm, D_MODEL), row),
                pl.BlockSpec((1, D_MODEL), fixed),
                pl.BlockSpec((D_MODEL, D_MIX), fixed),
                pl.BlockSpec((D_MODEL, D_MIX), fixed),
                pl.BlockSpec((D_MODEL, D_MIX), fixed),
                pl.BlockSpec((D_MODEL, D_MEMQ), fixed)]
    args = [h, g, wq, wk, wv, wm]
    if forget:
        in_specs.append(pl.BlockSpec((HEAD_PAD, D_MODEL), fixed))
        args.append(wf_t)
    if rope:
        tiles_per_seq = seq // tm
        tab = lambda i: (i % tiles_per_seq, 0)
        in_specs += [pl.BlockSpec((tm, LANES), tab), pl.BlockSpec((tm, LANES), tab)]
        args += list(rope_tabs)
    out_shape = [jax.ShapeDtypeStruct((t, D_MIX), jnp.bfloat16)] * 3
    out_shape.append(jax.ShapeDtypeStruct((t, D_MEMQ), jnp.bfloat16))
    out_specs = [pl.BlockSpec((tm, D_MIX), row)] * 3
    out_specs.append(pl.BlockSpec((tm, D_MEMQ), row))
    if forget:
        out_shape.append(jax.ShapeDtypeStruct((HEAD_PAD, t), jnp.float32))
        out_specs.append(pl.BlockSpec((HEAD_PAD, tm), lambda i: (0, i)))
    return pl.pallas_call(
        functools.partial(_in_proj_kernel, rope=rope, forget=forget),
        grid=(t // tm,),
        in_specs=in_specs, out_specs=out_specs, out_shape=out_shape,
        compiler_params=_cparams(1),
        name="in_proj_rope" if rope else "in_proj_fox",
    )(*args)


def _mem_kv_kernel(x_ref, g_ref, wk_ref, wv_ref, km_ref, vm_ref):
    xn = _rms(x_ref[...], g_ref[...]).astype(jnp.bfloat16)
    km_ref[...] = jnp.dot(xn, wk_ref[...],
                          preferred_element_type=jnp.float32).astype(jnp.bfloat16)
    vm_ref[...] = jnp.dot(xn, wv_ref[...],
                          preferred_element_type=jnp.float32).astype(jnp.bfloat16)


def _mem_kv(mem2d, g, wk, wv, *, mem_tokens):
    rows = mem2d.shape[0]
    row = lambda i: (i, 0)
    fixed = lambda i: (0, 0)
    return pl.pallas_call(
        _mem_kv_kernel,
        grid=(rows // mem_tokens,),
        in_specs=[pl.BlockSpec((mem_tokens, D_MODEL), row),
                  pl.BlockSpec((1, D_MODEL), fixed),
                  pl.BlockSpec((D_MODEL, D_MEMQ), fixed),
                  pl.BlockSpec((D_MODEL, D_MEMQ), fixed)],
        out_specs=[pl.BlockSpec((mem_tokens, D_MEMQ), row)] * 2,
        out_shape=[jax.ShapeDtypeStruct((rows, D_MEMQ), jnp.bfloat16)] * 2,
        compiler_params=_cparams(1),
        name="mem_kv",
    )(mem2d, g, wk, wv)


def _forget_cumsum_kernel(f_ref, b_ref, c_ref, *, n_chunks):
    tk = CUM_TILE
    x = f_ref[...] + b_ref[...]
    ls = jnp.minimum(x, 0.0) - jnp.log1p(jnp.exp(-jnp.abs(x)))
    r = lax.broadcasted_iota(jnp.int32, (tk, tk), 0)
    c = lax.broadcasted_iota(jnp.int32, (tk, tk), 1)
    tri = (r <= c).astype(jnp.bfloat16)
    carry = jnp.zeros((HEAD_PAD, 1), jnp.float32)
    for j in range(n_chunks):
        blk = ls[:, j * tk:(j + 1) * tk]
        hi = blk.astype(jnp.bfloat16)
        rem = blk - hi.astype(jnp.float32)
        mid = rem.astype(jnp.bfloat16)
        lo = (rem - mid.astype(jnp.float32)).astype(jnp.bfloat16)
        cs = (jnp.dot(hi, tri, preferred_element_type=jnp.float32)
              + jnp.dot(mid, tri, preferred_element_type=jnp.float32)
              + jnp.dot(lo, tri, preferred_element_type=jnp.float32)) + carry
        c_ref[j] = cs
        carry = cs[:, tk - 1:tk]


def _forget_cumsum(f_t, bias, *, batch, seq):
    n_chunks = seq // CUM_TILE
    return pl.pallas_call(
        functools.partial(_forget_cumsum_kernel, n_chunks=n_chunks),
        grid=(batch,),
        in_specs=[pl.BlockSpec((HEAD_PAD, seq), lambda b: (0, b)),
                  pl.BlockSpec((HEAD_PAD, 1), lambda b: (0, 0))],
        out_specs=pl.BlockSpec((n_chunks, HEAD_PAD, CUM_TILE), lambda b: (b, 0, 0)),
        out_shape=jax.ShapeDtypeStruct((batch * n_chunks, HEAD_PAD, CUM_TILE),
                                       jnp.float32),
        compiler_params=_cparams(1),
        name="forget_cumsum",
    )(f_t, bias)


def _fox_kernel(q_ref, k_ref, v_ref, c_ref, o_ref, m_scr, l_scr, acc_scr):
    tq = tk = FOX_TILE
    pair = pl.program_id(1)
    qi = pl.program_id(2)
    first = _lane_first_half()
    q = q_ref[...]
    zero = jnp.zeros_like(q)
    q_heads = (jnp.where(first, q, zero), jnp.where(first, zero, q))
    c_base = [c_ref[qi, pl.ds(2 * pair + hh, 1), :][:, 0:1] for hh in range(2)]

    m_scr[...] = jnp.full(m_scr.shape, NEG, jnp.float32)
    l_scr[...] = jnp.zeros(l_scr.shape, jnp.float32)
    acc_scr[...] = jnp.zeros(acc_scr.shape, jnp.float32)

    def step(j, masked):
        start = pl.multiple_of(j * tk, tk)
        k = k_ref[pl.ds(start, tk), :]
        v = v_ref[pl.ds(start, tk), :]
        for hh in range(2):
            s = lax.dot_general(q_heads[hh], k, _NT, preferred_element_type=jnp.float32)
            s = s + (c_base[hh] - c_ref[j, pl.ds(2 * pair + hh, 1), :])
            if masked:
                row = lax.broadcasted_iota(jnp.int32, (tq, tk), 0)
                col = lax.broadcasted_iota(jnp.int32, (tq, tk), 1)
                s = jnp.where(col <= row, s, NEG)
            m_prev = m_scr[hh]
            m_new = jnp.maximum(m_prev, jnp.max(s, axis=-1, keepdims=True))
            alpha = jnp.exp(m_prev - m_new)
            p = jnp.exp(s - m_new)
            l_scr[hh] = alpha * l_scr[hh] + jnp.sum(p, axis=-1, keepdims=True)
            acc_scr[hh] = alpha * acc_scr[hh] + jnp.dot(
                p.astype(jnp.bfloat16), v, preferred_element_type=jnp.float32)
            m_scr[hh] = m_new

    def body(j, carry):
        step(j, False)
        return carry

    lax.fori_loop(0, qi, body, 0)
    step(qi, True)
    o0 = acc_scr[0] * (1.0 / l_scr[0])
    o1 = acc_scr[1] * (1.0 / l_scr[1])
    o_ref[...] = jnp.where(first, o0, o1).astype(o_ref.dtype)


def _fox_attention(q, k, v, c, *, batch, seq):
    tq = FOX_TILE
    n_chunks = seq // tq
    q3, k3, v3 = (a.reshape(batch, seq, D_MIX) for a in (q, k, v))
    out = pl.pallas_call(
        _fox_kernel,
        grid=(batch, N_PAIRS, n_chunks),
        in_specs=[pl.BlockSpec((None, tq, LANES), lambda b, p, i: (b, i, p)),
                  pl.BlockSpec((None, seq, LANES), lambda b, p, i: (b, 0, p)),
                  pl.BlockSpec((None, seq, LANES), lambda b, p, i: (b, 0, p)),
                  pl.BlockSpec((n_chunks, HEAD_PAD, tq), lambda b, p, i: (b, 0, 0))],
        out_specs=pl.BlockSpec((None, tq, LANES), lambda b, p, i: (b, i, p)),
        out_shape=jax.ShapeDtypeStruct((batch, seq, D_MIX), jnp.bfloat16),
        scratch_shapes=[pltpu.VMEM((2, tq, 1), jnp.float32),
                        pltpu.VMEM((2, tq, 1), jnp.float32),
                        pltpu.VMEM((2, tq, LANES), jnp.float32)],
        compiler_params=_cparams(3),
        name="fox_attention",
    )(q3, k3, v3, c)
    return out.reshape(batch * seq, D_MIX)


def _dilated_kernel(q_ref, kp_ref, kc_ref, vp_ref, vc_ref, o_ref, lse_ref):
    blk = DIL_BLOCK
    n = pl.program_id(2)
    first = _lane_first_half()
    row = lax.broadcasted_iota(jnp.int32, (blk, 2 * blk), 0)
    col = lax.broadcasted_iota(jnp.int32, (blk, 2 * blk), 1)
    dist = row + blk - col
    first_key = jnp.where(n > 0, 0, blk)
    valid = (dist >= 0) & (dist <= blk) & (col >= first_key)
    for p in range(N_PAIRS):
        cols = slice(p * LANES, (p + 1) * LANES)
        q = q_ref[:, cols]
        zero = jnp.zeros_like(q)
        k2 = jnp.concatenate([kp_ref[:, cols], kc_ref[:, cols]], axis=0)
        v2 = jnp.concatenate([vp_ref[:, cols], vc_ref[:, cols]], axis=0)
        outs, lses = [], []
        for hh in range(2):
            qh = jnp.where(first, q, zero) if hh == 0 else jnp.where(first, zero, q)
            s = lax.dot_general(qh, k2, _NT, preferred_element_type=jnp.float32)
            s = jnp.where(valid, s, NEG)
            m = jnp.max(s, axis=-1, keepdims=True)
            e = jnp.exp(s - m)
            den = jnp.sum(e, axis=-1, keepdims=True)
            pn = (e * (1.0 / den)).astype(jnp.bfloat16)
            outs.append(jnp.dot(pn, v2, preferred_element_type=jnp.float32))
            lses.append(m + jnp.log(den))
        o_ref[:, cols] = jnp.where(first, outs[0], outs[1])
        lse_ref[:, cols] = jnp.where(first, lses[0], lses[1])


def _dilated_branch(q, k, v, dilation, *, batch, seq):
    blk = DIL_BLOCK
    n_rows = seq // dilation
    n_blocks = n_rows // blk
    view = lambda a: a.reshape(batch, n_rows, dilation * D_MIX)
    cur = lambda b, r, n: (b, n, r)
    prev = lambda b, r, n: (b, jnp.maximum(n - 1, 0), r)
    spec = lambda f: pl.BlockSpec((None, blk, D_MIX), f)
    o, lse = pl.pallas_call(
        _dilated_kernel,
        grid=(batch, dilation, n_blocks),
        in_specs=[spec(cur), spec(prev), spec(cur), spec(prev), spec(cur)],
        out_specs=[spec(cur), spec(cur)],
        out_shape=[jax.ShapeDtypeStruct((batch, n_rows, dilation * D_MIX),
                                        jnp.float32)] * 2,
        compiler_params=_cparams(3),
        name=f"dilated_d{dilation}",
    )(view(q), view(k), view(k), view(v), view(v))
    t = batch * seq
    return o.reshape(t, D_MIX), lse.reshape(t, D_MIX)


def _out_proj_kernel(*refs, n_branches):
    h_ref = refs[0]
    pos = 1
    if n_branches:
        o_refs = refs[pos:pos + n_branches]; pos += n_branches
        lse_refs = refs[pos:pos + n_branches]; pos += n_branches
    else:
        mix_ref = refs[pos]; pos += 1
    qm_ref, km_ref, vm_ref, wmix_ref, wmem_ref, out_ref = refs[pos:pos + 6]

    if n_branches:
        lses = [r[...] for r in lse_refs]
        m = lses[0]
        for l in lses[1:]:
            m = jnp.maximum(m, l)
        es = [jnp.exp(l - m) for l in lses]
        den = es[0]
        for e in es[1:]:
            den = den + e
        inv = 1.0 / den
        mix = (es[0] * inv) * o_refs[0][...]
        for e, o_ref in zip(es[1:], o_refs[1:]):
            mix = mix + (e * inv) * o_ref[...]
        mix = mix.astype(jnp.bfloat16)
    else:
        mix = mix_ref[...]

    first = _lane_first_half()
    mem_pairs = []
    for p in range(D_MEMQ // LANES):
        cols = slice(p * LANES, (p + 1) * LANES)
        qm = qm_ref[:, cols]
        zero = jnp.zeros_like(qm)
        km = km_ref[:, cols]
        vm = vm_ref[:, cols]
        outs = []
        for hh in range(2):
            qh = jnp.where(first, qm, zero) if hh == 0 else jnp.where(first, zero, qm)
            s = lax.dot_general(qh, km, _NT, preferred_element_type=jnp.float32)
            mx = jnp.max(s, axis=-1, keepdims=True)
            e = jnp.exp(s - mx)
            pn = (e * (1.0 / jnp.sum(e, axis=-1, keepdims=True))).astype(jnp.bfloat16)
            outs.append(jnp.dot(pn, vm, preferred_element_type=jnp.float32))
        mem_pairs.append(jnp.where(first, outs[0], outs[1]).astype(jnp.bfloat16))
    mem_out = jnp.concatenate(mem_pairs, axis=-1)

    y = jnp.dot(mix, wmix_ref[...], preferred_element_type=jnp.float32)
    y = y + jnp.dot(mem_out, wmem_ref[...], preferred_element_type=jnp.float32)
    out_ref[...] = h_ref[...] + y


def _out_proj(h, mix_inputs, qm, km, vm, w_mix, w_mem, *, seq, mem_tokens, n_branches):
    t = h.shape[0]
    tm = ROW_TILE
    tiles_per_seq = seq // tm
    row = lambda i: (i, 0)
    fixed = lambda i: (0, 0)
    per_batch = lambda i: (i // tiles_per_seq, 0)
    mix_dtype_cols = [pl.BlockSpec((tm, D_MIX), row)] * len(mix_inputs)
    in_specs = ([pl.BlockSpec((tm, D_MODEL), row)] + mix_dtype_cols +
                [pl.BlockSpec((tm, D_MEMQ), row),
                 pl.BlockSpec((mem_tokens, D_MEMQ), per_batch),
                 pl.BlockSpec((mem_tokens, D_MEMQ), per_batch),
                 pl.BlockSpec((D_MIX, D_MODEL), fixed),
                 pl.BlockSpec((D_MEMQ, D_MODEL), fixed)])
    return pl.pallas_call(
        functools.partial(_out_proj_kernel, n_branches=n_branches),
        grid=(t // tm,),
        in_specs=in_specs,
        out_specs=pl.BlockSpec((tm, D_MODEL), row),
        out_shape=jax.ShapeDtypeStruct((t, D_MODEL), jnp.float32),
        compiler_params=_cparams(1),
        name="out_proj_merge" if n_branches else "out_proj",
    )(h, *mix_inputs, qm, km, vm, w_mix, w_mem)


def _ffn_kernel(h_ref, g_ref, wup_ref, cw_ref, cb_ref, wdown_ref, gfin_ref, out_ref,
                carry_ref, *, tiles_per_seq, final_norm):
    tm = ROW_TILE
    halo = 8
    i = pl.program_id(0)

    @pl.when(i % tiles_per_seq == 0)
    def _():
        carry_ref[...] = jnp.zeros(carry_ref.shape, jnp.float32)

    h = h_ref[...]
    xn = _rms(h, g_ref[...]).astype(jnp.bfloat16)
    row = lax.broadcasted_iota(jnp.int32, (tm, FF_CHUNK), 0)

    def conv(u, cols):
        prev = carry_ref[:, cols]
        back1 = jnp.where(row == 0, prev[halo - 1:halo, :], pltpu.roll(u, 1, 0))
        back2 = jnp.where(row == 0, prev[halo - 2:halo - 1, :],
                          jnp.where(row == 1, prev[halo - 1:halo, :], pltpu.roll(u, 2, 0)))
        carry_ref[:, cols] = u[tm - halo:tm, :]
        w = cw_ref[:, cols]
        c = cb_ref[:, cols] + w[0:1, :] * back2
        c = c + w[1:2, :] * back1
        return c + w[2:3, :] * u

    acc = jnp.zeros((tm, D_MODEL), jnp.float32)
    for j in range(D_FF // FF_CHUNK):
        vcols = slice(j * FF_CHUNK, (j + 1) * FF_CHUNK)
        gcols = slice(D_FF + j * FF_CHUNK, D_FF + (j + 1) * FF_CHUNK)
        val = conv(jnp.dot(xn, wup_ref[:, vcols], preferred_element_type=jnp.float32), vcols)
        gate = conv(jnp.dot(xn, wup_ref[:, gcols], preferred_element_type=jnp.float32), gcols)
        act = (gate * (1.0 / (1.0 + jnp.exp(-gate))) * val).astype(jnp.bfloat16)
        acc = acc + jnp.dot(act, wdown_ref[vcols, :], preferred_element_type=jnp.float32)
    y = h + acc
    if final_norm:
        y = _rms(y, gfin_ref[...])
    out_ref[...] = y


def _ffn(h, g, w_up, conv_w, conv_b, w_down, g_final, *, seq, final_norm):
    t = h.shape[0]
    tm = ROW_TILE
    row = lambda i: (i, 0)
    fixed = lambda i: (0, 0)
    once = dict(pipeline_mode=pl.Buffered(1))
    return pl.pallas_call(
        functools.partial(_ffn_kernel, tiles_per_seq=seq // tm, final_norm=final_norm),
        grid=(t // tm,),
        in_specs=[pl.BlockSpec((tm, D_MODEL), row),
                  pl.BlockSpec((1, D_MODEL), fixed),
                  pl.BlockSpec((D_MODEL, 2 * D_FF), fixed, **once),
                  pl.BlockSpec((CONV_WIDTH, 2 * D_FF), fixed),
                  pl.BlockSpec((1, 2 * D_FF), fixed),
                  pl.BlockSpec((D_FF, D_MODEL), fixed, **once),
                  pl.BlockSpec((1, D_MODEL), fixed)],
        out_specs=pl.BlockSpec((tm, D_MODEL), row),
        out_shape=jax.ShapeDtypeStruct((t, D_MODEL), jnp.float32),
        scratch_shapes=[pltpu.VMEM((8, 2 * D_FF), jnp.float32)],
        compiler_params=_cparams(1),
        name="conv_ffn_final" if final_norm else "conv_ffn",
    )(h, g, w_up, conv_w, conv_b, w_down, g_final)


def _rope_tables(seq):
    inv = 1.0 / (ROPE_THETA ** (jnp.arange(0, HEAD_DIM, 2, dtype=jnp.float32) / HEAD_DIM))
    ang = jnp.arange(seq, dtype=jnp.float32)[:, None] * inv[None, :]
    cos, sin = jnp.cos(ang), jnp.sin(ang)
    reps = LANES // HEAD_DIM
    cos_t = jnp.tile(jnp.concatenate([cos, cos], axis=-1), (1, reps))
    sin_t = jnp.tile(jnp.concatenate([-sin, sin], axis=-1), (1, reps))
    return cos_t, sin_t


def kernel(x, mem, norm_mix, norm_mem, norm_ffn, w_in_fox, b_forget, w_in_dil,
           w_mem_kv, w_out, w_up, conv_w, conv_b, w_down, norm_final):
    batch, seq, _ = x.shape
    mem_tokens = mem.shape[1]
    depth = norm_mix.shape[0]
    bf = lambda a: a.astype(jnp.bfloat16)
    h = x.reshape(batch * seq, D_MODEL)
    mem2d = mem.reshape(batch * mem_tokens, D_MODEL)
    rope_tabs = _rope_tables(seq)
    g_final = norm_final.reshape(1, D_MODEL)

    for layer in range(depth):
        kind, slot = layer % 2, layer // 2
        g_mix = norm_mix[layer].reshape(1, D_MODEL)
        km, vm = _mem_kv(mem2d, norm_mem[layer].reshape(1, D_MODEL),
                         bf(w_mem_kv[layer][:, :D_MEMQ]), bf(w_mem_kv[layer][:, D_MEMQ:]),
                         mem_tokens=mem_tokens)
        if kind == 0:
            w = w_in_fox[slot]
            wf_t = jnp.zeros((HEAD_PAD, D_MODEL), jnp.float32).at[:N_MIX_HEADS].set(
                w[:, 3 * D_MIX:3 * D_MIX + N_MIX_HEADS].T)
            q, k, v, qm, f_t = _in_proj(
                h, g_mix, bf(w[:, :D_MIX]), bf(w[:, D_MIX:2 * D_MIX]),
                bf(w[:, 2 * D_MIX:3 * D_MIX]), bf(w[:, 3 * D_MIX + N_MIX_HEADS:]),
                wf_t=bf(wf_t), seq=seq)
            bias = jnp.zeros((HEAD_PAD, 1), jnp.float32).at[:N_MIX_HEADS, 0].set(
                b_forget[slot])
            c = _forget_cumsum(f_t, bias, batch=batch, seq=seq)
            mix_inputs = [_fox_attention(q, k, v, c, batch=batch, seq=seq)]
            n_branches = 0
        else:
            w = w_in_dil[slot]
            q, k, v, qm = _in_proj(
                h, g_mix, bf(w[:, :D_MIX]), bf(w[:, D_MIX:2 * D_MIX]),
                bf(w[:, 2 * D_MIX:3 * D_MIX]), bf(w[:, 3 * D_MIX:]),
                rope_tabs=rope_tabs, seq=seq)
            branches = [_dilated_branch(q, k, v, d, batch=batch, seq=seq)
                        for d in DILATIONS]
            mix_inputs = [o for o, _ in branches] + [l for _, l in branches]
            n_branches = len(DILATIONS)
        h = _out_proj(h, mix_inputs, qm, km, vm,
                      bf(w_out[layer][:D_MIX]), bf(w_out[layer][D_MIX:]),
                      seq=seq, mem_tokens=mem_tokens, n_branches=n_branches)
        h = _ffn(h, norm_ffn[layer].reshape(1, D_MODEL), bf(w_up[layer]), conv_w[layer],
                 conv_b[layer].reshape(1, 2 * D_FF), bf(w_down[layer]), g_final,
                 seq=seq, final_norm=(layer == depth - 1))
    return h.reshape(batch, seq, D_MODEL)
```

```python
import functools

import jax
import jax.numpy as jnp
from jax import lax
from jax.experimental import pallas as pl
from jax.experimental.pallas import tpu as pltpu

D_MODEL = 1024
HEAD_DIM = 64
N_MEM_HEADS = 4
N_MIX_HEADS = 12
D_MIX = N_MIX_HEADS * HEAD_DIM
D_MEMQ = N_MEM_HEADS * HEAD_DIM
D_FF = 2816
CONV_WIDTH = 3
ROPE_THETA = 10000.0
DIL_BLOCK = 128
DILATIONS = (1, 4, 16)
NORM_EPS = 1e-6
NEG = -1e30
SCALE = HEAD_DIM ** -0.5

LOG2E = 1.4426950408889634

LANES = 128
N_PAIRS = D_MIX // LANES
VMEM_LIMIT = 48 * 1024 * 1024

ROW_TILE = 512
FOX_Q_TILE = 2048
FOX_K_TILE = 512
FF_CHUNK = 256
DECAY_PIECES = 3
DECAY_LANES_PER_HEAD = 2 * DECAY_PIECES

_NT = (((1,), (1,)), ((), ()))


def _cparams(n_axes):
    return pltpu.CompilerParams(
        dimension_semantics=("arbitrary",) * n_axes,
        vmem_limit_bytes=VMEM_LIMIT)


def _rms(x, g):
    y = x * lax.rsqrt(jnp.mean(x * x, axis=-1, keepdims=True) + NORM_EPS)
    return y * g


def _lane_first_half():
    lane = lax.broadcasted_iota(jnp.int32, (1, LANES), 1)
    return lane < HEAD_DIM


def _split3(x):
    hi = x.astype(jnp.bfloat16)
    rem = x - hi.astype(jnp.float32)
    mid = rem.astype(jnp.bfloat16)
    lo = (rem - mid.astype(jnp.float32)).astype(jnp.bfloat16)
    return hi, mid, lo


def _in_proj_kernel(*refs, rope, forget, tiles_per_seq):
    x_ref, g_ref, wq_ref, wk_ref, wv_ref, wm_ref = refs[:6]
    pos = 6
    if forget:
        wf_ref, bf_ref, tri_ref, sel_ref, ones_ref = refs[pos:pos + 5]; pos += 5
    if rope:
        cos_ref, sin_ref = refs[pos:pos + 2]; pos += 2
    q_ref, k_ref, v_ref, qm_ref = refs[pos:pos + 4]; pos += 4
    if forget:
        qx_ref, kx_ref, carry_ref = refs[pos:pos + 3]

    xn = _rms(x_ref[...], g_ref[...]).astype(jnp.bfloat16)
    q_scale = SCALE * LOG2E if forget else SCALE

    def rotate(t):
        lane = lax.broadcasted_iota(jnp.int32, (1, LANES), 1)
        first = (lane % HEAD_DIM) < (HEAD_DIM // 2)
        partner = jnp.where(first, pltpu.roll(t, LANES - HEAD_DIM // 2, 1),
                            pltpu.roll(t, HEAD_DIM // 2, 1))
        return t * cos_ref[...] + partner * sin_ref[...]

    q = jnp.dot(xn, wq_ref[...], preferred_element_type=jnp.float32)
    k = jnp.dot(xn, wk_ref[...], preferred_element_type=jnp.float32)
    if rope:
        for p in range(N_PAIRS):
            cols = slice(p * LANES, (p + 1) * LANES)
            q_ref[:, cols] = (rotate(q[:, cols]) * SCALE).astype(jnp.bfloat16)
            k_ref[:, cols] = rotate(k[:, cols]).astype(jnp.bfloat16)
    else:
        q_ref[...] = (q * q_scale).astype(jnp.bfloat16)
        k_ref[...] = k.astype(jnp.bfloat16)
    v_ref[...] = jnp.dot(xn, wv_ref[...],
                         preferred_element_type=jnp.float32).astype(jnp.bfloat16)
    qm = jnp.dot(xn, wm_ref[...], preferred_element_type=jnp.float32)
    qm_ref[...] = (qm * SCALE).astype(jnp.bfloat16)
    if forget:
        @pl.when(pl.program_id(0) % tiles_per_seq == 0)
        def _():
            carry_ref[...] = jnp.zeros(carry_ref.shape, jnp.float32)

        f = jnp.dot(xn, wf_ref[...], preferred_element_type=jnp.float32) + bf_ref[...]
        log_f = jnp.minimum(f, 0.0) - jnp.log1p(jnp.exp(-jnp.abs(f)))
        tri = tri_ref[...]
        c = carry_ref[0:1, :]
        for piece in _split3(log_f):
            c = c + jnp.dot(tri, piece, preferred_element_type=jnp.float32)
        carry_ref[0:1, :] = c[c.shape[0] - 1:, :]
        x = ones_ref[...]
        for idx, piece in enumerate(_split3(c * LOG2E)):
            x = x + jnp.dot(piece, sel_ref[idx], preferred_element_type=jnp.float32)
        qx_ref[...] = x[:, :D_MIX].astype(jnp.bfloat16)
        kx_ref[...] = x[:, D_MIX:].astype(jnp.bfloat16)


def _decay_selectors():
    sel = jnp.zeros((DECAY_PIECES, LANES, 2 * D_MIX), jnp.float32)
    ones = jnp.zeros((1, 2 * D_MIX), jnp.float32)
    for h in range(N_MIX_HEADS):
        base = (h // 2) * LANES + (h % 2) * DECAY_LANES_PER_HEAD
        for piece in range(DECAY_PIECES):
            sel = sel.at[piece, h, base + piece].set(1.0)
            sel = sel.at[piece, h, D_MIX + base + DECAY_PIECES + piece].set(-1.0)
            ones = ones.at[0, base + DECAY_PIECES + piece].set(1.0)
            ones = ones.at[0, D_MIX + base + piece].set(1.0)
    return sel.astype(jnp.bfloat16), ones


def _in_proj(h, g, wq, wk, wv, wm, forget_params=None, rope_tabs=None, *, seq):
    t = h.shape[0]
    tm = ROW_TILE
    forget = forget_params is not None
    rope = rope_tabs is not None
    row = lambda i: (i, 0)
    fixed = lambda i: (0, 0)
    in_specs = [pl.BlockSpec((tm, D_MODEL), row),
                pl.BlockSpec((1, D_MODEL), fixed),
                pl.BlockSpec((D_MODEL, D_MIX), fixed),
                pl.BlockSpec((D_MODEL, D_MIX), fixed),
                pl.BlockSpec((D_MODEL, D_MIX), fixed),
                pl.BlockSpec((D_MODEL, D_MEMQ), fixed)]
    args = [h, g, wq, wk, wv, wm]
    tiles_per_seq = seq // tm
    scratch = []
    if forget:
        w_f, b_f = forget_params
        sel, ones = _decay_selectors()
        r = lax.broadcasted_iota(jnp.int32, (tm, tm), 0)
        c = lax.broadcasted_iota(jnp.int32, (tm, tm), 1)
        tri = (c <= r).astype(jnp.bfloat16)
        in_specs += [pl.BlockSpec((D_MODEL, LANES), fixed),
                     pl.BlockSpec((1, LANES), fixed),
                     pl.BlockSpec((tm, tm), fixed),
                     pl.BlockSpec((DECAY_PIECES, LANES, 2 * D_MIX), lambda i: (0, 0, 0)),
                     pl.BlockSpec((1, 2 * D_MIX), fixed)]
        args += [w_f, b_f, tri, sel, ones]
        scratch.append(pltpu.VMEM((8, LANES), jnp.float32))
    if rope:
        tab = lambda i: (i % tiles_per_seq, 0)
        in_specs += [pl.BlockSpec((tm, LANES), tab), pl.BlockSpec((tm, LANES), tab)]
        args += list(rope_tabs)
    out_shape =[jax.ShapeDtypeStruct((t, D_MIX), jnp.bfloat16)] * 3
    out_shape.append(jax.ShapeDtypeStruct((t, D_MEMQ), jnp.bfloat16))
    out_specs = [pl.BlockSpec((tm, D_MIX), row)] * 3
    out_specs.append(pl.BlockSpec((tm, D_MEMQ), row))
    if forget:
        out_shape += [jax.ShapeDtypeStruct((t, D_MIX), jnp.bfloat16)] * 2
        out_specs += [pl.BlockSpec((tm, D_MIX), row)] * 2
    return pl.pallas_call(
        functools.partial(_in_proj_kernel, rope=rope, forget=forget,
                          tiles_per_seq=tiles_per_seq),
        grid=(t // tm,),
        in_specs=in_specs, out_specs=out_specs, out_shape=out_shape,
        scratch_shapes=scratch,
        compiler_params=_cparams(1),
        name="in_proj_rope" if rope else "in_proj_fox",
    )(*args)


def _mem_kv_kernel(x_ref, g_ref, wk_ref, wv_ref, km_ref, vm_ref):
    xn = _rms(x_ref[...], g_ref[...]).astype(jnp.bfloat16)
    km_ref[...] = jnp.dot(xn, wk_ref[...],
                          preferred_element_type=jnp.float32).astype(jnp.bfloat16)
    vm_ref[...] = jnp.dot(xn, wv_ref[...],
                          preferred_element_type=jnp.float32).astype(jnp.bfloat16)


def _mem_kv(mem2d, g, wk, wv, *, mem_tokens):
    rows = mem2d.shape[0]
    row = lambda i: (i, 0)
    fixed = lambda i: (0, 0)
    return pl.pallas_call(
        _mem_kv_kernel,
        grid=(rows // mem_tokens,),
        in_specs=[pl.BlockSpec((mem_tokens, D_MODEL), row),
                  pl.BlockSpec((1, D_MODEL), fixed),
                  pl.BlockSpec((D_MODEL, D_MEMQ), fixed),
                  pl.BlockSpec((D_MODEL, D_MEMQ), fixed)],
        out_specs=[pl.BlockSpec((mem_tokens, D_MEMQ), row)] * 2,
        out_shape=[jax.ShapeDtypeStruct((rows, D_MEMQ), jnp.bfloat16)] * 2,
        compiler_params=_cparams(1),
        name="mem_kv",
    )(mem2d, g, wk, wv)


def _fox_kernel(q_ref, qx_ref, k_ref, kx_ref, v_ref, o_ref, m_scr, l_scr, acc_scr):
    tq, tk = FOX_Q_TILE, FOX_K_TILE
    n_blocks = tq // tk
    qi = pl.program_id(2)
    lane = lax.broadcasted_iota(jnp.int32, (1, LANES), 1)
    first = lane < HEAD_DIM
    head_lanes = (first, jnp.logical_not(first))
    decay_lanes = (lane < DECAY_LANES_PER_HEAD,
                   (lane >= DECAY_LANES_PER_HEAD) & (lane < 2 * DECAY_LANES_PER_HEAD))
    q = q_ref[...]
    qx = qx_ref[...]
    zero = jnp.zeros_like(q)
    q_heads = [jnp.concatenate([jnp.where(head_lanes[hh], q, zero),
                                jnp.where(decay_lanes[hh], qx, zero)], axis=1)
               for hh in range(2)]

    m_scr[...] = jnp.full(m_scr.shape, NEG, jnp.float32)
    l_scr[...] = jnp.zeros(l_scr.shape, jnp.float32)
    acc_scr[...] = jnp.zeros(acc_scr.shape, jnp.float32)
    ones = jnp.ones((tk, LANES), jnp.bfloat16)

    def load_kv(j):
        start = pl.multiple_of(j * tk, tk)
        k = jnp.concatenate([k_ref[pl.ds(start, tk), :], kx_ref[pl.ds(start, tk), :]],
                            axis=1)
        v = jnp.concatenate([v_ref[pl.ds(start, tk), :], ones], axis=1)
        return k, v

    def chain(hh, r, k, v, diagonal):
        rows = slice(r * tk, (r + 1) * tk)
        s = lax.dot_general(q_heads[hh][rows], k, _NT, preferred_element_type=jnp.float32)
        if diagonal:
            row = lax.broadcasted_iota(jnp.int32, (tk, tk), 0)
            col = lax.broadcasted_iota(jnp.int32, (tk, tk), 1)
            s = jnp.where(col <= row, s, NEG)
        m_prev = m_scr[hh, rows, :]
        m_new = jnp.maximum(m_prev, jnp.max(s, axis=-1, keepdims=True))
        alpha = jnp.exp2(m_prev - m_new)
        p = jnp.exp2(s - jnp.concatenate([m_new] * (tk // LANES), axis=1))
        pv = jnp.dot(p.astype(jnp.bfloat16), v, preferred_element_type=jnp.float32)
        l_scr[hh, rows, :] = alpha * l_scr[hh, rows, :] + pv[:, LANES:]
        acc_scr[hh, rows, :] = alpha * acc_scr[hh, rows, :] + pv[:, :LANES]
        m_scr[hh, rows, :] = m_new

    def body(j, carry):
        k, v = load_kv(j)
        for r in range(n_blocks):
            for hh in range(2):
                chain(hh, r, k, v, False)
        return carry

    n_before = qi * n_blocks
    lax.fori_loop(0, n_before, body, 0)
    for d in range(n_blocks):
        k, v = load_kv(n_before + d)
        for r in range(d, n_blocks):
            for hh in range(2):
                chain(hh, r, k, v, r == d)
    o0 = acc_scr[0] * (1.0 / l_scr[0])
    o1 = acc_scr[1] * (1.0 / l_scr[1])
    o_ref[...] = jnp.where(first, o0, o1).astype(o_ref.dtype)


def _fox_attention(q, qx, k, kx, v, *, batch, seq):
    tq = FOX_Q_TILE
    as_seq = lambda a: a.reshape(batch, seq, D_MIX)
    q_spec = pl.BlockSpec((None, tq, LANES), lambda b, p, i: (b, i, p))
    kv_spec = pl.BlockSpec((None, seq, LANES), lambda b, p, i: (b, 0, p))
    out = pl.pallas_call(
        _fox_kernel,
        grid=(batch, N_PAIRS, seq // tq),
        in_specs=[q_spec, q_spec, kv_spec, kv_spec, kv_spec],
        out_specs=q_spec,
        out_shape=jax.ShapeDtypeStruct((batch, seq, D_MIX), jnp.bfloat16),
        scratch_shapes=[pltpu.VMEM((2, tq, LANES), jnp.float32),
                        pltpu.VMEM((2, tq, LANES), jnp.float32),
                        pltpu.VMEM((2, tq, LANES), jnp.float32)],
        compiler_params=_cparams(3),
        name="fox_attention",
    )(as_seq(q), as_seq(qx), as_seq(k), as_seq(kx), as_seq(v))
    return out.reshape(batch * seq, D_MIX)


def _dilated_kernel(q_ref, kp_ref, kc_ref, vp_ref, vc_ref, o_ref, lse_ref):
    blk = DIL_BLOCK
    n = pl.program_id(2)
    first = _lane_first_half()
    row = lax.broadcasted_iota(jnp.int32, (blk, 2 * blk), 0)
    col = lax.broadcasted_iota(jnp.int32, (blk, 2 * blk), 1)
    dist = row + blk - col
    first_key = jnp.where(n > 0, 0, blk)
    valid = (dist >= 0) & (dist <= blk) & (col >= first_key)
    for p in range(N_PAIRS):
        cols = slice(p * LANES, (p + 1) * LANES)
        q = q_ref[:, cols]
        zero = jnp.zeros_like(q)
        k2 = jnp.concatenate([kp_ref[:, cols], kc_ref[:, cols]], axis=0)
        v2 = jnp.concatenate([vp_ref[:, cols], vc_ref[:, cols]], axis=0)
        outs, lses = [], []
        for hh in range(2):
            qh = jnp.where(first, q, zero) if hh == 0 else jnp.where(first, zero, q)
            s = lax.dot_general(qh, k2, _NT, preferred_element_type=jnp.float32)
            s = jnp.where(valid, s, NEG)
            m = jnp.max(s, axis=-1, keepdims=True)
            e = jnp.exp(s - m)
            den = jnp.sum(e, axis=-1, keepdims=True)
            pn = (e * (1.0 / den)).astype(jnp.bfloat16)
            outs.append(jnp.dot(pn, v2, preferred_element_type=jnp.float32))
            lses.append(m + jnp.log(den))
        o_ref[:, cols] = jnp.where(first, outs[0], outs[1])
        lse_ref[:, cols] = jnp.where(first, lses[0], lses[1])


def _dilated_branch(q, k, v, dilation, *, batch, seq):
    blk = DIL_BLOCK
    n_rows = seq // dilation
    n_blocks = n_rows // blk
    view = lambda a: a.reshape(batch, n_rows, dilation * D_MIX)
    cur = lambda b, r, n: (b, n, r)
    prev = lambda b, r, n: (b, jnp.maximum(n - 1, 0), r)
    spec = lambda f: pl.BlockSpec((None, blk, D_MIX), f)
    o, lse = pl.pallas_call(
        _dilated_kernel,
        grid=(batch, dilation, n_blocks),
        in_specs=[spec(cur), spec(prev), spec(cur), spec(prev), spec(cur)],
        out_specs=[spec(cur), spec(cur)],
        out_shape=[jax.ShapeDtypeStruct((batch, n_rows, dilation * D_MIX),
                                        jnp.float32)] * 2,
        compiler_params=_cparams(3),
        name=f"dilated_d{dilation}",
    )(view(q), view(k), view(k), view(v), view(v))
    t = batch * seq
    return o.reshape(t, D_MIX), lse.reshape(t, D_MIX)


def _out_proj_kernel(*refs, n_branches):
    h_ref = refs[0]
    pos = 1
    if n_branches:
        o_refs = refs[pos:pos + n_branches]; pos += n_branches
        lse_refs = refs[pos:pos + n_branches]; pos += n_branches
    else:
        mix_ref = refs[pos]; pos += 1
    qm_ref, km_ref, vm_ref, wmix_ref, wmem_ref, out_ref = refs[pos:pos + 6]

    if n_branches:
        lses = [r[...] for r in lse_refs]
        m = lses[0]
        for l in lses[1:]:
            m = jnp.maximum(m, l)
        es = [jnp.exp(l - m) for l in lses]
        den = es[0]
        for e in es[1:]:
            den = den + e
        inv = 1.0 / den
        mix = (es[0] * inv) * o_refs[0][...]
        for e, o_ref in zip(es[1:], o_refs[1:]):
            mix = mix + (e * inv) * o_ref[...]
        mix = mix.astype(jnp.bfloat16)
    else:
        mix = mix_ref[...]

    first = _lane_first_half()
    mem_pairs = []
    for p in range(D_MEMQ // LANES):
        cols = slice(p * LANES, (p + 1) * LANES)
        qm = qm_ref[:, cols]
        zero = jnp.zeros_like(qm)
        km = km_ref[:, cols]
        vm = vm_ref[:, cols]
        outs = []
        for hh in range(2):
            qh = jnp.where(first, qm, zero) if hh == 0 else jnp.where(first, zero, qm)
            s = lax.dot_general(qh, km, _NT, preferred_element_type=jnp.float32)
            mx = jnp.max(s, axis=-1, keepdims=True)
            e = jnp.exp(s - mx)
            pn = (e * (1.0 / jnp.sum(e, axis=-1, keepdims=True))).astype(jnp.bfloat16)
            outs.append(jnp.dot(pn, vm, preferred_element_type=jnp.float32))
        mem_pairs.append(jnp.where(first, outs[0], outs[1]).astype(jnp.bfloat16))
    mem_out = jnp.concatenate(mem_pairs, axis=-1)

    y = jnp.dot(mix, wmix_ref[...], preferred_element_type=jnp.float32)
    y = y + jnp.dot(mem_out, wmem_ref[...], preferred_element_type=jnp.float32)
    out_ref[...] = h_ref[...] + y


def _out_proj(h, mix_inputs, qm, km, vm, w_mix, w_mem, *, seq, mem_tokens, n_branches):
    t = h.shape[0]
    tm = ROW_TILE
    tiles_per_seq = seq // tm
    row = lambda i: (i, 0)
    fixed = lambda i: (0, 0)
    per_batch = lambda i: (i // tiles_per_seq, 0)
    mix_dtype_cols = [pl.BlockSpec((tm, D_MIX), row)] * len(mix_inputs)
    in_specs = ([pl.BlockSpec((tm, D_MODEL), row)] + mix_dtype_cols +
                [pl.BlockSpec((tm, D_MEMQ), row),
                 pl.BlockSpec((mem_tokens, D_MEMQ), per_batch),
                 pl.BlockSpec((mem_tokens, D_MEMQ), per_batch),
                 pl.BlockSpec((D_MIX, D_MODEL), fixed),
                 pl.BlockSpec((D_MEMQ, D_MODEL), fixed)])
    return pl.pallas_call(
        functools.partial(_out_proj_kernel, n_branches=n_branches),
        grid=(t // tm,),
        in_specs=in_specs,
        out_specs=pl.BlockSpec((tm, D_MODEL), row),
        out_shape=jax.ShapeDtypeStruct((t, D_MODEL), jnp.float32),
        compiler_params=_cparams(1),
        name="out_proj_merge" if n_branches else "out_proj",
    )(h, *mix_inputs, qm, km, vm, w_mix, w_mem)


def _ffn_kernel(h_ref, g_ref, wup_ref, cw_ref, cb_ref, wdown_ref, gfin_ref, out_ref,
                carry_ref, *, tiles_per_seq, final_norm):
    tm = ROW_TILE
    halo = 8
    i = pl.program_id(0)

    @pl.when(i % tiles_per_seq == 0)
    def _():
        carry_ref[...] = jnp.zeros(carry_ref.shape, jnp.float32)

    h = h_ref[...]
    xn = _rms(h, g_ref[...]).astype(jnp.bfloat16)
    row = lax.broadcasted_iota(jnp.int32, (tm, FF_CHUNK), 0)

    def conv(u, cols):
        prev = carry_ref[:, cols]
        back1 = jnp.where(row == 0, prev[halo - 1:halo, :], pltpu.roll(u, 1, 0))
        back2 = jnp.where(row == 0, prev[halo - 2:halo - 1, :],
                          jnp.where(row == 1, prev[halo - 1:halo, :], pltpu.roll(u, 2, 0)))
        carry_ref[:, cols] = u[tm - halo:tm, :]
        w = cw_ref[:, cols]
        c = cb_ref[:, cols] + w[0:1, :] * back2
        c = c + w[1:2, :] * back1
        return c + w[2:3, :] * u

    acc = jnp.zeros((tm, D_MODEL), jnp.float32)
    for j in range(D_FF // FF_CHUNK):
        vcols = slice(j * FF_CHUNK, (j + 1) * FF_CHUNK)
        gcols = slice(D_FF + j * FF_CHUNK, D_FF + (j + 1) * FF_CHUNK)
        val = conv(jnp.dot(xn, wup_ref[:, vcols], preferred_element_type=jnp.float32), vcols)
        gate = conv(jnp.dot(xn, wup_ref[:, gcols], preferred_element_type=jnp.float32), gcols)
        act = (gate * (1.0 / (1.0 + jnp.exp(-gate))) * val).astype(jnp.bfloat16)
        acc = acc + jnp.dot(act, wdown_ref[vcols, :], preferred_element_type=jnp.float32)
    y = h + acc
    if final_norm:
        y = _rms(y, gfin_ref[...])
    out_ref[...] = y


def _ffn(h, g, w_up, conv_w, conv_b, w_down, g_final, *, seq, final_norm):
    t = h.shape[0]
    tm = ROW_TILE
    row = lambda i: (i, 0)
    fixed = lambda i: (0, 0)
    once = dict(pipeline_mode=pl.Buffered(1))
    return pl.pallas_call(
        functools.partial(_ffn_kernel, tiles_per_seq=seq // tm, final_norm=final_norm),
        grid=(t // tm,),
        in_specs=[pl.BlockSpec((tm, D_MODEL), row),
                  pl.BlockSpec((1, D_MODEL), fixed),
                  pl.BlockSpec((D_MODEL, 2 * D_FF), fixed, **once),
                  pl.BlockSpec((CONV_WIDTH, 2 * D_FF), fixed),
                  pl.BlockSpec((1, 2 * D_FF), fixed),
                  pl.BlockSpec((D_FF, D_MODEL), fixed, **once),
                  pl.BlockSpec((1, D_MODEL), fixed)],
        out_specs=pl.BlockSpec((tm, D_MODEL), row),
        out_shape=jax.ShapeDtypeStruct((t, D_MODEL), jnp.float32),
        scratch_shapes=[pltpu.VMEM((8, 2 * D_FF), jnp.float32)],
        compiler_params=_cparams(1),
        name="conv_ffn_final" if final_norm else "conv_ffn",
    )(h, g, w_up, conv_w, conv_b, w_down, g_final)


def _rope_tables(seq):
    inv = 1.0 / (ROPE_THETA ** (jnp.arange(0, HEAD_DIM, 2, dtype=jnp.float32) / HEAD_DIM))
    ang = jnp.arange(seq, dtype=jnp.float32)[:, None] * inv[None, :]
    cos, sin = jnp.cos(ang), jnp.sin(ang)
    reps = LANES // HEAD_DIM
    cos_t = jnp.tile(jnp.concatenate([cos, cos], axis=-1), (1, reps))
    sin_t = jnp.tile(jnp.concatenate([-sin, sin], axis=-1), (1, reps))
    return cos_t, sin_t


def kernel(x, mem, norm_mix, norm_mem, norm_ffn, w_in_fox, b_forget, w_in_dil,
           w_mem_kv, w_out, w_up, conv_w, conv_b, w_down, norm_final):
    batch, seq, _ = x.shape
    mem_tokens = mem.shape[1]
    depth = norm_mix.shape[0]
    bf = lambda a: a.astype(jnp.bfloat16)
    h = x.reshape(batch * seq, D_MODEL)
    mem2d = mem.reshape(batch * mem_tokens, D_MODEL)
    rope_tabs = _rope_tables(seq)
    g_final = norm_final.reshape(1, D_MODEL)

    for layer in range(depth):
        kind, slot = layer % 2, layer // 2
        g_mix = norm_mix[layer].reshape(1, D_MODEL)
        km, vm = _mem_kv(mem2d, norm_mem[layer].reshape(1, D_MODEL),
                         bf(w_mem_kv[layer][:, :D_MEMQ]), bf(w_mem_kv[layer][:, D_MEMQ:]),
                         mem_tokens=mem_tokens)
        if kind == 0:
            w = w_in_fox[slot]
            pad = ((0, 0), (0, LANES - N_MIX_HEADS))
            w_f = jnp.pad(w[:, 3 * D_MIX:3 * D_MIX + N_MIX_HEADS], pad)
            b_f = jnp.pad(b_forget[slot].reshape(1, N_MIX_HEADS), pad)
            q, k, v, qm, qx, kx = _in_proj(
                h, g_mix, bf(w[:, :D_MIX]), bf(w[:, D_MIX:2 * D_MIX]),
                bf(w[:, 2 * D_MIX:3 * D_MIX]), bf(w[:, 3 * D_MIX + N_MIX_HEADS:]),
                forget_params=(bf(w_f), b_f), seq=seq)
            mix_inputs = [_fox_attention(q, qx, k, kx, v, batch=batch, seq=seq)]
            n_branches = 0
        else:
            w = w_in_dil[slot]
            q, k, v, qm = _in_proj(
                h, g_mix, bf(w[:, :D_MIX]), bf(w[:, D_MIX:2 * D_MIX]),
                bf(w[:, 2 * D_MIX:3 * D_MIX]), bf(w[:, 3 * D_MIX:]),
                rope_tabs=rope_tabs, seq=seq)
            branches = [_dilated_branch(q, k, v, d, batch=batch, seq=seq)
                        for d in DILATIONS]
            mix_inputs = [o for o, _ in branches] + [l for _, l in branches]
            n_branches = len(DILATIONS)
        h = _out_proj(h, mix_inputs, qm, km, vm,
                      bf(w_out[layer][:D_MIX]), bf(w_out[layer][D_MIX:]),
                      seq=seq, mem_tokens=mem_tokens, n_branches=n_branches)
        h = _ffn(h, norm_ffn[layer].reshape(1, D_MODEL), bf(w_up[layer]), conv_w[layer],
                 conv_b[layer].reshape(1, 2 * D_FF), bf(w_down[layer]), g_final,
                 seq=seq, final_norm=(layer == depth - 1))
    return h.reshape(batch, seq, D_MODEL)
```

```python
import functools

import numpy as np

import jax
import jax.numpy as jnp
from jax import lax
from jax.experimental import pallas as pl
from jax.experimental.pallas import tpu as pltpu

D_MODEL = 1024
HEAD_DIM = 64
N_MEM_HEADS = 4
N_MIX_HEADS = 12
D_MIX = N_MIX_HEADS * HEAD_DIM
D_MEMQ = N_MEM_HEADS * HEAD_DIM
D_FF = 2816
CONV_WIDTH = 3
ROPE_THETA = 10000.0
DIL_BLOCK = 128
NORM_EPS = 1e-6
NEG = -1e30
SCALE = HEAD_DIM ** -0.5
LOG2E = 1.4426950408889634

LANES = 128
N_PAIRS = D_MIX // LANES
VMEM_LIMIT = 48 * 1024 * 1024

ROW_TILE = 512
FOX_Q_TILE = 2048
FOX_K_TILE = 512
FF_CHUNK = 256
DECAY_PIECES = 3
DECAY_LANES_PER_HEAD = 2 * DECAY_PIECES
DIL_CHUNK = 2048
DIL_RES = 16
DIL_RUN = DIL_CHUNK // DIL_RES
RUN_PER_TILE = ROW_TILE // DIL_RES

_NT = (((1,), (1,)), ((), ()))


def _cparams(n_axes):
    return pltpu.CompilerParams(
        dimension_semantics=("arbitrary",) * n_axes,
        vmem_limit_bytes=VMEM_LIMIT)


def _rms(x, g):
    y = x * lax.rsqrt(jnp.mean(x * x, axis=-1, keepdims=True) + NORM_EPS)
    return y * g


def _lane_first_half():
    lane = lax.broadcasted_iota(jnp.int32, (1, LANES), 1)
    return lane < HEAD_DIM


def _split3(x):
    hi = x.astype(jnp.bfloat16)
    rem = x - hi.astype(jnp.float32)
    mid = rem.astype(jnp.bfloat16)
    lo = (rem - mid.astype(jnp.float32)).astype(jnp.bfloat16)
    return hi, mid, lo


def _in_proj_kernel(*refs, rope, forget, tiles_per_seq):
    x_ref, g_ref, wq_ref, wk_ref, wv_ref, wm_ref = refs[:6]
    pos = 6
    if forget:
        wf_ref, bf_ref, tri_ref, sel_ref, ones_ref = refs[pos:pos + 5]; pos += 5
    if rope:
        cos_ref, sin_ref = refs[pos:pos + 2]; pos += 2
    q_ref, k_ref, v_ref, qm_ref = refs[pos:pos + 4]; pos += 4
    if forget:
        qx_ref, kx_ref, carry_ref = refs[pos:pos + 3]
    if rope:
        stage_ref = refs[pos]

    xn = _rms(x_ref[...], g_ref[...]).astype(jnp.bfloat16)
    q_scale = SCALE * LOG2E

    def rotate(t):
        lane = lax.broadcasted_iota(jnp.int32, (1, LANES), 1)
        first = (lane % HEAD_DIM) < (HEAD_DIM // 2)
        partner = jnp.where(first, pltpu.roll(t, LANES - HEAD_DIM // 2, 1),
                            pltpu.roll(t, HEAD_DIM // 2, 1))
        return t * cos_ref[...] + partner * sin_ref[...]

    def emit_residue_major(y, out_ref, fn):
        n = y.shape[1] // LANES
        for p in range(n):
            stage_ref[p] = fn(y[:, p * LANES:(p + 1) * LANES])
        for r in range(DIL_RES):
            for p in range(n):
                out_ref[r, :, p * LANES:(p + 1) * LANES] = stage_ref[
                    p, pl.ds(r, RUN_PER_TILE, stride=DIL_RES), :].astype(jnp.bfloat16)

    q = jnp.dot(xn, wq_ref[...], preferred_element_type=jnp.float32)
    k = jnp.dot(xn, wk_ref[...], preferred_element_type=jnp.float32)
    v = jnp.dot(xn, wv_ref[...], preferred_element_type=jnp.float32)
    qm = jnp.dot(xn, wm_ref[...], preferred_element_type=jnp.float32)
    if rope:
        emit_residue_major(q, q_ref, lambda t: rotate(t) * q_scale)
        emit_residue_major(k, k_ref, rotate)
        emit_residue_major(v, v_ref, lambda t: t)
        emit_residue_major(qm, qm_ref, lambda t: t * SCALE)
    else:
        q_ref[...] = (q * q_scale).astype(jnp.bfloat16)
        k_ref[...] = k.astype(jnp.bfloat16)
        v_ref[...] = v.astype(jnp.bfloat16)
        qm_ref[...] = (qm * SCALE).astype(jnp.bfloat16)
    if forget:
        @pl.when(pl.program_id(0) % tiles_per_seq == 0)
        def _():
            carry_ref[...] = jnp.zeros(carry_ref.shape, jnp.float32)

        f = jnp.dot(xn, wf_ref[...], preferred_element_type=jnp.float32) + bf_ref[...]
        log_f = jnp.minimum(f, 0.0) - jnp.log1p(jnp.exp(-jnp.abs(f)))
        tri = tri_ref[...]
        c = carry_ref[0:1, :]
        for piece in _split3(log_f):
            c = c + jnp.dot(tri, piece, preferred_element_type=jnp.float32)
        carry_ref[0:1, :] = c[c.shape[0] - 1:, :]
        x = ones_ref[...]
        for idx, piece in enumerate(_split3(c * LOG2E)):
            x = x + jnp.dot(piece, sel_ref[idx], preferred_element_type=jnp.float32)
        qx_ref[...] = x[:, :LANES].astype(jnp.bfloat16)
        kx_ref[...] = x[:, LANES:].astype(jnp.bfloat16)


def _decay_selectors():
    sel = np.zeros((DECAY_PIECES, LANES, 2 * LANES), np.float32)
    ones = np.zeros((1, 2 * LANES), np.float32)
    for h in range(N_MIX_HEADS):
        base = h * DECAY_LANES_PER_HEAD
        for piece in range(DECAY_PIECES):
            sel[piece, h, base + piece] = 1.0
            sel[piece, h, LANES + base + DECAY_PIECES + piece] = -1.0
            ones[0, base + DECAY_PIECES + piece] = 1.0
            ones[0, LANES + base + piece] = 1.0
    return jnp.asarray(sel, jnp.bfloat16), jnp.asarray(ones)


def _in_proj(h, g, wq, wk, wv, wm, forget_params=None, rope_tabs=None, *, seq):
    t = h.shape[0]
    tm = ROW_TILE
    forget = forget_params is not None
    rope = rope_tabs is not None
    row = lambda i: (i, 0)
    fixed = lambda i: (0, 0)
    in_specs = [pl.BlockSpec((tm, D_MODEL), row),
                pl.BlockSpec((1, D_MODEL), fixed),
                pl.BlockSpec((D_MODEL, D_MIX), fixed),
                pl.BlockSpec((D_MODEL, D_MIX), fixed),
                pl.BlockSpec((D_MODEL, D_MIX), fixed),
                pl.BlockSpec((D_MODEL, D_MEMQ), fixed)]
    args = [h, g, wq, wk, wv, wm]
    tiles_per_seq = seq // tm
    scratch = []
    if forget:
        w_f, b_f = forget_params
        sel, ones = _decay_selectors()
        tri = jnp.asarray(np.tril(np.ones((tm, tm), np.float32)), jnp.bfloat16)
        in_specs += [pl.BlockSpec((D_MODEL, LANES), fixed),
                     pl.BlockSpec((1, LANES), fixed),
                     pl.BlockSpec((tm, tm), fixed),
                     pl.BlockSpec((DECAY_PIECES, LANES, 2 * LANES), lambda i: (0, 0, 0)),
                     pl.BlockSpec((1, 2 * LANES), fixed)]
        args += [w_f, b_f, tri, sel, ones]
        out_shape = [jax.ShapeDtypeStruct((t, D_MIX), jnp.bfloat16)] * 3
        out_shape += [jax.ShapeDtypeStruct((t, D_MEMQ), jnp.bfloat16)]
        out_shape += [jax.ShapeDtypeStruct((t, LANES), jnp.bfloat16)] * 2
        out_specs = [pl.BlockSpec((tm, D_MIX), row)] * 3
        out_specs += [pl.BlockSpec((tm, D_MEMQ), row)]
        out_specs += [pl.BlockSpec((tm, LANES), row)] * 2
        scratch.append(pltpu.VMEM((8, LANES), jnp.float32))
    if rope:
        tab = lambda i: (i % tiles_per_seq, 0)
        in_specs += [pl.BlockSpec((tm, LANES), tab), pl.BlockSpec((tm, LANES), tab)]
        args += list(rope_tabs)
        tiles_per_chunk = DIL_CHUNK // tm
        n_chunks = t // DIL_CHUNK
        rm = lambda i: (i // tiles_per_chunk, 0, i % tiles_per_chunk, 0)
        out_shape = [jax.ShapeDtypeStruct((n_chunks, DIL_RES, DIL_RUN, D_MIX),
                                          jnp.bfloat16)] * 3
        out_shape += [jax.ShapeDtypeStruct((n_chunks, DIL_RES, DIL_RUN, D_MEMQ),
                                           jnp.bfloat16)]
        out_specs = [pl.BlockSpec((None, DIL_RES, RUN_PER_TILE, D_MIX), rm)] * 3
        out_specs += [pl.BlockSpec((None, DIL_RES, RUN_PER_TILE, D_MEMQ), rm)]
        scratch.append(pltpu.VMEM((N_PAIRS, tm, LANES), jnp.float32))
    return pl.pallas_call(
        functools.partial(_in_proj_kernel, rope=rope, forget=forget,
                          tiles_per_seq=tiles_per_seq),
        grid=(t // tm,),
        in_specs=in_specs, out_specs=out_specs, out_shape=out_shape,
        scratch_shapes=scratch,
        compiler_params=_cparams(1),
        name="in_proj_rope" if rope else "in_proj_fox",
    )(*args)


def _mem_kv_kernel(x_ref, g_ref, wk_ref, wv_ref, km_ref, vm_ref):
    xn = _rms(x_ref[...], g_ref[...]).astype(jnp.bfloat16)
    km_ref[...] = jnp.dot(xn, wk_ref[...],
                          preferred_element_type=jnp.float32).astype(jnp.bfloat16)
    vm_ref[...] = jnp.dot(xn, wv_ref[...],
                          preferred_element_type=jnp.float32).astype(jnp.bfloat16)


def _mem_kv(mem2d, g, wk, wv, *, mem_tokens):
    rows = mem2d.shape[0]
    row = lambda i: (i, 0)
    fixed = lambda i: (0, 0)
    return pl.pallas_call(
        _mem_kv_kernel,
        grid=(rows // mem_tokens,),
        in_specs=[pl.BlockSpec((mem_tokens, D_MODEL), row),
                  pl.BlockSpec((1, D_MODEL), fixed),
                  pl.BlockSpec((D_MODEL, D_MEMQ), fixed),
                  pl.BlockSpec((D_MODEL, D_MEMQ), fixed)],
        out_specs=[pl.BlockSpec((mem_tokens, D_MEMQ), row)] * 2,
        out_shape=[jax.ShapeDtypeStruct((rows, D_MEMQ), jnp.bfloat16)] * 2,
        compiler_params=_cparams(1),
        name="mem_kv",
    )(mem2d, g, wk, wv)


def _fox_kernel(q_ref, qx_ref, k_ref, kx_ref, v_ref, o_ref, m_scr, l_scr, acc_scr):
    tq, tk = FOX_Q_TILE, FOX_K_TILE
    n_blocks = tq // tk
    pair = pl.program_id(1)
    qi = pl.program_id(2)
    lane = lax.broadcasted_iota(jnp.int32, (1, LANES), 1)
    first = lane < HEAD_DIM
    head_lanes = (first, jnp.logical_not(first))
    q = q_ref[...]
    qx = qx_ref[...]
    zero = jnp.zeros_like(q)
    q_heads = []
    for hh in range(2):
        lo = (2 * pair + hh) * DECAY_LANES_PER_HEAD
        own_decay = (lane >= lo) & (lane < lo + DECAY_LANES_PER_HEAD)
        q_heads.append(jnp.concatenate([jnp.where(head_lanes[hh], q, zero),
                                        jnp.where(own_decay, qx, zero)], axis=1))

    m_scr[...] = jnp.full(m_scr.shape, NEG, jnp.float32)
    l_scr[...] = jnp.zeros(l_scr.shape, jnp.float32)
    acc_scr[...] = jnp.zeros(acc_scr.shape, jnp.float32)
    ones = jnp.ones((tk, LANES), jnp.bfloat16)

    def load_kv(j):
        start = pl.multiple_of(j * tk, tk)
        k = jnp.concatenate([k_ref[pl.ds(start, tk), :], kx_ref[pl.ds(start, tk), :]],
                            axis=1)
        v = jnp.concatenate([v_ref[pl.ds(start, tk), :], ones], axis=1)
        return k, v

    def chain(hh, r, k, v, diagonal):
        rows = slice(r * tk, (r + 1) * tk)
        s = lax.dot_general(q_heads[hh][rows], k, _NT, preferred_element_type=jnp.float32)
        if diagonal:
            row = lax.broadcasted_iota(jnp.int32, (tk, tk), 0)
            col = lax.broadcasted_iota(jnp.int32, (tk, tk), 1)
            s = jnp.where(col <= row, s, NEG)
        m_prev = m_scr[hh, rows, :]
        m_new = jnp.maximum(m_prev, jnp.max(s, axis=-1, keepdims=True))
        alpha = jnp.exp2(m_prev - m_new)
        p = jnp.exp2(s - jnp.concatenate([m_new] * (tk // LANES), axis=1))
        pv = jnp.dot(p.astype(jnp.bfloat16), v, preferred_element_type=jnp.float32)
        l_scr[hh, rows, :] = alpha * l_scr[hh, rows, :] + pv[:, LANES:]
        acc_scr[hh, rows, :] = alpha * acc_scr[hh, rows, :] + pv[:, :LANES]
        m_scr[hh, rows, :] = m_new

    def body(j, carry):
        k, v = load_kv(j)
        for r in range(n_blocks):
            for hh in range(2):
                chain(hh, r, k, v, False)
        return carry

    n_before = qi * n_blocks
    lax.fori_loop(0, n_before, body, 0)
    for d in range(n_blocks):
        k, v = load_kv(n_before + d)
        for r in range(d, n_blocks):
            for hh in range(2):
                chain(hh, r, k, v, r == d)
    o0 = acc_scr[0] * (1.0 / l_scr[0])
    o1 = acc_scr[1] * (1.0 / l_scr[1])
    o_ref[...] = jnp.where(first, o0, o1).astype(o_ref.dtype)


def _fox_attention(q, qx, k, kx, v, *, batch, seq):
    tq = FOX_Q_TILE
    as_seq = lambda a: a.reshape(batch, seq, a.shape[-1])
    q_spec = pl.BlockSpec((None, tq, LANES), lambda b, p, i: (b, i, p))
    qx_spec = pl.BlockSpec((None, tq, LANES), lambda b, p, i: (b, i, 0))
    kv_spec = pl.BlockSpec((None, seq, LANES), lambda b, p, i: (b, 0, p))
    kx_spec = pl.BlockSpec((None, seq, LANES), lambda b, p, i: (b, 0, 0))
    out = pl.pallas_call(
        _fox_kernel,
        grid=(batch, N_PAIRS, seq // tq),
        in_specs=[q_spec, qx_spec, kv_spec, kx_spec, kv_spec],
        out_specs=q_spec,
        out_shape=jax.ShapeDtypeStruct((batch, seq, D_MIX), jnp.bfloat16),
        scratch_shapes=[pltpu.VMEM((2, tq, LANES), jnp.float32),
                        pltpu.VMEM((2, tq, LANES), jnp.float32),
                        pltpu.VMEM((2, tq, LANES), jnp.float32)],
        compiler_params=_cparams(3),
        name="fox_attention",
    )(as_seq(q), as_seq(qx), as_seq(k), as_seq(kx), as_seq(v))
    return out.reshape(batch * seq, D_MIX)


def _band_bias(n_q, entry_q, entry_k):
    a = np.arange(2 * n_q)[:, None] % n_q
    c = np.arange(2 * n_q)[None, :]
    dist = entry_q(a) - entry_k(c)
    ok = (dist >= 0) & (dist <= DIL_BLOCK)
    planes = [ok, ok & (c >= n_q)]
    return jnp.asarray(np.where(np.stack(planes), 0.0, NEG), jnp.float32)


def _dilated_masks():
    blk = DIL_BLOCK
    b16 = _band_bias(blk, lambda a: a + blk, lambda c: c)
    run4 = 32
    b4 = _band_bias(blk, lambda a: blk + 4 * (a % run4) + a // run4,
                    lambda c: (c // blk) * blk + 4 * ((c % blk) % run4) + (c % blk) // run4)
    n1 = 2 * blk
    b1 = _band_bias(n1, lambda a: n1 + DIL_RES * (a % 16) + a // 16,
                    lambda c: (c // n1) * n1 + DIL_RES * ((c % n1) % 16) + (c % n1) // 16)
    return b16, b4, b1


def _dilated_kernel(q_ref, kp_ref, kc_ref, vp_ref, vc_ref, b16_ref, b4_ref, b1_ref,
                    o_ref, m_scr, l_scr, acc_scr):
    blk = DIL_BLOCK
    first = _lane_first_half()
    seq_start = jnp.where(pl.program_id(1) > 0, 0, 1)

    def gather(ref, slices):
        return jnp.concatenate([ref[s, :] for s in slices], axis=0)

    def chain(q_slices, key_parts, bias, first_branch):
        q = gather(q_ref, q_slices)
        zero = jnp.zeros_like(q)
        q2 = jnp.concatenate([jnp.where(first, q, zero), jnp.where(first, zero, q)], axis=0)
        k2 = jnp.concatenate([gather(kr, sl) for kr, _, sl in key_parts], axis=0)
        v2 = jnp.concatenate([gather(vr, sl) for _, vr, sl in key_parts], axis=0)
        n_q, n_k = q.shape[0], k2.shape[0]
        v2 = jnp.concatenate([v2, jnp.ones((n_k, LANES), jnp.bfloat16)], axis=1)
        s = lax.dot_general(q2, k2, _NT, preferred_element_type=jnp.float32) + bias
        m_cur = jnp.max(s, axis=-1, keepdims=True)
        if first_branch:
            m_new = jnp.broadcast_to(m_cur, (2 * n_q, LANES))
        else:
            m_prev = jnp.concatenate([gather(m_scr.at[hh], q_slices) for hh in range(2)],
                                     axis=0)
            m_new = jnp.maximum(m_prev, m_cur)
            alpha = jnp.exp2(m_prev - m_new)
        p = jnp.exp2(s - jnp.concatenate([m_new] * (n_k // LANES), axis=1))
        pv = jnp.dot(p.astype(jnp.bfloat16), v2, preferred_element_type=jnp.float32)
        num, den = pv[:, :LANES], pv[:, LANES:]
        if not first_branch:
            l_prev = jnp.concatenate([gather(l_scr.at[hh], q_slices) for hh in range(2)],
                                     axis=0)
            a_prev = jnp.concatenate([gather(acc_scr.at[hh], q_slices) for hh in range(2)],
                                     axis=0)
            den = alpha * l_prev + den
            num = alpha * a_prev + num
        for hh in range(2):
            off = hh * n_q
            for sl in q_slices:
                n = sl.stop - sl.start
                m_scr[hh, sl, :] = m_new[off:off + n]
                l_scr[hh, sl, :] = den[off:off + n]
                acc_scr[hh, sl, :] = num[off:off + n]
                off += n

    def block(cur, run, tail_start, bias_ref, first_branch, is_first_block):
        if is_first_block:
            prv = [slice(s.start - (s.start % DIL_RUN) + tail_start,
                         s.start - (s.start % DIL_RUN) + DIL_RUN) for s in cur]
            parts = [(kp_ref, vp_ref, prv), (kc_ref, vc_ref, cur)]
            bias = bias_ref[seq_start]
        else:
            prv = [slice(s.start - run, s.stop - run) for s in cur]
            parts = [(kc_ref, vc_ref, prv), (kc_ref, vc_ref, cur)]
            bias = bias_ref[0]
        chain(cur, parts, bias, first_branch)

    for r in range(DIL_RES):
        block([slice(r * DIL_RUN, (r + 1) * DIL_RUN)], DIL_RUN, 0, b16_ref, True, True)
    run4 = blk // 4
    for r4 in range(4):
        for mb in range(DIL_RUN // run4):
            cur = [slice((r4 + 4 * u) * DIL_RUN + run4 * mb,
                         (r4 + 4 * u) * DIL_RUN + run4 * (mb + 1)) for u in range(4)]
            block(cur, run4, DIL_RUN - run4, b4_ref, False, mb == 0)
    run1 = 2 * blk // DIL_RES
    for mb in range(DIL_RUN // run1):
        cur = [slice(r * DIL_RUN + run1 * mb, r * DIL_RUN + run1 * (mb + 1))
               for r in range(DIL_RES)]
        block(cur, run1, DIL_RUN - run1, b1_ref, False, mb == 0)

    o0 = acc_scr[0] * (1.0 / l_scr[0])
    o1 = acc_scr[1] * (1.0 / l_scr[1])
    o_ref[...] = jnp.where(first, o0, o1).astype(o_ref.dtype)


def _dilated_attention(q, k, v, *, batch, seq):
    n_chunks = seq // DIL_CHUNK
    flat = lambda a: a.reshape(batch * n_chunks, DIL_CHUNK, D_MIX)
    masks = _dilated_masks()
    cur = lambda b, c, p: (b * n_chunks + c, 0, p)
    prev = lambda b, c, p: (b * n_chunks + jnp.maximum(c - 1, 0), 0, p)
    spec = lambda f: pl.BlockSpec((None, DIL_CHUNK, LANES), f)
    mask_specs = [pl.BlockSpec(m.shape, lambda b, c, p: (0, 0, 0)) for m in masks]
    out = pl.pallas_call(
        _dilated_kernel,
        grid=(batch, n_chunks, N_PAIRS),
        in_specs=[spec(cur), spec(prev), spec(cur), spec(prev), spec(cur)] + mask_specs,
        out_specs=spec(cur),
        out_shape=jax.ShapeDtypeStruct((batch * n_chunks, DIL_CHUNK, D_MIX), jnp.bfloat16),
        scratch_shapes=[pltpu.VMEM((2, DIL_CHUNK, LANES), jnp.float32)] * 3,
        compiler_params=_cparams(3),
        name="dilated_attention",
    )(flat(q), flat(k), flat(k), flat(v), flat(v), *masks)
    return out.reshape(batch * n_chunks, DIL_RES, DIL_RUN, D_MIX)


def _out_proj_kernel(*refs, residue_major):
    h_ref, mix_ref, qm_ref, km_ref, vm_ref, wmix_ref, wmem_ref, out_ref = refs[:8]
    tm = ROW_TILE
    if residue_major:
        stage_ref = refs[8]
        mix = mix_ref[...].reshape(tm, D_MIX)
        qm_all = qm_ref[...].reshape(tm, D_MEMQ)
    else:
        mix = mix_ref[...]
        qm_all = qm_ref[...]

    first = _lane_first_half()
    mem_pairs = []
    for p in range(D_MEMQ // LANES):
        cols = slice(p * LANES, (p + 1) * LANES)
        qm = qm_all[:, cols]
        zero = jnp.zeros_like(qm)
        km = km_ref[:, cols]
        vm = vm_ref[:, cols]
        outs = []
        for hh in range(2):
            qh = jnp.where(first, qm, zero) if hh == 0 else jnp.where(first, zero, qm)
            s = lax.dot_general(qh, km, _NT, preferred_element_type=jnp.float32)
            mx = jnp.max(s, axis=-1, keepdims=True)
            e = jnp.exp(s - mx)
            pn = (e * (1.0 / jnp.sum(e, axis=-1, keepdims=True))).astype(jnp.bfloat16)
            outs.append(jnp.dot(pn, vm, preferred_element_type=jnp.float32))
        mem_pairs.append(jnp.where(first, outs[0], outs[1]).astype(jnp.bfloat16))
    mem_out = jnp.concatenate(mem_pairs, axis=-1)

    y = jnp.dot(mix, wmix_ref[...], preferred_element_type=jnp.float32)
    y = y + jnp.dot(mem_out, wmem_ref[...], preferred_element_type=jnp.float32)
    if residue_major:
        n = D_MODEL // LANES
        for p in range(n):
            for r in range(DIL_RES):
                stage_ref[p, pl.ds(r, RUN_PER_TILE, stride=DIL_RES), :] = y[
                    r * RUN_PER_TILE:(r + 1) * RUN_PER_TILE, p * LANES:(p + 1) * LANES]
        for p in range(n):
            cols = slice(p * LANES, (p + 1) * LANES)
            out_ref[:, cols] = h_ref[:, cols] + stage_ref[p]
    else:
        out_ref[...] = h_ref[...] + y


def _out_proj(h, mix, qm, km, vm, w_mix, w_mem, *, seq, mem_tokens, residue_major):
    t = h.shape[0]
    tm = ROW_TILE
    tiles_per_seq = seq // tm
    row = lambda i: (i, 0)
    fixed = lambda i: (0, 0)
    per_batch = lambda i: (i // tiles_per_seq, 0)
    scratch = []
    if residue_major:
        tiles_per_chunk = DIL_CHUNK // tm
        rm = lambda i: (i // tiles_per_chunk, 0, i % tiles_per_chunk, 0)
        mix_spec = pl.BlockSpec((None, DIL_RES, RUN_PER_TILE, D_MIX), rm)
        qm_spec = pl.BlockSpec((None, DIL_RES, RUN_PER_TILE, D_MEMQ), rm)
        scratch.append(pltpu.VMEM((D_MODEL // LANES, tm, LANES), jnp.float32))
    else:
        mix_spec = pl.BlockSpec((tm, D_MIX), row)
        qm_spec = pl.BlockSpec((tm, D_MEMQ), row)
    return pl.pallas_call(
        functools.partial(_out_proj_kernel, residue_major=residue_major),
        grid=(t // tm,),
        in_specs=[pl.BlockSpec((tm, D_MODEL), row), mix_spec, qm_spec,
                  pl.BlockSpec((mem_tokens, D_MEMQ), per_batch),
                  pl.BlockSpec((mem_tokens, D_MEMQ), per_batch),
                  pl.BlockSpec((D_MIX, D_MODEL), fixed),
                  pl.BlockSpec((D_MEMQ, D_MODEL), fixed)],
        out_specs=pl.BlockSpec((tm, D_MODEL), row),
        out_shape=jax.ShapeDtypeStruct((t, D_MODEL), jnp.float32),
        scratch_shapes=scratch,
        compiler_params=_cparams(1),
        name="out_proj_residue_major" if residue_major else "out_proj",
    )(h, mix, qm, km, vm, w_mix, w_mem)


def _ffn_kernel(h_ref, g_ref, wup_ref, cw_ref, cb_ref, wdown_ref, gfin_ref, out_ref,
                carry_ref, act_ref, *, tiles_per_seq, final_norm):
    tm = ROW_TILE
    halo = 8
    n_chunks = D_FF // FF_CHUNK
    i = pl.program_id(0)

    @pl.when(i % tiles_per_seq == 0)
    def _():
        carry_ref[...] = jnp.zeros(carry_ref.shape, jnp.float32)

    h = h_ref[...]
    xn = _rms(h, g_ref[...]).astype(jnp.bfloat16)
    row = lax.broadcasted_iota(jnp.int32, (tm, FF_CHUNK), 0)

    def conv(u, cols):
        prev = carry_ref[:, cols]
        back1 = jnp.where(row == 0, prev[halo - 1:halo, :], pltpu.roll(u, 1, 0))
        back2 = jnp.where(row == 0, prev[halo - 2:halo - 1, :],
                          jnp.where(row == 1, prev[halo - 1:halo, :], pltpu.roll(u, 2, 0)))
        carry_ref[:, cols] = u[tm - halo:tm, :]
        w = cw_ref[:, cols]
        c = cb_ref[:, cols] + w[0:1, :] * back2
        c = c + w[1:2, :] * back1
        return c + w[2:3, :] * u

    def val_cols(j):
        return slice(j * FF_CHUNK, (j + 1) * FF_CHUNK)

    def gate_cols(j):
        return slice(D_FF + j * FF_CHUNK, D_FF + (j + 1) * FF_CHUNK)

    def up(j):
        return (jnp.dot(xn, wup_ref[:, val_cols(j)], preferred_element_type=jnp.float32),
                jnp.dot(xn, wup_ref[:, gate_cols(j)], preferred_element_type=jnp.float32))

    u = up(0)
    for j in range(n_chunks):
        u_next = up(j + 1) if j + 1 < n_chunks else None
        val = conv(u[0], val_cols(j))
        gate = conv(u[1], gate_cols(j))
        act_ref[:, val_cols(j)] = (gate * (1.0 / (1.0 + jnp.exp(-gate))) * val
                                   ).astype(jnp.bfloat16)
        u = u_next
    y = h + jnp.dot(act_ref[...], wdown_ref[...], preferred_element_type=jnp.float32)
    if final_norm:
        y = _rms(y, gfin_ref[...])
    out_ref[...] = y


def _ffn(h, g, w_up, conv_w, conv_b, w_down, g_final, *, seq, final_norm):
    t = h.shape[0]
    tm = ROW_TILE
    row = lambda i: (i, 0)
    fixed = lambda i: (0, 0)
    once = dict(pipeline_mode=pl.Buffered(1))
    return pl.pallas_call(
        functools.partial(_ffn_kernel, tiles_per_seq=seq // tm, final_norm=final_norm),
        grid=(t // tm,),
        in_specs=[pl.BlockSpec((tm, D_MODEL), row),
                  pl.BlockSpec((1, D_MODEL), fixed),
                  pl.BlockSpec((D_MODEL, 2 * D_FF), fixed, **once),
                  pl.BlockSpec((CONV_WIDTH, 2 * D_FF), fixed),
                  pl.BlockSpec((1, 2 * D_FF), fixed),
                  pl.BlockSpec((D_FF, D_MODEL), fixed, **once),
                  pl.BlockSpec((1, D_MODEL), fixed)],
        out_specs=pl.BlockSpec((tm, D_MODEL), row),
        out_shape=jax.ShapeDtypeStruct((t, D_MODEL), jnp.float32),
        scratch_shapes=[pltpu.VMEM((8, 2 * D_FF), jnp.float32),
                        pltpu.VMEM((tm, D_FF), jnp.bfloat16)],
        compiler_params=_cparams(1),
        name="conv_ffn_final" if final_norm else "conv_ffn",
    )(h, g, w_up, conv_w, conv_b, w_down, g_final)


def _rope_tables(seq):
    inv = 1.0 / (ROPE_THETA ** (jnp.arange(0, HEAD_DIM, 2, dtype=jnp.float32) / HEAD_DIM))
    ang = jnp.arange(seq, dtype=jnp.float32)[:, None] * inv[None, :]
    cos, sin = jnp.cos(ang), jnp.sin(ang)
    reps = LANES // HEAD_DIM
    cos_t = jnp.tile(jnp.concatenate([cos, cos], axis=-1), (1, reps))
    sin_t = jnp.tile(jnp.concatenate([-sin, sin], axis=-1), (1, reps))
    return cos_t, sin_t


def kernel(x, mem, norm_mix, norm_mem, norm_ffn, w_in_fox, b_forget, w_in_dil,
           w_mem_kv, w_out, w_up, conv_w, conv_b, w_down, norm_final):
    batch, seq, _ = x.shape
    assert seq % DIL_CHUNK == 0 and seq % FOX_Q_TILE == 0
    mem_tokens = mem.shape[1]
    depth = norm_mix.shape[0]
    bf = lambda a: a.astype(jnp.bfloat16)
    h = x.reshape(batch * seq, D_MODEL)
    mem2d = mem.reshape(batch * mem_tokens, D_MODEL)
    rope_tabs = _rope_tables(seq)
    g_final = norm_final.reshape(1, D_MODEL)

    for layer in range(depth):
        kind, slot = layer % 2, layer // 2
        g_mix = norm_mix[layer].reshape(1, D_MODEL)
        km, vm = _mem_kv(mem2d, norm_mem[layer].reshape(1, D_MODEL),
                         bf(w_mem_kv[layer][:, :D_MEMQ]), bf(w_mem_kv[layer][:, D_MEMQ:]),
                         mem_tokens=mem_tokens)
        if kind == 0:
            w = w_in_fox[slot]
            pad = ((0, 0), (0, LANES - N_MIX_HEADS))
            w_f = jnp.pad(w[:, 3 * D_MIX:3 * D_MIX + N_MIX_HEADS], pad)
            b_f = jnp.pad(b_forget[slot].reshape(1, N_MIX_HEADS), pad)
            q, k, v, qm, qx, kx = _in_proj(
                h, g_mix, bf(w[:, :D_MIX]), bf(w[:, D_MIX:2 * D_MIX]),
                bf(w[:, 2 * D_MIX:3 * D_MIX]), bf(w[:, 3 * D_MIX + N_MIX_HEADS:]),
                forget_params=(bf(w_f), b_f), seq=seq)
            mix = _fox_attention(q, qx, k, kx, v, batch=batch, seq=seq)
        else:
            w = w_in_dil[slot]
            q, k, v, qm = _in_proj(
                h, g_mix, bf(w[:, :D_MIX]), bf(w[:, D_MIX:2 * D_MIX]),
                bf(w[:, 2 * D_MIX:3 * D_MIX]), bf(w[:, 3 * D_MIX:]),
                rope_tabs=rope_tabs, seq=seq)
            mix = _dilated_attention(q, k, v, batch=batch, seq=seq)
        h = _out_proj(h, mix, qm, km, vm,
                      bf(w_out[layer][:D_MIX]), bf(w_out[layer][D_MIX:]),
                      seq=seq, mem_tokens=mem_tokens, residue_major=(kind == 1))
        h = _ffn(h, norm_ffn[layer].reshape(1, D_MODEL), bf(w_up[layer]), conv_w[layer],
                 conv_b[layer].reshape(1, 2 * D_FF), bf(w_down[layer]), g_final,
                 seq=seq, final_norm=(layer == depth - 1))
    return h.reshape(batch, seq, D_MODEL)
```

```python
import functools

import numpy as np

import jax
import jax.numpy as jnp
from jax import lax
from jax.experimental import pallas as pl
from jax.experimental.pallas import tpu as pltpu

D_MODEL = 1024
HEAD_DIM = 64
N_MEM_HEADS = 4
N_MIX_HEADS = 12
D_MIX = N_MIX_HEADS * HEAD_DIM
D_MEMQ = N_MEM_HEADS * HEAD_DIM
D_FF = 2816
CONV_WIDTH = 3
ROPE_THETA = 10000.0
DIL_BLOCK = 128
NORM_EPS = 1e-6
NEG = -1e30
SCALE = HEAD_DIM ** -0.5
LOG2E = 1.4426950408889634

LANES = 128
N_PAIRS = D_MIX // LANES
VMEM_LIMIT = 48 * 1024 * 1024

ROW_TILE = 512
FFN_TILE = 1024
FOX_Q_TILE = 2048
FOX_K_TILE = 512
FF_CHUNK = 256
DECAY_PIECES = 3
DECAY_LANES_PER_HEAD = 2 * DECAY_PIECES
PIECE_LANES = 16
DIL_CHUNK = 2048
DIL_RES = 16
DIL_RUN = ROW_TILE // DIL_RES
TILES_PER_CHUNK = DIL_CHUNK // ROW_TILE

_NT = (((1,), (1,)), ((), ()))


def _cparams(n_axes):
    return pltpu.CompilerParams(
        dimension_semantics=("arbitrary",) * n_axes,
        vmem_limit_bytes=VMEM_LIMIT)


def _rms(x, g):
    y = x * lax.rsqrt(jnp.mean(x * x, axis=-1, keepdims=True) + NORM_EPS)
    return y * g


def _lane_first_half():
    lane = lax.broadcasted_iota(jnp.int32, (1, LANES), 1)
    return lane < HEAD_DIM


def _split3(x):
    hi = x.astype(jnp.bfloat16)
    rem = x - hi.astype(jnp.float32)
    mid = rem.astype(jnp.bfloat16)
    lo = (rem - mid.astype(jnp.float32)).astype(jnp.bfloat16)
    return hi, mid, lo


def _in_proj_kernel(*refs, rope, forget, tiles_per_seq):
    x_ref, g_ref, wq_ref, wk_ref, wv_ref, wm_ref = refs[:6]
    pos = 6
    if forget:
        wf_ref, bf_ref, tri_ref, sel_ref, ones_ref = refs[pos:pos + 5]; pos += 5
    if rope:
        cos_ref, sin_ref = refs[pos:pos + 2]; pos += 2
    q_ref, k_ref, v_ref, qm_ref = refs[pos:pos + 4]; pos += 4
    if forget:
        qx_ref, kx_ref, carry_ref = refs[pos:pos + 3]

    if rope:
        x = jnp.concatenate([x_ref[:, r, :] for r in range(DIL_RES)], axis=0)
    else:
        x = x_ref[...]
    xn = _rms(x, g_ref[...]).astype(jnp.bfloat16)
    q_scale = SCALE * LOG2E

    def proj(w_ref):
        return jnp.dot(xn, w_ref[...], preferred_element_type=jnp.float32)

    def rotate(t):
        lane = lax.broadcasted_iota(jnp.int32, (1, LANES), 1)
        first = (lane % HEAD_DIM) < (HEAD_DIM // 2)
        partner = jnp.where(first, pltpu.roll(t, LANES - HEAD_DIM // 2, 1),
                            pltpu.roll(t, HEAD_DIM // 2, 1))
        return t * cos_ref[...] + partner * sin_ref[...]

    if rope:
        q = proj(wq_ref)
        for p in range(N_PAIRS):
            cols = slice(p * LANES, (p + 1) * LANES)
            q_ref[:, cols] = (rotate(q[:, cols]) * q_scale).astype(jnp.bfloat16)
        k = proj(wk_ref)
        for p in range(N_PAIRS):
            cols = slice(p * LANES, (p + 1) * LANES)
            k_ref[:, cols] = rotate(k[:, cols]).astype(jnp.bfloat16)
        v_ref[...] = proj(wv_ref).astype(jnp.bfloat16)
        qm_ref[...] = (proj(wm_ref) * SCALE).astype(jnp.bfloat16)
        return

    @pl.when(pl.program_id(0) % tiles_per_seq == 0)
    def _():
        carry_ref[...] = jnp.zeros(carry_ref.shape, jnp.float32)

    f = proj(wf_ref) + bf_ref[...]
    q_ref[...] = (proj(wq_ref) * q_scale).astype(jnp.bfloat16)
    log_f = jnp.minimum(f, 0.0) - jnp.log1p(jnp.exp(-jnp.abs(f)))
    lane = lax.broadcasted_iota(jnp.int32, (1, LANES), 1)
    hi, mid, lo = _split3(log_f)
    packed = jnp.where(lane < PIECE_LANES, hi, jnp.where(lane < 2 * PIECE_LANES, mid, lo))
    c3 = carry_ref[0:1, :] + jnp.dot(tri_ref[...], packed, preferred_element_type=jnp.float32)
    carry_ref[0:1, :] = c3[c3.shape[0] - 1:, :]
    c = (c3 + pltpu.roll(c3, LANES - PIECE_LANES, 1)) + pltpu.roll(c3, LANES - 2 * PIECE_LANES, 1)
    k_ref[...] = proj(wk_ref).astype(jnp.bfloat16)
    hi, mid, lo = _split3(c * LOG2E)
    packed = jnp.where(lane < PIECE_LANES, hi,
                       jnp.where(lane < 2 * PIECE_LANES, pltpu.roll(mid, PIECE_LANES, 1),
                                 pltpu.roll(lo, 2 * PIECE_LANES, 1)))
    x_decay = ones_ref[...] + jnp.dot(packed, sel_ref[...], preferred_element_type=jnp.float32)
    qx_ref[...] = x_decay[:, :LANES].astype(jnp.bfloat16)
    kx_ref[...] = x_decay[:, LANES:].astype(jnp.bfloat16)
    v_ref[...] = proj(wv_ref).astype(jnp.bfloat16)
    qm_ref[...] = (proj(wm_ref) * SCALE).astype(jnp.bfloat16)


def _decay_selectors():
    sel = np.zeros((LANES, 2 * LANES), np.float32)
    ones = np.zeros((1, 2 * LANES), np.float32)
    for h in range(N_MIX_HEADS):
        base = h * DECAY_LANES_PER_HEAD
        for piece in range(DECAY_PIECES):
            sel[piece * PIECE_LANES + h, base + piece] = 1.0
            sel[piece * PIECE_LANES + h, LANES + base + DECAY_PIECES + piece] = -1.0
            ones[0, base + DECAY_PIECES + piece] = 1.0
            ones[0, LANES + base + piece] = 1.0
    return jnp.asarray(sel, jnp.bfloat16), jnp.asarray(ones)


def _forget_lanes(a):
    a = jnp.pad(a, ((0, 0), (0, PIECE_LANES - N_MIX_HEADS)))
    a = jnp.tile(a, (1, DECAY_PIECES))
    return jnp.pad(a, ((0, 0), (0, LANES - DECAY_PIECES * PIECE_LANES)))


def _in_proj(h, g, wq, wk, wv, wm, forget_params=None, rope_tabs=None, *, seq):
    t = h.shape[0]
    tm = ROW_TILE
    forget = forget_params is not None
    rope = rope_tabs is not None
    row = lambda i: (i, 0)
    fixed = lambda i: (0, 0)
    x_spec = pl.BlockSpec((tm, D_MODEL), row)
    if rope:
        h = h.reshape(t // DIL_RES, DIL_RES, D_MODEL)
        x_spec = pl.BlockSpec((DIL_RUN, DIL_RES, D_MODEL), lambda i: (i, 0, 0))
    in_specs = [x_spec,
                pl.BlockSpec((1, D_MODEL), fixed),
                pl.BlockSpec((D_MODEL, D_MIX), fixed),
                pl.BlockSpec((D_MODEL, D_MIX), fixed),
                pl.BlockSpec((D_MODEL, D_MIX), fixed),
                pl.BlockSpec((D_MODEL, D_MEMQ), fixed)]
    args = [h, g, wq, wk, wv, wm]
    tiles_per_seq = seq // tm
    out_shape = [jax.ShapeDtypeStruct((t, D_MIX), jnp.bfloat16)] * 3
    out_shape += [jax.ShapeDtypeStruct((t, D_MEMQ), jnp.bfloat16)]
    out_specs = [pl.BlockSpec((tm, D_MIX), row)] * 3
    out_specs += [pl.BlockSpec((tm, D_MEMQ), row)]
    scratch = []
    if forget:
        w_f, b_f = forget_params
        sel, ones = _decay_selectors()
        tri = jnp.asarray(np.tril(np.ones((tm, tm), np.float32)), jnp.bfloat16)
        in_specs += [pl.BlockSpec((D_MODEL, LANES), fixed),
                     pl.BlockSpec((1, LANES), fixed),
                     pl.BlockSpec((tm, tm), fixed),
                     pl.BlockSpec((LANES, 2 * LANES), fixed),
                     pl.BlockSpec((1, 2 * LANES), fixed)]
        args += [w_f, b_f, tri, sel, ones]
        out_shape += [jax.ShapeDtypeStruct((t, LANES), jnp.bfloat16)] * 2
        out_specs += [pl.BlockSpec((tm, LANES), row)] * 2
        scratch.append(pltpu.VMEM((8, LANES), jnp.float32))
    if rope:
        tab = lambda i: (i % tiles_per_seq, 0)
        in_specs += [pl.BlockSpec((tm, LANES), tab), pl.BlockSpec((tm, LANES), tab)]
        args += list(rope_tabs)
    return pl.pallas_call(
        functools.partial(_in_proj_kernel, rope=rope, forget=forget,
                          tiles_per_seq=tiles_per_seq),
        grid=(t // tm,),
        in_specs=in_specs, out_specs=out_specs, out_shape=out_shape,
        scratch_shapes=scratch,
        compiler_params=_cparams(1),
        name="in_proj_rope" if rope else "in_proj_fox",
    )(*args)


def _mem_kv_kernel(x_ref, g_ref, wk_ref, wv_ref, km_ref, vm_ref):
    xn = _rms(x_ref[...], g_ref[...]).astype(jnp.bfloat16)
    km_ref[...] = jnp.dot(xn, wk_ref[...],
                          preferred_element_type=jnp.float32).astype(jnp.bfloat16)
    vm_ref[...] = jnp.dot(xn, wv_ref[...],
                          preferred_element_type=jnp.float32).astype(jnp.bfloat16)


def _mem_kv(mem2d, g, wk, wv, *, mem_tokens):
    rows = mem2d.shape[0]
    row = lambda i: (i, 0)
    fixed = lambda i: (0, 0)
    return pl.pallas_call(
        _mem_kv_kernel,
        grid=(rows // mem_tokens,),
        in_specs=[pl.BlockSpec((mem_tokens, D_MODEL), row),
                  pl.BlockSpec((1, D_MODEL), fixed),
                  pl.BlockSpec((D_MODEL, D_MEMQ), fixed),
                  pl.BlockSpec((D_MODEL, D_MEMQ), fixed)],
        out_specs=[pl.BlockSpec((mem_tokens, D_MEMQ), row)] * 2,
        out_shape=[jax.ShapeDtypeStruct((rows, D_MEMQ), jnp.bfloat16)] * 2,
        compiler_params=_cparams(1),
        name="mem_kv",
    )(mem2d, g, wk, wv)


def _fox_kernel(q_ref, qx_ref, k_ref, kx_ref, v_ref, o_ref, m_scr, l_scr, acc_scr):
    tq, tk = FOX_Q_TILE, FOX_K_TILE
    n_blocks = tq // tk
    pair = pl.program_id(1)
    qi = pl.program_id(2)
    lane = lax.broadcasted_iota(jnp.int32, (1, LANES), 1)
    first = lane < HEAD_DIM
    head_lanes = (first, jnp.logical_not(first))
    q = q_ref[...]
    qx = qx_ref[...]
    zero = jnp.zeros_like(q)
    q_heads = []
    for hh in range(2):
        lo = (2 * pair + hh) * DECAY_LANES_PER_HEAD
        own_decay = (lane >= lo) & (lane < lo + DECAY_LANES_PER_HEAD)
        q_heads.append(jnp.concatenate([jnp.where(head_lanes[hh], q, zero),
                                        jnp.where(own_decay, qx, zero)], axis=1))

    m_scr[...] = jnp.full(m_scr.shape, NEG, jnp.float32)
    l_scr[...] = jnp.zeros(l_scr.shape, jnp.float32)
    acc_scr[...] = jnp.zeros(acc_scr.shape, jnp.float32)
    ones = jnp.ones((tk, LANES), jnp.bfloat16)

    def load_kv(j):
        start = pl.multiple_of(j * tk, tk)
        k = jnp.concatenate([k_ref[pl.ds(start, tk), :], kx_ref[pl.ds(start, tk), :]],
                            axis=1)
        v = jnp.concatenate([v_ref[pl.ds(start, tk), :], ones], axis=1)
        return k, v

    def chain(hh, r, k, v, diagonal):
        rows = slice(r * tk, (r + 1) * tk)
        s = lax.dot_general(q_heads[hh][rows], k, _NT, preferred_element_type=jnp.float32)
        if diagonal:
            row = lax.broadcasted_iota(jnp.int32, (tk, tk), 0)
            col = lax.broadcasted_iota(jnp.int32, (tk, tk), 1)
            s = jnp.where(col <= row, s, NEG)
        m_prev = m_scr[hh, rows, :]
        m_new = jnp.maximum(m_prev, jnp.max(s, axis=-1, keepdims=True))
        alpha = jnp.exp2(m_prev - m_new)
        p = jnp.exp2(s - jnp.concatenate([m_new] * (tk // LANES), axis=1))
        pv = jnp.dot(p.astype(jnp.bfloat16), v, preferred_element_type=jnp.float32)
        l_scr[hh, rows, :] = alpha * l_scr[hh, rows, :] + pv[:, LANES:]
        acc_scr[hh, rows, :] = alpha * acc_scr[hh, rows, :] + pv[:, :LANES]
        m_scr[hh, rows, :] = m_new

    def body(j, carry):
        k, v = load_kv(j)
        for r in range(n_blocks):
            for hh in range(2):
                chain(hh, r, k, v, False)
        return carry

    n_before = qi * n_blocks
    lax.fori_loop(0, n_before, body, 0)
    for d in range(n_blocks):
        k, v = load_kv(n_before + d)
        for r in range(d, n_blocks):
            for hh in range(2):
                chain(hh, r, k, v, r == d)
    o0 = acc_scr[0] * (1.0 / l_scr[0])
    o1 = acc_scr[1] * (1.0 / l_scr[1])
    o_ref[...] = jnp.where(first, o0, o1).astype(o_ref.dtype)


def _fox_attention(q, qx, k, kx, v, *, batch, seq):
    tq = FOX_Q_TILE
    as_seq = lambda a: a.reshape(batch, seq, a.shape[-1])
    q_spec = pl.BlockSpec((None, tq, LANES), lambda b, p, i: (b, i, p))
    qx_spec = pl.BlockSpec((None, tq, LANES), lambda b, p, i: (b, i, 0))
    kv_spec = pl.BlockSpec((None, seq, LANES), lambda b, p, i: (b, 0, p))
    kx_spec = pl.BlockSpec((None, seq, LANES), lambda b, p, i: (b, 0, 0))
    out = pl.pallas_call(
        _fox_kernel,
        grid=(batch, N_PAIRS, seq // tq),
        in_specs=[q_spec, qx_spec, kv_spec, kx_spec, kv_spec],
        out_specs=q_spec,
        out_shape=jax.ShapeDtypeStruct((batch, seq, D_MIX), jnp.bfloat16),
        scratch_shapes=[pltpu.VMEM((2, tq, LANES), jnp.float32),
                        pltpu.VMEM((2, tq, LANES), jnp.float32),
                        pltpu.VMEM((2, tq, LANES), jnp.float32)],
        compiler_params=_cparams(3),
        name="fox_attention",
    )(as_seq(q), as_seq(qx), as_seq(k), as_seq(kx), as_seq(v))
    return out.reshape(batch * seq, D_MIX)


def _band_bias(n_q, entry_q, entry_k):
    a = np.arange(2 * n_q)[:, None] % n_q
    c = np.arange(2 * n_q)[None, :]
    dist = entry_q(a) - entry_k(c)
    ok = (dist >= 0) & (dist <= DIL_BLOCK)
    planes = [ok, ok & (c >= n_q)]
    return jnp.asarray(np.where(np.stack(planes), 0.0, NEG), jnp.float32)


def _dilated_masks():
    blk = DIL_BLOCK
    b16 = _band_bias(blk, lambda a: a + blk, lambda c: c)
    run4 = 32
    b4 = _band_bias(blk, lambda a: blk + 4 * (a % run4) + a // run4,
                    lambda c: (c // blk) * blk + 4 * ((c % blk) % run4) + (c % blk) // run4)
    n1 = 2 * blk
    b1 = _band_bias(n1, lambda a: n1 + DIL_RES * (a % 16) + a // 16,
                    lambda c: (c // n1) * n1 + DIL_RES * ((c % n1) % 16) + (c % n1) // 16)
    return b16, b4, b1


def _dilated_kernel(q_ref, kp_ref, kc_ref, vp_ref, vc_ref, b16_ref, b4_ref, b1_ref,
                    o_ref, m_scr, l_scr, acc_scr):
    first = _lane_first_half()
    seq_start = jnp.where(pl.program_id(1) > 0, 0, 1)

    def gather(ref, slices):
        return jnp.concatenate([ref[s, :] for s in slices], axis=0)

    def chain(q_slices, key_parts, bias, first_branch):
        q = gather(q_ref, q_slices)
        zero = jnp.zeros_like(q)
        q2 = jnp.concatenate([jnp.where(first, q, zero), jnp.where(first, zero, q)], axis=0)
        k2 = jnp.concatenate([gather(kr, sl) for kr, _, sl in key_parts], axis=0)
        v2 = jnp.concatenate([gather(vr, sl) for _, vr, sl in key_parts], axis=0)
        n_q, n_k = q.shape[0], k2.shape[0]
        v2 = jnp.concatenate([v2, jnp.ones((n_k, LANES), jnp.bfloat16)], axis=1)
        s = lax.dot_general(q2, k2, _NT, preferred_element_type=jnp.float32) + bias
        m_cur = jnp.max(s, axis=-1, keepdims=True)
        if first_branch:
            m_new = jnp.broadcast_to(m_cur, (2 * n_q, LANES))
        else:
            m_prev = jnp.concatenate([gather(m_scr.at[hh], q_slices) for hh in range(2)],
                                     axis=0)
            m_new = jnp.maximum(m_prev, m_cur)
            alpha = jnp.exp2(m_prev - m_new)
        p = jnp.exp2(s - jnp.concatenate([m_new] * (n_k // LANES), axis=1))
        pv = jnp.dot(p.astype(jnp.bfloat16), v2, preferred_element_type=jnp.float32)
        num, den = pv[:, :LANES], pv[:, LANES:]
        if not first_branch:
            l_prev = jnp.concatenate([gather(l_scr.at[hh], q_slices) for hh in range(2)],
                                     axis=0)
            a_prev = jnp.concatenate([gather(acc_scr.at[hh], q_slices) for hh in range(2)],
                                     axis=0)
            den = alpha * l_prev + den
            num = alpha * a_prev + num
        for hh in range(2):
            off = hh * n_q
            for sl in q_slices:
                n = sl.stop - sl.start
                m_scr[hh, sl, :] = m_new[off:off + n]
                l_scr[hh, sl, :] = den[off:off + n]
                acc_scr[hh, sl, :] = num[off:off + n]
                off += n

    def run(tile, r, lo, n):
        start = tile * ROW_TILE + r * DIL_RUN + lo
        return slice(start, start + n)

    def block(cur, prv, prev_chunk, bias_ref, first_branch):
        if prev_chunk:
            parts = [(kp_ref, vp_ref, prv), (kc_ref, vc_ref, cur)]
            bias = bias_ref[seq_start]
        else:
            parts = [(kc_ref, vc_ref, prv), (kc_ref, vc_ref, cur)]
            bias = bias_ref[0]
        chain(cur, parts, bias, first_branch)

    last = TILES_PER_CHUNK - 1
    for r in range(DIL_RES):
        rows = [run(t, r, 0, DIL_RUN) for t in range(TILES_PER_CHUNK)]
        block(rows, rows, True, b16_ref, True)
    for r4 in range(4):
        for t in range(TILES_PER_CHUNK):
            cur = [run(t, r4 + 4 * u, 0, DIL_RUN) for u in range(4)]
            prv = [run(t - 1 if t else last, r4 + 4 * u, 0, DIL_RUN) for u in range(4)]
            block(cur, prv, t == 0, b4_ref, False)
    half = DIL_RUN // 2
    for t in range(TILES_PER_CHUNK):
        for hf in range(2):
            cur = [run(t, r, hf * half, half) for r in range(DIL_RES)]
            if hf:
                prv = [run(t, r, 0, half) for r in range(DIL_RES)]
            else:
                prv = [run(t - 1 if t else last, r, half, half) for r in range(DIL_RES)]
            block(cur, prv, t == 0 and hf == 0, b1_ref, False)

    o0 = acc_scr[0] * (1.0 / l_scr[0])
    o1 = acc_scr[1] * (1.0 / l_scr[1])
    o_ref[...] = jnp.where(first, o0, o1).astype(o_ref.dtype)


def _dilated_attention(q, k, v, *, batch, seq):
    n_chunks = seq // DIL_CHUNK
    flat = lambda a: a.reshape(batch * n_chunks, DIL_CHUNK, D_MIX)
    masks = _dilated_masks()
    cur = lambda b, c, p: (b * n_chunks + c, 0, p)
    prev = lambda b, c, p: (b * n_chunks + jnp.maximum(c - 1, 0), 0, p)
    spec = lambda f: pl.BlockSpec((None, DIL_CHUNK, LANES), f)
    mask_specs = [pl.BlockSpec(m.shape, lambda b, c, p: (0, 0, 0)) for m in masks]
    out = pl.pallas_call(
        _dilated_kernel,
        grid=(batch, n_chunks, N_PAIRS),
        in_specs=[spec(cur), spec(prev), spec(cur), spec(prev), spec(cur)] + mask_specs,
        out_specs=spec(cur),
        out_shape=jax.ShapeDtypeStruct((batch * n_chunks, DIL_CHUNK, D_MIX), jnp.bfloat16),
        scratch_shapes=[pltpu.VMEM((2, DIL_CHUNK, LANES), jnp.float32)] * 3,
        compiler_params=_cparams(3),
        name="dilated_attention",
    )(flat(q), flat(k), flat(k), flat(v), flat(v), *masks)
    return out.reshape(batch * seq, D_MIX)


def _out_proj_kernel(h_ref, mix_ref, qm_ref, km_ref, vm_ref, wmix_ref, wmem_ref, out_ref, *,
                     residue_major):
    first = _lane_first_half()
    mem_pairs = []
    for p in range(D_MEMQ // LANES):
        cols = slice(p * LANES, (p + 1) * LANES)
        qm = qm_ref[:, cols]
        zero = jnp.zeros_like(qm)
        km = km_ref[:, cols]
        vm = vm_ref[:, cols]
        outs = []
        for hh in range(2):
            qh = jnp.where(first, qm, zero) if hh == 0 else jnp.where(first, zero, qm)
            s = lax.dot_general(qh, km, _NT, preferred_element_type=jnp.float32)
            mx = jnp.max(s, axis=-1, keepdims=True)
            e = jnp.exp(s - mx)
            pn = (e * (1.0 / jnp.sum(e, axis=-1, keepdims=True))).astype(jnp.bfloat16)
            outs.append(jnp.dot(pn, vm, preferred_element_type=jnp.float32))
        mem_pairs.append(jnp.where(first, outs[0], outs[1]).astype(jnp.bfloat16))
    mem_out = jnp.concatenate(mem_pairs, axis=-1)

    y = jnp.dot(mix_ref[...], wmix_ref[...], preferred_element_type=jnp.float32)
    y = y + jnp.dot(mem_out, wmem_ref[...], preferred_element_type=jnp.float32)
    if residue_major:
        for r in range(DIL_RES):
            out_ref[:, r, :] = h_ref[:, r, :] + y[r * DIL_RUN:(r + 1) * DIL_RUN, :]
    else:
        out_ref[...] = h_ref[...] + y


def _out_proj(h, mix, qm, km, vm, w_mix, w_mem, *, seq, mem_tokens, residue_major):
    t = h.shape[0]
    tm = ROW_TILE
    tiles_per_seq = seq // tm
    row = lambda i: (i, 0)
    fixed = lambda i: (0, 0)
    per_batch = lambda i: (i // tiles_per_seq, 0)
    h_spec = pl.BlockSpec((tm, D_MODEL), row)
    h_shape = (t, D_MODEL)
    if residue_major:
        h_shape = (t // DIL_RES, DIL_RES, D_MODEL)
        h_spec = pl.BlockSpec((DIL_RUN, DIL_RES, D_MODEL), lambda i: (i, 0, 0))
    out = pl.pallas_call(
        functools.partial(_out_proj_kernel, residue_major=residue_major),
        grid=(t // tm,),
        in_specs=[h_spec,
                  pl.BlockSpec((tm, D_MIX), row),
                  pl.BlockSpec((tm, D_MEMQ), row),
                  pl.BlockSpec((mem_tokens, D_MEMQ), per_batch),
                  pl.BlockSpec((mem_tokens, D_MEMQ), per_batch),
                  pl.BlockSpec((D_MIX, D_MODEL), fixed),
                  pl.BlockSpec((D_MEMQ, D_MODEL), fixed)],
        out_specs=h_spec,
        out_shape=jax.ShapeDtypeStruct(h_shape, jnp.float32),
        compiler_params=_cparams(1),
        name="out_proj_residue_major" if residue_major else "out_proj",
    )(h.reshape(h_shape), mix, qm, km, vm, w_mix, w_mem)
    return out.reshape(t, D_MODEL)


def _ffn_kernel(h_ref, g_ref, wup_ref, cw_ref, cb_ref, wdown_ref, gfin_ref, out_ref,
                carry_ref, act_ref, *, tiles_per_seq, final_norm):
    tm = FFN_TILE
    halo = 8
    n_chunks = D_FF // FF_CHUNK
    i = pl.program_id(0)

    @pl.when(i % tiles_per_seq == 0)
    def _():
        carry_ref[...] = jnp.zeros(carry_ref.shape, jnp.float32)

    h = h_ref[...]
    xn = _rms(h, g_ref[...]).astype(jnp.bfloat16)
    row = lax.broadcasted_iota(jnp.int32, (tm, FF_CHUNK), 0)

    def conv(u, cols):
        prev = carry_ref[:, cols]
        back1 = jnp.where(row == 0, prev[halo - 1:halo, :], pltpu.roll(u, 1, 0))
        back2 = jnp.where(row == 0, prev[halo - 2:halo - 1, :],
                          jnp.where(row == 1, prev[halo - 1:halo, :], pltpu.roll(u, 2, 0)))
        carry_ref[:, cols] = u[tm - halo:tm, :]
        w = cw_ref[:, cols]
        c = cb_ref[:, cols] + w[0:1, :] * back2
        c = c + w[1:2, :] * back1
        return c + w[2:3, :] * u

    def val_cols(j):
        return slice(j * FF_CHUNK, (j + 1) * FF_CHUNK)

    def gate_cols(j):
        return slice(D_FF + j * FF_CHUNK, D_FF + (j + 1) * FF_CHUNK)

    def up(j):
        return (jnp.dot(xn, wup_ref[:, val_cols(j)], preferred_element_type=jnp.float32),
                jnp.dot(xn, wup_ref[:, gate_cols(j)], preferred_element_type=jnp.float32))

    u = up(0)
    for j in range(n_chunks):
        u_next = up(j + 1) if j + 1 < n_chunks else None
        val = conv(u[0], val_cols(j))
        gate = conv(u[1], gate_cols(j))
        act_ref[:, val_cols(j)] = (gate * (1.0 / (1.0 + jnp.exp(-gate))) * val
                                   ).astype(jnp.bfloat16)
        u = u_next
    y = h + jnp.dot(act_ref[...], wdown_ref[...], preferred_element_type=jnp.float32)
    if final_norm:
        y = _rms(y, gfin_ref[...])
    out_ref[...] = y


def _ffn(h, g, w_up, conv_w, conv_b, w_down, g_final, *, seq, final_norm):
    t = h.shape[0]
    tm = FFN_TILE
    row = lambda i: (i, 0)
    fixed = lambda i: (0, 0)
    once = dict(pipeline_mode=pl.Buffered(1))
    return pl.pallas_call(
        functools.partial(_ffn_kernel, tiles_per_seq=seq // tm, final_norm=final_norm),
        grid=(t // tm,),
        in_specs=[pl.BlockSpec((tm, D_MODEL), row),
                  pl.BlockSpec((1, D_MODEL), fixed),
                  pl.BlockSpec((D_MODEL, 2 * D_FF), fixed, **once),
                  pl.BlockSpec((CONV_WIDTH, 2 * D_FF), fixed),
                  pl.BlockSpec((1, 2 * D_FF), fixed),
                  pl.BlockSpec((D_FF, D_MODEL), fixed, **once),
                  pl.BlockSpec((1, D_MODEL), fixed)],
        out_specs=pl.BlockSpec((tm, D_MODEL), row),
        out_shape=jax.ShapeDtypeStruct((t, D_MODEL), jnp.float32),
        scratch_shapes=[pltpu.VMEM((8, 2 * D_FF), jnp.float32),
                        pltpu.VMEM((tm, D_FF), jnp.bfloat16)],
        compiler_params=_cparams(1),
        name="conv_ffn_final" if final_norm else "conv_ffn",
    )(h, g, w_up, conv_w, conv_b, w_down, g_final)


def _rope_tables(seq):
    inv = 1.0 / (ROPE_THETA ** (jnp.arange(0, HEAD_DIM, 2, dtype=jnp.float32) / HEAD_DIM))
    ang = jnp.arange(seq, dtype=jnp.float32)[:, None] * inv[None, :]
    cos, sin = jnp.cos(ang), jnp.sin(ang)
    reps = LANES // HEAD_DIM
    cos_t = jnp.tile(jnp.concatenate([cos, cos], axis=-1), (1, reps))
    sin_t = jnp.tile(jnp.concatenate([-sin, sin], axis=-1), (1, reps))

    def residue_major(tab):
        tab = tab.reshape(seq // ROW_TILE, DIL_RUN, DIL_RES, LANES)
        return tab.transpose(0, 2, 1, 3).reshape(seq, LANES)

    return residue_major(cos_t), residue_major(sin_t)


def kernel(x, mem, norm_mix, norm_mem, norm_ffn, w_in_fox, b_forget, w_in_dil,
           w_mem_kv, w_out, w_up, conv_w, conv_b, w_down, norm_final):
    batch, seq, _ = x.shape
    assert seq % DIL_CHUNK == 0 and seq % FOX_Q_TILE == 0
    mem_tokens = mem.shape[1]
    depth = norm_mix.shape[0]
    bf = lambda a: a.astype(jnp.bfloat16)
    h = x.reshape(batch * seq, D_MODEL)
    mem2d = mem.reshape(batch * mem_tokens, D_MODEL)
    rope_tabs = _rope_tables(seq)
    g_final = norm_final.reshape(1, D_MODEL)

    for layer in range(depth):
        kind, slot = layer % 2, layer // 2
        g_mix = norm_mix[layer].reshape(1, D_MODEL)
        km, vm = _mem_kv(mem2d, norm_mem[layer].reshape(1, D_MODEL),
                         bf(w_mem_kv[layer][:, :D_MEMQ]), bf(w_mem_kv[layer][:, D_MEMQ:]),
                         mem_tokens=mem_tokens)
        if kind == 0:
            w = w_in_fox[slot]
            w_f = _forget_lanes(w[:, 3 * D_MIX:3 * D_MIX + N_MIX_HEADS])
            b_f = _forget_lanes(b_forget[slot].reshape(1, N_MIX_HEADS))
            q, k, v, qm, qx, kx = _in_proj(
                h, g_mix, bf(w[:, :D_MIX]), bf(w[:, D_MIX:2 * D_MIX]),
                bf(w[:, 2 * D_MIX:3 * D_MIX]), bf(w[:, 3 * D_MIX + N_MIX_HEADS:]),
                forget_params=(bf(w_f), b_f), seq=seq)
            mix = _fox_attention(q, qx, k, kx, v, batch=batch, seq=seq)
        else:
            w = w_in_dil[slot]
            q, k, v, qm = _in_proj(
                h, g_mix, bf(w[:, :D_MIX]), bf(w[:, D_MIX:2 * D_MIX]),
                bf(w[:, 2 * D_MIX:3 * D_MIX]), bf(w[:, 3 * D_MIX:]),
                rope_tabs=rope_tabs, seq=seq)
            mix = _dilated_attention(q, k, v, batch=batch, seq=seq)
        h = _out_proj(h, mix, qm, km, vm,
                      bf(w_out[layer][:D_MIX]), bf(w_out[layer][D_MIX:]),
                      seq=seq, mem_tokens=mem_tokens, residue_major=(kind == 1))
        h = _ffn(h, norm_ffn[layer].reshape(1, D_MODEL), bf(w_up[layer]), conv_w[layer],
                 conv_b[layer].reshape(1, 2 * D_FF), bf(w_down[layer]), g_final,
                 seq=seq, final_norm=(layer == depth - 1))
    return h.reshape(batch, seq, D_MODEL)
```

```python
import functools

import numpy as np

import jax
import jax.numpy as jnp
from jax import lax
from jax.experimental import pallas as pl
from jax.experimental.pallas import tpu as pltpu

D_MODEL = 1024
HEAD_DIM = 64
N_MEM_HEADS = 4
N_MIX_HEADS = 12
D_MIX = N_MIX_HEADS * HEAD_DIM
D_MEMQ = N_MEM_HEADS * HEAD_DIM
D_FF = 2816
CONV_WIDTH = 3
ROPE_THETA = 10000.0
DIL_BLOCK = 128
NORM_EPS = 1e-6
NEG = -1e30
SCALE = HEAD_DIM ** -0.5
LOG2E = 1.4426950408889634

LANES = 128
N_PAIRS = D_MIX // LANES
VMEM_LIMIT = 48 * 1024 * 1024

ROW_TILE = 512
FFN_TILE = 1024
FOX_Q_TILE = 2048
FOX_K_TILE = 512
FF_CHUNK = 256
DECAY_PIECES = 3
DECAY_LANES_PER_HEAD = 2 * DECAY_PIECES
PIECE_LANES = 16
DIL_CHUNK = 2048
DIL_RES = 16
DIL_RUN = ROW_TILE // DIL_RES
TILES_PER_CHUNK = DIL_CHUNK // ROW_TILE

_NT = (((1,), (1,)), ((), ()))


def _cparams(n_axes):
    return pltpu.CompilerParams(
        dimension_semantics=("arbitrary",) * n_axes,
        vmem_limit_bytes=VMEM_LIMIT)


def _rms(x, g):
    y = x * lax.rsqrt(jnp.mean(x * x, axis=-1, keepdims=True) + NORM_EPS)
    return y * g


def _lane_first_half():
    lane = lax.broadcasted_iota(jnp.int32, (1, LANES), 1)
    return lane < HEAD_DIM


def _split3(x):
    hi = x.astype(jnp.bfloat16)
    rem = x - hi.astype(jnp.float32)
    mid = rem.astype(jnp.bfloat16)
    lo = (rem - mid.astype(jnp.float32)).astype(jnp.bfloat16)
    return hi, mid, lo


def _in_proj_kernel(*refs, rope, forget, tiles_per_seq):
    x_ref, g_ref, w_ref = refs[:3]
    pos = 3
    if forget:
        wm_ref, wf_ref, bf_ref, tri_ref, sel_ref, ones_ref = refs[pos:pos + 6]; pos += 6
    if rope:
        cos_ref, sin_ref = refs[pos:pos + 2]; pos += 2
    q_ref, k_ref, v_ref, qm_ref = refs[pos:pos + 4]; pos += 4
    if forget:
        qx_ref, kx_ref, carry_ref = refs[pos:pos + 3]

    if rope:
        x = jnp.concatenate([x_ref[:, r, :] for r in range(DIL_RES)], axis=0)
    else:
        x = x_ref[...]
    xn = _rms(x, g_ref[...]).astype(jnp.bfloat16)
    q_scale = SCALE * LOG2E

    def proj(index):
        width = D_MIX if index < 3 else D_MEMQ
        return jnp.dot(xn, w_ref[:, index * D_MIX:index * D_MIX + width],
                       preferred_element_type=jnp.float32)

    def rotate(t):
        lane = lax.broadcasted_iota(jnp.int32, (1, LANES), 1)
        first = (lane % HEAD_DIM) < (HEAD_DIM // 2)
        partner = jnp.where(first, pltpu.roll(t, LANES - HEAD_DIM // 2, 1),
                            pltpu.roll(t, HEAD_DIM // 2, 1))
        return t * cos_ref[...] + partner * sin_ref[...]

    if rope:
        q = proj(0)
        for p in range(N_PAIRS):
            cols = slice(p * LANES, (p + 1) * LANES)
            q_ref[:, cols] = (rotate(q[:, cols]) * q_scale).astype(jnp.bfloat16)
        k = proj(1)
        for p in range(N_PAIRS):
            cols = slice(p * LANES, (p + 1) * LANES)
            k_ref[:, cols] = rotate(k[:, cols]).astype(jnp.bfloat16)
        v_ref[...] = proj(2).astype(jnp.bfloat16)
        qm_ref[...] = (proj(3) * SCALE).astype(jnp.bfloat16)
        return

    @pl.when(pl.program_id(0) % tiles_per_seq == 0)
    def _():
        carry_ref[...] = jnp.zeros(carry_ref.shape, jnp.float32)

    f = jnp.dot(xn, wf_ref[...], preferred_element_type=jnp.float32) + bf_ref[...]
    q_ref[...] = (proj(0) * q_scale).astype(jnp.bfloat16)
    log_f = jnp.minimum(f, 0.0) - jnp.log1p(jnp.exp(-jnp.abs(f)))
    lane = lax.broadcasted_iota(jnp.int32, (1, LANES), 1)
    hi, mid, lo = _split3(log_f)
    packed = jnp.where(lane < PIECE_LANES, hi, jnp.where(lane < 2 * PIECE_LANES, mid, lo))
    c3 = carry_ref[0:1, :] + jnp.dot(tri_ref[...], packed, preferred_element_type=jnp.float32)
    carry_ref[0:1, :] = c3[c3.shape[0] - 1:, :]
    c = (c3 + pltpu.roll(c3, LANES - PIECE_LANES, 1)) + pltpu.roll(c3, LANES - 2 * PIECE_LANES, 1)
    k_ref[...] = proj(1).astype(jnp.bfloat16)
    hi, mid, lo = _split3(c * LOG2E)
    packed = jnp.where(lane < PIECE_LANES, hi,
                       jnp.where(lane < 2 * PIECE_LANES, pltpu.roll(mid, PIECE_LANES, 1),
                                 pltpu.roll(lo, 2 * PIECE_LANES, 1)))
    x_decay = ones_ref[...] + jnp.dot(packed, sel_ref[...], preferred_element_type=jnp.float32)
    qx_ref[...] = x_decay[:, :LANES].astype(jnp.bfloat16)
    kx_ref[...] = x_decay[:, LANES:].astype(jnp.bfloat16)
    v_ref[...] = proj(2).astype(jnp.bfloat16)
    qm_ref[...] = (jnp.dot(xn, wm_ref[...], preferred_element_type=jnp.float32) * SCALE
                   ).astype(jnp.bfloat16)


def _decay_selectors():
    sel = np.zeros((LANES, 2 * LANES), np.float32)
    ones = np.zeros((1, 2 * LANES), np.float32)
    for h in range(N_MIX_HEADS):
        base = h * DECAY_LANES_PER_HEAD
        for piece in range(DECAY_PIECES):
            sel[piece * PIECE_LANES + h, base + piece] = 1.0
            sel[piece * PIECE_LANES + h, LANES + base + DECAY_PIECES + piece] = -1.0
            ones[0, base + DECAY_PIECES + piece] = 1.0
            ones[0, LANES + base + piece] = 1.0
    return jnp.asarray(sel, jnp.bfloat16), jnp.asarray(ones)


def _forget_lanes(a):
    a = jnp.pad(a, ((0, 0), (0, PIECE_LANES - N_MIX_HEADS)))
    a = jnp.tile(a, (1, DECAY_PIECES))
    return jnp.pad(a, ((0, 0), (0, LANES - DECAY_PIECES * PIECE_LANES)))


def _layer_spec(stacked, index, **kwargs):
    return pl.BlockSpec((None,) + stacked.shape[1:], lambda *_: (index, 0, 0), **kwargs)


def _in_proj(h, g, w, slot, layer, forget_params=None, rope_tabs=None, *, seq):
    t = h.shape[0]
    tm = ROW_TILE
    forget = forget_params is not None
    rope = rope_tabs is not None
    row = lambda i: (i, 0)
    fixed = lambda i: (0, 0)
    x_spec = pl.BlockSpec((tm, D_MODEL), row)
    if rope:
        h = h.reshape(t // DIL_RES, DIL_RES, D_MODEL)
        x_spec = pl.BlockSpec((DIL_RUN, DIL_RES, D_MODEL), lambda i: (i, 0, 0))
    in_specs = [x_spec, _layer_spec(g, layer), _layer_spec(w, slot)]
    args = [h, g, w]
    tiles_per_seq = seq // tm
    out_shape = [jax.ShapeDtypeStruct((t, D_MIX), jnp.bfloat16)] * 3
    out_shape += [jax.ShapeDtypeStruct((t, D_MEMQ), jnp.bfloat16)]
    out_specs = [pl.BlockSpec((tm, D_MIX), row)] * 3
    out_specs += [pl.BlockSpec((tm, D_MEMQ), row)]
    scratch = []
    if forget:
        w_m, w_f, b_f = forget_params
        sel, ones = _decay_selectors()
        tri = jnp.asarray(np.tril(np.ones((tm, tm), np.float32)), jnp.bfloat16)
        in_specs += [_layer_spec(w_m, slot), _layer_spec(w_f, slot), _layer_spec(b_f, slot),
                     pl.BlockSpec((tm, tm), fixed),
                     pl.BlockSpec((LANES, 2 * LANES), fixed),
                     pl.BlockSpec((1, 2 * LANES), fixed)]
        args += [w_m, w_f, b_f, tri, sel, ones]
        out_shape += [jax.ShapeDtypeStruct((t, LANES), jnp.bfloat16)] * 2
        out_specs += [pl.BlockSpec((tm, LANES), row)] * 2
        scratch.append(pltpu.VMEM((8, LANES), jnp.float32))
    if rope:
        tab = lambda i: (i % tiles_per_seq, 0)
        in_specs += [pl.BlockSpec((tm, LANES), tab), pl.BlockSpec((tm, LANES), tab)]
        args += list(rope_tabs)
    return pl.pallas_call(
        functools.partial(_in_proj_kernel, rope=rope, forget=forget,
                          tiles_per_seq=tiles_per_seq),
        grid=(t // tm,),
        in_specs=in_specs, out_specs=out_specs, out_shape=out_shape,
        scratch_shapes=scratch,
        compiler_params=_cparams(1),
        name="in_proj_rope" if rope else "in_proj_fox",
    )(*args)


def _mem_kv_kernel(x_ref, g_ref, w_ref, km_ref, vm_ref):
    xn = _rms(x_ref[...], g_ref[...]).astype(jnp.bfloat16)
    kv = jnp.dot(xn, w_ref[...], preferred_element_type=jnp.float32)
    km_ref[...] = kv[:, :D_MEMQ].astype(jnp.bfloat16)
    vm_ref[...] = kv[:, D_MEMQ:].astype(jnp.bfloat16)


def _mem_kv(mem2d, g, w, *, mem_tokens):
    rows = mem2d.shape[0]
    depth = w.shape[0]
    per_layer = lambda shape: pl.BlockSpec((None,) + shape, lambda l, b: (l, 0, 0))
    out_spec = pl.BlockSpec((None, mem_tokens, D_MEMQ), lambda l, b: (l, b, 0))
    return pl.pallas_call(
        _mem_kv_kernel,
        grid=(depth, rows // mem_tokens),
        in_specs=[pl.BlockSpec((mem_tokens, D_MODEL), lambda l, b: (b, 0)),
                  per_layer((1, D_MODEL)),
                  per_layer((D_MODEL, 2 * D_MEMQ))],
        out_specs=[out_spec, out_spec],
        out_shape=[jax.ShapeDtypeStruct((depth, rows, D_MEMQ), jnp.bfloat16)] * 2,
        compiler_params=_cparams(2),
        name="mem_kv",
    )(mem2d, g, w)


def _fox_kernel(q_ref, qx_ref, k_ref, kx_ref, v_ref, o_ref, m_scr, l_scr, acc_scr):
    tq, tk = FOX_Q_TILE, FOX_K_TILE
    n_blocks = tq // tk
    pair = pl.program_id(1)
    qi = pl.program_id(2)
    lane = lax.broadcasted_iota(jnp.int32, (1, LANES), 1)
    first = lane < HEAD_DIM
    head_lanes = (first, jnp.logical_not(first))
    q = q_ref[...]
    qx = qx_ref[...]
    zero = jnp.zeros_like(q)
    q_heads = []
    for hh in range(2):
        lo = (2 * pair + hh) * DECAY_LANES_PER_HEAD
        own_decay = (lane >= lo) & (lane < lo + DECAY_LANES_PER_HEAD)
        q_heads.append(jnp.concatenate([jnp.where(head_lanes[hh], q, zero),
                                        jnp.where(own_decay, qx, zero)], axis=1))

    m_scr[...] = jnp.full(m_scr.shape, NEG, jnp.float32)
    l_scr[...] = jnp.zeros(l_scr.shape, jnp.float32)
    acc_scr[...] = jnp.zeros(acc_scr.shape, jnp.float32)
    ones = jnp.ones((tk, LANES), jnp.bfloat16)

    def load_kv(j):
        start = pl.multiple_of(j * tk, tk)
        k = jnp.concatenate([k_ref[pl.ds(start, tk), :], kx_ref[pl.ds(start, tk), :]],
                            axis=1)
        v = jnp.concatenate([v_ref[pl.ds(start, tk), :], ones], axis=1)
        return k, v

    def chain(hh, r, k, v, diagonal):
        rows = slice(r * tk, (r + 1) * tk)
        s = lax.dot_general(q_heads[hh][rows], k, _NT, preferred_element_type=jnp.float32)
        if diagonal:
            row = lax.broadcasted_iota(jnp.int32, (tk, tk), 0)
            col = lax.broadcasted_iota(jnp.int32, (tk, tk), 1)
            s = jnp.where(col <= row, s, NEG)
        m_prev = m_scr[hh, rows, :]
        m_new = jnp.maximum(m_prev, jnp.max(s, axis=-1, keepdims=True))
        alpha = jnp.exp2(m_prev - m_new)
        p = jnp.exp2(s - jnp.concatenate([m_new] * (tk // LANES), axis=1))
        pv = jnp.dot(p.astype(jnp.bfloat16), v, preferred_element_type=jnp.float32)
        l_scr[hh, rows, :] = alpha * l_scr[hh, rows, :] + pv[:, LANES:]
        acc_scr[hh, rows, :] = alpha * acc_scr[hh, rows, :] + pv[:, :LANES]
        m_scr[hh, rows, :] = m_new

    def body(j, carry):
        for half in range(2):
            k, v = load_kv(2 * j + half)
            for r in range(n_blocks):
                for hh in range(2):
                    chain(hh, r, k, v, False)
        return carry

    assert n_blocks % 2 == 0
    n_before = qi * n_blocks
    lax.fori_loop(0, n_before // 2, body, 0)
    for d in range(n_blocks):
        k, v = load_kv(n_before + d)
        for r in range(d, n_blocks):
            for hh in range(2):
                chain(hh, r, k, v, r == d)
    o0 = acc_scr[0] * (1.0 / l_scr[0])
    o1 = acc_scr[1] * (1.0 / l_scr[1])
    o_ref[...] = jnp.where(first, o0, o1).astype(o_ref.dtype)


def _fox_attention(q, qx, k, kx, v, *, batch, seq):
    tq = FOX_Q_TILE
    as_seq = lambda a: a.reshape(batch, seq, a.shape[-1])
    q_spec = pl.BlockSpec((None, tq, LANES), lambda b, p, i: (b, i, p))
    qx_spec = pl.BlockSpec((None, tq, LANES), lambda b, p, i: (b, i, 0))
    kv_spec = pl.BlockSpec((None, seq, LANES), lambda b, p, i: (b, 0, p))
    kx_spec = pl.BlockSpec((None, seq, LANES), lambda b, p, i: (b, 0, 0))
    out = pl.pallas_call(
        _fox_kernel,
        grid=(batch, N_PAIRS, seq // tq),
        in_specs=[q_spec, qx_spec, kv_spec, kx_spec, kv_spec],
        out_specs=q_spec,
        out_shape=jax.ShapeDtypeStruct((batch, seq, D_MIX), jnp.bfloat16),
        scratch_shapes=[pltpu.VMEM((2, tq, LANES), jnp.float32),
                        pltpu.VMEM((2, tq, LANES), jnp.float32),
                        pltpu.VMEM((2, tq, LANES), jnp.float32)],
        compiler_params=_cparams(3),
        name="fox_attention",
    )(as_seq(q), as_seq(qx), as_seq(k), as_seq(kx), as_seq(v))
    return out.reshape(batch * seq, D_MIX)


def _band_bias(n_q, entry_q, entry_k):
    a = np.arange(2 * n_q)[:, None] % n_q
    c = np.arange(2 * n_q)[None, :]
    dist = entry_q(a) - entry_k(c)
    ok = (dist >= 0) & (dist <= DIL_BLOCK)
    planes = [ok, ok & (c >= n_q)]
    return jnp.asarray(np.where(np.stack(planes), 0.0, NEG), jnp.float32)


def _dilated_masks():
    blk = DIL_BLOCK
    b16 = _band_bias(blk, lambda a: a + blk, lambda c: c)
    run4 = 32
    b4 = _band_bias(blk, lambda a: blk + 4 * (a % run4) + a // run4,
                    lambda c: (c // blk) * blk + 4 * ((c % blk) % run4) + (c % blk) // run4)
    run1 = blk // DIL_RES
    b1 = _band_bias(blk, lambda a: blk + DIL_RES * (a % run1) + a // run1,
                    lambda c: (c // blk) * blk + DIL_RES * ((c % blk) % run1) + (c % blk) // run1)
    return b16, b4, b1


def _dilated_kernel(q_ref, kp_ref, kc_ref, vp_ref, vc_ref, b16_ref, b4_ref, b1_ref,
                    o_ref, m_scr, l_scr, acc_scr):
    first = _lane_first_half()
    seq_start = jnp.where(pl.program_id(1) > 0, 0, 1)

    def gather(ref, slices):
        return jnp.concatenate([ref[s, :] for s in slices], axis=0)

    def chain(q_slices, key_parts, bias, first_branch):
        q = gather(q_ref, q_slices)
        zero = jnp.zeros_like(q)
        q2 = jnp.concatenate([jnp.where(first, q, zero), jnp.where(first, zero, q)], axis=0)
        k2 = jnp.concatenate([gather(kr, sl) for kr, _, sl in key_parts], axis=0)
        v2 = jnp.concatenate([gather(vr, sl) for _, vr, sl in key_parts], axis=0)
        n_q, n_k = q.shape[0], k2.shape[0]
        v2 = jnp.concatenate([v2, jnp.ones((n_k, LANES), jnp.bfloat16)], axis=1)
        s = lax.dot_general(q2, k2, _NT, preferred_element_type=jnp.float32) + bias
        m_cur = jnp.max(s, axis=-1, keepdims=True)
        if first_branch:
            m_new = jnp.broadcast_to(m_cur, (2 * n_q, LANES))
        else:
            m_prev = jnp.concatenate([gather(m_scr.at[hh], q_slices) for hh in range(2)],
                                     axis=0)
            m_new = jnp.maximum(m_prev, m_cur)
            alpha = jnp.exp2(m_prev - m_new)
        p = jnp.exp2(s - jnp.concatenate([m_new] * (n_k // LANES), axis=1))
        pv = jnp.dot(p.astype(jnp.bfloat16), v2, preferred_element_type=jnp.float32)
        num, den = pv[:, :LANES], pv[:, LANES:]
        if not first_branch:
            l_prev = jnp.concatenate([gather(l_scr.at[hh], q_slices) for hh in range(2)],
                                     axis=0)
            a_prev = jnp.concatenate([gather(acc_scr.at[hh], q_slices) for hh in range(2)],
                                     axis=0)
            den = alpha * l_prev + den
            num = alpha * a_prev + num
        for hh in range(2):
            off = hh * n_q
            for sl in q_slices:
                n = sl.stop - sl.start
                m_scr[hh, sl, :] = m_new[off:off + n]
                l_scr[hh, sl, :] = den[off:off + n]
                acc_scr[hh, sl, :] = num[off:off + n]
                off += n

    def run(tile, r, lo, n):
        start = tile * ROW_TILE + r * DIL_RUN + lo
        return slice(start, start + n)

    def block(cur, prv, prev_chunk, bias_ref, first_branch):
        if prev_chunk:
            parts = [(kp_ref, vp_ref, prv), (kc_ref, vc_ref, cur)]
            bias = bias_ref[seq_start]
        else:
            parts = [(kc_ref, vc_ref, prv), (kc_ref, vc_ref, cur)]
            bias = bias_ref[0]
        chain(cur, parts, bias, first_branch)

    last = TILES_PER_CHUNK - 1
    for r in range(DIL_RES):
        rows = [run(t, r, 0, DIL_RUN) for t in range(TILES_PER_CHUNK)]
        block(rows, rows, True, b16_ref, True)
    for r4 in range(4):
        for t in range(TILES_PER_CHUNK):
            cur = [run(t, r4 + 4 * u, 0, DIL_RUN) for u in range(4)]
            prv = [run(t - 1 if t else last, r4 + 4 * u, 0, DIL_RUN) for u in range(4)]
            block(cur, prv, t == 0, b4_ref, False)
    quarter = DIL_BLOCK // DIL_RES
    per_tile = DIL_RUN // quarter
    for t in range(TILES_PER_CHUNK):
        for qt in range(per_tile):
            cur = [run(t, r, qt * quarter, quarter) for r in range(DIL_RES)]
            if qt:
                prv = [run(t, r, (qt - 1) * quarter, quarter) for r in range(DIL_RES)]
            else:
                prv = [run(t - 1 if t else last, r, DIL_RUN - quarter, quarter)
                       for r in range(DIL_RES)]
            block(cur, prv, t == 0 and qt == 0, b1_ref, False)

    o0 = acc_scr[0] * (1.0 / l_scr[0])
    o1 = acc_scr[1] * (1.0 / l_scr[1])
    o_ref[...] = jnp.where(first, o0, o1).astype(o_ref.dtype)


def _dilated_attention(q, k, v, *, batch, seq):
    n_chunks = seq // DIL_CHUNK
    flat = lambda a: a.reshape(batch * n_chunks, DIL_CHUNK, D_MIX)
    masks = _dilated_masks()
    cur = lambda b, c, p: (b * n_chunks + c, 0, p)
    prev = lambda b, c, p: (b * n_chunks + jnp.maximum(c - 1, 0), 0, p)
    spec = lambda f: pl.BlockSpec((None, DIL_CHUNK, LANES), f)
    mask_specs = [pl.BlockSpec(m.shape, lambda b, c, p: (0, 0, 0)) for m in masks]
    out = pl.pallas_call(
        _dilated_kernel,
        grid=(batch, n_chunks, N_PAIRS),
        in_specs=[spec(cur), spec(prev), spec(cur), spec(prev), spec(cur)] + mask_specs,
        out_specs=spec(cur),
        out_shape=jax.ShapeDtypeStruct((batch * n_chunks, DIL_CHUNK, D_MIX), jnp.bfloat16),
        scratch_shapes=[pltpu.VMEM((2, DIL_CHUNK, LANES), jnp.float32)] * 3,
        compiler_params=_cparams(3),
        name="dilated_attention",
    )(flat(q), flat(k), flat(k), flat(v), flat(v), *masks)
    return out.reshape(batch * seq, D_MIX)


def _out_proj_kernel(h_ref, mix_ref, qm_ref, km_ref, vm_ref, w_ref, out_ref, *,
                     residue_major):
    first = _lane_first_half()
    mem_pairs = []
    for p in range(D_MEMQ // LANES):
        cols = slice(p * LANES, (p + 1) * LANES)
        qm = qm_ref[:, cols]
        zero = jnp.zeros_like(qm)
        km = km_ref[:, cols]
        vm = vm_ref[:, cols]
        outs = []
        for hh in range(2):
            qh = jnp.where(first, qm, zero) if hh == 0 else jnp.where(first, zero, qm)
            s = lax.dot_general(qh, km, _NT, preferred_element_type=jnp.float32)
            mx = jnp.max(s, axis=-1, keepdims=True)
            e = jnp.exp(s - mx)
            pn = (e * (1.0 / jnp.sum(e, axis=-1, keepdims=True))).astype(jnp.bfloat16)
            outs.append(jnp.dot(pn, vm, preferred_element_type=jnp.float32))
        mem_pairs.append(jnp.where(first, outs[0], outs[1]).astype(jnp.bfloat16))
    mem_out = jnp.concatenate(mem_pairs, axis=-1)

    y = jnp.dot(mix_ref[...], w_ref[:D_MIX, :], preferred_element_type=jnp.float32)
    y = y + jnp.dot(mem_out, w_ref[D_MIX:, :], preferred_element_type=jnp.float32)
    if residue_major:
        for r in range(DIL_RES):
            out_ref[:, r, :] = h_ref[:, r, :] + y[r * DIL_RUN:(r + 1) * DIL_RUN, :]
    else:
        out_ref[...] = h_ref[...] + y


def _out_proj(h, mix, qm, km, vm, w_out, layer, *, seq, mem_tokens, residue_major):
    t = h.shape[0]
    tm = ROW_TILE
    tiles_per_seq = seq // tm
    row = lambda i: (i, 0)
    mem_spec = pl.BlockSpec((None, mem_tokens, D_MEMQ),
                            lambda i: (layer, i // tiles_per_seq, 0))
    h_spec = pl.BlockSpec((tm, D_MODEL), row)
    h_shape = (t, D_MODEL)
    if residue_major:
        h_shape = (t // DIL_RES, DIL_RES, D_MODEL)
        h_spec = pl.BlockSpec((DIL_RUN, DIL_RES, D_MODEL), lambda i: (i, 0, 0))
    out = pl.pallas_call(
        functools.partial(_out_proj_kernel, residue_major=residue_major),
        grid=(t // tm,),
        in_specs=[h_spec,
                  pl.BlockSpec((tm, D_MIX), row),
                  pl.BlockSpec((tm, D_MEMQ), row),
                  mem_spec, mem_spec, _layer_spec(w_out, layer)],
        out_specs=h_spec,
        out_shape=jax.ShapeDtypeStruct(h_shape, jnp.float32),
        compiler_params=_cparams(1),
        name="out_proj_residue_major" if residue_major else "out_proj",
    )(h.reshape(h_shape), mix, qm, km, vm, w_out)
    return out.reshape(t, D_MODEL)


def _ffn_kernel(h_ref, g_ref, wup_ref, cw_ref, cb_ref, wdown_ref, gfin_ref, out_ref,
                carry_ref, act_ref, *, tiles_per_seq, final_norm):
    tm = FFN_TILE
    halo = 8
    n_chunks = D_FF // FF_CHUNK
    i = pl.program_id(0)

    @pl.when(i % tiles_per_seq == 0)
    def _():
        carry_ref[...] = jnp.zeros(carry_ref.shape, jnp.float32)

    h = h_ref[...]
    xn = _rms(h, g_ref[...]).astype(jnp.bfloat16)
    row = lax.broadcasted_iota(jnp.int32, (tm, FF_CHUNK), 0)

    def conv(u, cols):
        prev = carry_ref[:, cols]
        back1 = jnp.where(row == 0, prev[halo - 1:halo, :], pltpu.roll(u, 1, 0))
        back2 = jnp.where(row == 0, prev[halo - 2:halo - 1, :],
                          jnp.where(row == 1, prev[halo - 1:halo, :], pltpu.roll(u, 2, 0)))
        carry_ref[:, cols] = u[tm - halo:tm, :]
        w = cw_ref[:, cols]
        c = cb_ref[:, cols] + w[0:1, :] * back2
        c = c + w[1:2, :] * back1
        return c + w[2:3, :] * u

    def val_cols(j):
        return slice(j * FF_CHUNK, (j + 1) * FF_CHUNK)

    def gate_cols(j):
        return slice(D_FF + j * FF_CHUNK, D_FF + (j + 1) * FF_CHUNK)

    def up(j):
        return (jnp.dot(xn, wup_ref[:, val_cols(j)], preferred_element_type=jnp.float32),
                jnp.dot(xn, wup_ref[:, gate_cols(j)], preferred_element_type=jnp.float32))

    u = up(0)
    for j in range(n_chunks):
        u_next = up(j + 1) if j + 1 < n_chunks else None
        val = conv(u[0], val_cols(j))
        gate = conv(u[1], gate_cols(j))
        act_ref[:, val_cols(j)] = (gate * (1.0 / (1.0 + jnp.exp(-gate))) * val
                                   ).astype(jnp.bfloat16)
        u = u_next
    y = h + jnp.dot(act_ref[...], wdown_ref[...], preferred_element_type=jnp.float32)
    if final_norm:
        y = _rms(y, gfin_ref[...])
    out_ref[...] = y


def _ffn(h, g, w_up, conv_w, conv_b, w_down, g_final, layer, *, seq, final_norm):
    t = h.shape[0]
    tm = FFN_TILE
    row = lambda i: (i, 0)
    fixed = lambda i: (0, 0)
    once = dict(pipeline_mode=pl.Buffered(1))
    return pl.pallas_call(
        functools.partial(_ffn_kernel, tiles_per_seq=seq // tm, final_norm=final_norm),
        grid=(t // tm,),
        in_specs=[pl.BlockSpec((tm, D_MODEL), row),
                  _layer_spec(g, layer),
                  _layer_spec(w_up, layer, **once),
                  _layer_spec(conv_w, layer),
                  _layer_spec(conv_b, layer),
                  _layer_spec(w_down, layer, **once),
                  pl.BlockSpec((1, D_MODEL), fixed)],
        out_specs=pl.BlockSpec((tm, D_MODEL), row),
        out_shape=jax.ShapeDtypeStruct((t, D_MODEL), jnp.float32),
        scratch_shapes=[pltpu.VMEM((8, 2 * D_FF), jnp.float32),
                        pltpu.VMEM((tm, D_FF), jnp.bfloat16)],
        compiler_params=_cparams(1),
        name="conv_ffn_final" if final_norm else "conv_ffn",
    )(h, g, w_up, conv_w, conv_b, w_down, g_final)


def _rope_tables(seq):
    inv = 1.0 / (ROPE_THETA ** (jnp.arange(0, HEAD_DIM, 2, dtype=jnp.float32) / HEAD_DIM))
    ang = jnp.arange(seq, dtype=jnp.float32)[:, None] * inv[None, :]
    cos, sin = jnp.cos(ang), jnp.sin(ang)
    reps = LANES // HEAD_DIM
    cos_t = jnp.tile(jnp.concatenate([cos, cos], axis=-1), (1, reps))
    sin_t = jnp.tile(jnp.concatenate([-sin, sin], axis=-1), (1, reps))

    def residue_major(tab):
        tab = tab.reshape(seq // ROW_TILE, DIL_RUN, DIL_RES, LANES)
        return tab.transpose(0, 2, 1, 3).reshape(seq, LANES)

    return residue_major(cos_t), residue_major(sin_t)


def kernel(x, mem, norm_mix, norm_mem, norm_ffn, w_in_fox, b_forget, w_in_dil,
           w_mem_kv, w_out, w_up, conv_w, conv_b, w_down, norm_final):
    batch, seq, _ = x.shape
    assert seq % DIL_CHUNK == 0 and seq % FOX_Q_TILE == 0
    mem_tokens = mem.shape[1]
    depth = norm_mix.shape[0]
    bf = lambda a: a.astype(jnp.bfloat16)
    row3 = lambda a: a.reshape(a.shape[0], 1, a.shape[1])
    h = x.reshape(batch * seq, D_MODEL)
    mem2d = mem.reshape(batch * mem_tokens, D_MODEL)
    rope_tabs = _rope_tables(seq)
    g_mix, g_ffn, g_final = row3(norm_mix), row3(norm_ffn), norm_final.reshape(1, D_MODEL)

    w_fox, w_dil, w_out_b, w_up_b, w_down_b = (bf(w_in_fox), bf(w_in_dil), bf(w_out),
                                               bf(w_up), bf(w_down))
    conv_b3 = row3(conv_b)
    n_fox = w_in_fox.shape[0]
    tail = w_in_fox[:, :, 3 * D_MIX:]
    forget_params = (
        bf(tail[:, :, N_MIX_HEADS:]),
        bf(_forget_lanes(tail[:, :, :N_MIX_HEADS].reshape(n_fox * D_MODEL, N_MIX_HEADS))
           ).reshape(n_fox, D_MODEL, LANES),
        _forget_lanes(b_forget).reshape(n_fox, 1, LANES))
    km, vm = _mem_kv(mem2d, row3(norm_mem), bf(w_mem_kv), mem_tokens=mem_tokens)

    for layer in range(depth):
        kind, slot = layer % 2, layer // 2
        if kind == 0:
            q, k, v, qm, qx, kx = _in_proj(h, g_mix, w_fox, slot, layer,
                                           forget_params=forget_params, seq=seq)
            mix = _fox_attention(q, qx, k, kx, v, batch=batch, seq=seq)
        else:
            q, k, v, qm = _in_proj(h, g_mix, w_dil, slot, layer, rope_tabs=rope_tabs, seq=seq)
            mix = _dilated_attention(q, k, v, batch=batch, seq=seq)
        h = _out_proj(h, mix, qm, km, vm, w_out_b, layer,
                      seq=seq, mem_tokens=mem_tokens, residue_major=(kind == 1))
        h = _ffn(h, g_ffn, w_up_b, conv_w, conv_b3, w_down_b, g_final, layer,
                 seq=seq, final_norm=(layer == depth - 1))
    return h.reshape(batch, seq, D_MODEL)
```

```python
import functools

import numpy as np

import jax
import jax.numpy as jnp
from jax import lax
from jax.experimental import pallas as pl
from jax.experimental.pallas import tpu as pltpu

D_MODEL = 1024
HEAD_DIM = 64
N_MEM_HEADS = 4
N_MIX_HEADS = 12
D_MIX = N_MIX_HEADS * HEAD_DIM
D_MEMQ = N_MEM_HEADS * HEAD_DIM
D_FF = 2816
CONV_WIDTH = 3
ROPE_THETA = 10000.0
DIL_BLOCK = 128
NORM_EPS = 1e-6
NEG = -1e30
SCALE = HEAD_DIM ** -0.5
LOG2E = 1.4426950408889634

LANES = 128
N_PAIRS = D_MIX // LANES
VMEM_LIMIT = 48 * 1024 * 1024

ROW_TILE = 512
FFN_TILE = 1024
FOX_Q_TILE = 2048
FOX_K_TILE = 512
FF_CHUNK = 256
DECAY_PIECES = 3
DECAY_LANES_PER_HEAD = 2 * DECAY_PIECES
PIECE_LANES = 16
DIL_CHUNK = 2048
DIL_RES = 16
DIL_RUN = ROW_TILE // DIL_RES
TILES_PER_CHUNK = DIL_CHUNK // ROW_TILE

_NT = (((1,), (1,)), ((), ()))


def _cparams(n_axes):
    return pltpu.CompilerParams(
        dimension_semantics=("arbitrary",) * n_axes,
        vmem_limit_bytes=VMEM_LIMIT)


def _rms(x, g):
    y = x * lax.rsqrt(jnp.mean(x * x, axis=-1, keepdims=True) + NORM_EPS)
    return y * g


def _lane_first_half():
    lane = lax.broadcasted_iota(jnp.int32, (1, LANES), 1)
    return lane < HEAD_DIM


def _split3(x):
    hi = x.astype(jnp.bfloat16)
    rem = x - hi.astype(jnp.float32)
    mid = rem.astype(jnp.bfloat16)
    lo = (rem - mid.astype(jnp.float32)).astype(jnp.bfloat16)
    return hi, mid, lo


def _in_proj_kernel(*refs, rope, forget, tiles_per_seq):
    x_ref, g_ref, w_ref, km_ref, vm_ref = refs[:5]
    pos = 5
    if forget:
        wm_ref, wf_ref, bf_ref, tri_ref, sel_ref, ones_ref = refs[pos:pos + 6]; pos += 6
    if rope:
        cos_ref, sin_ref = refs[pos:pos + 2]; pos += 2
    q_ref, k_ref, v_ref, mo_ref = refs[pos:pos + 4]; pos += 4
    if forget:
        qx_ref, kx_ref, carry_ref = refs[pos:pos + 3]

    if rope:
        x = jnp.concatenate([x_ref[:, r, :] for r in range(DIL_RES)], axis=0)
    else:
        x = x_ref[...]
    xn = _rms(x, g_ref[...]).astype(jnp.bfloat16)
    q_scale = SCALE * LOG2E

    def proj(index):
        width = D_MIX if index < 3 else D_MEMQ
        w = w_ref[:, index * D_MIX:index * D_MIX + width].astype(jnp.bfloat16)
        return jnp.dot(xn, w, preferred_element_type=jnp.float32)

    def mem_attend(qm):
        first = _lane_first_half()
        qm = qm.astype(jnp.bfloat16)
        ones = jnp.ones((km_ref.shape[0], LANES), jnp.bfloat16)
        pairs = []
        for p in range(D_MEMQ // LANES):
            cols = slice(p * LANES, (p + 1) * LANES)
            qp = qm[:, cols]
            n = qp.shape[0]
            zero = jnp.zeros_like(qp)
            q2 = jnp.concatenate([jnp.where(first, qp, zero), jnp.where(first, zero, qp)],
                                 axis=0)
            v2 = jnp.concatenate([vm_ref[:, cols], ones], axis=1)
            s = lax.dot_general(q2, km_ref[:, cols], _NT, preferred_element_type=jnp.float32)
            e = jnp.exp2(s - jnp.max(s, axis=-1, keepdims=True))
            pv = jnp.dot(e.astype(jnp.bfloat16), v2, preferred_element_type=jnp.float32)
            o = pv[:, :LANES] * (1.0 / pv[:, LANES:])
            pairs.append(jnp.where(first, o[:n], o[n:]).astype(jnp.bfloat16))
        return jnp.concatenate(pairs, axis=-1)

    def rotate(t):
        lane = lax.broadcasted_iota(jnp.int32, (1, LANES), 1)
        first = (lane % HEAD_DIM) < (HEAD_DIM // 2)
        partner = jnp.where(first, pltpu.roll(t, LANES - HEAD_DIM // 2, 1),
                            pltpu.roll(t, HEAD_DIM // 2, 1))
        return t * cos_ref[...] + partner * sin_ref[...]

    if rope:
        q = proj(0)
        for p in range(N_PAIRS):
            cols = slice(p * LANES, (p + 1) * LANES)
            q_ref[:, cols] = (rotate(q[:, cols]) * q_scale).astype(jnp.bfloat16)
        k = proj(1)
        for p in range(N_PAIRS):
            cols = slice(p * LANES, (p + 1) * LANES)
            k_ref[:, cols] = rotate(k[:, cols]).astype(jnp.bfloat16)
        v_ref[...] = proj(2).astype(jnp.bfloat16)
        mo_ref[...] = mem_attend(proj(3) * q_scale)
        return

    @pl.when(pl.program_id(0) % tiles_per_seq == 0)
    def _():
        carry_ref[...] = jnp.zeros(carry_ref.shape, jnp.float32)

    f = jnp.dot(xn, wf_ref[...], preferred_element_type=jnp.float32) + bf_ref[...]
    mem_q = jnp.dot(xn, wm_ref[...], preferred_element_type=jnp.float32) * q_scale
    q_ref[...] = (proj(0) * q_scale).astype(jnp.bfloat16)
    log_f = jnp.minimum(f, 0.0) - jnp.log1p(jnp.exp(-jnp.abs(f)))
    lane = lax.broadcasted_iota(jnp.int32, (1, LANES), 1)
    hi, mid, lo = _split3(log_f)
    packed = jnp.where(lane < PIECE_LANES, hi, jnp.where(lane < 2 * PIECE_LANES, mid, lo))
    c3 = carry_ref[0:1, :] + jnp.dot(tri_ref[...], packed, preferred_element_type=jnp.float32)
    carry_ref[0:1, :] = c3[c3.shape[0] - 1:, :]
    c = (c3 + pltpu.roll(c3, LANES - PIECE_LANES, 1)) + pltpu.roll(c3, LANES - 2 * PIECE_LANES, 1)
    mo_ref[...] = mem_attend(mem_q)
    k_ref[...] = proj(1).astype(jnp.bfloat16)
    hi, mid, lo = _split3(c * LOG2E)
    packed = jnp.where(lane < PIECE_LANES, hi,
                       jnp.where(lane < 2 * PIECE_LANES, pltpu.roll(mid, PIECE_LANES, 1),
                                 pltpu.roll(lo, 2 * PIECE_LANES, 1)))
    x_decay = ones_ref[...] + jnp.dot(packed, sel_ref[...], preferred_element_type=jnp.float32)
    qx_ref[...] = x_decay[:, :LANES].astype(jnp.bfloat16)
    kx_ref[...] = x_decay[:, LANES:].astype(jnp.bfloat16)
    v_ref[...] = proj(2).astype(jnp.bfloat16)


def _decay_selectors():
    sel = np.zeros((LANES, 2 * LANES), np.float32)
    ones = np.zeros((1, 2 * LANES), np.float32)
    for h in range(N_MIX_HEADS):
        base = h * DECAY_LANES_PER_HEAD
        for piece in range(DECAY_PIECES):
            sel[piece * PIECE_LANES + h, base + piece] = 1.0
            sel[piece * PIECE_LANES + h, LANES + base + DECAY_PIECES + piece] = -1.0
            ones[0, base + DECAY_PIECES + piece] = 1.0
            ones[0, LANES + base + piece] = 1.0
    return jnp.asarray(sel, jnp.bfloat16), jnp.asarray(ones)


def _forget_lanes(a):
    a = jnp.pad(a, ((0, 0), (0, PIECE_LANES - N_MIX_HEADS)))
    a = jnp.tile(a, (1, DECAY_PIECES))
    return jnp.pad(a, ((0, 0), (0, LANES - DECAY_PIECES * PIECE_LANES)))


def _layer_spec(stacked, index, **kwargs):
    return pl.BlockSpec((None,) + stacked.shape[1:], lambda *_: (index, 0, 0), **kwargs)


def _in_proj(h, g, w, slot, layer, km, vm, forget_params=None, rope_tabs=None, *, seq):
    t = h.shape[0]
    tm = ROW_TILE
    forget = forget_params is not None
    rope = rope_tabs is not None
    row = lambda i: (i, 0)
    fixed = lambda i: (0, 0)
    x_spec = pl.BlockSpec((tm, D_MODEL), row)
    if rope:
        h = h.reshape(t // DIL_RES, DIL_RES, D_MODEL)
        x_spec = pl.BlockSpec((DIL_RUN, DIL_RES, D_MODEL), lambda i: (i, 0, 0))
    tiles_per_seq = seq // tm
    mem_tokens = km.shape[1] // (t // seq)
    mem_spec = pl.BlockSpec((None, mem_tokens, D_MEMQ),
                            lambda i: (layer, i // tiles_per_seq, 0))
    once = dict(pipeline_mode=pl.Buffered(1))
    in_specs = [x_spec, _layer_spec(g, layer), _layer_spec(w, slot, **once), mem_spec, mem_spec]
    args = [h, g, w, km, vm]
    out_shape = [jax.ShapeDtypeStruct((t, D_MIX), jnp.bfloat16)] * 3
    out_shape += [jax.ShapeDtypeStruct((t, D_MEMQ), jnp.bfloat16)]
    out_specs = [pl.BlockSpec((tm, D_MIX), row)] * 3
    out_specs += [pl.BlockSpec((tm, D_MEMQ), row)]
    scratch = []
    if forget:
        w_m, w_f, b_f = forget_params
        sel, ones = _decay_selectors()
        tri = jnp.asarray(np.tril(np.ones((tm, tm), np.float32)), jnp.bfloat16)
        in_specs += [_layer_spec(w_m, slot), _layer_spec(w_f, slot), _layer_spec(b_f, slot),
                     pl.BlockSpec((tm, tm), fixed),
                     pl.BlockSpec((LANES, 2 * LANES), fixed),
                     pl.BlockSpec((1, 2 * LANES), fixed)]
        args += [w_m, w_f, b_f, tri, sel, ones]
        out_shape += [jax.ShapeDtypeStruct((t, LANES), jnp.bfloat16)] * 2
        out_specs += [pl.BlockSpec((tm, LANES), row)] * 2
        scratch.append(pltpu.VMEM((8, LANES), jnp.float32))
    if rope:
        tab = lambda i: (i % tiles_per_seq, 0)
        in_specs += [pl.BlockSpec((tm, LANES), tab), pl.BlockSpec((tm, LANES), tab)]
        args += list(rope_tabs)
    return pl.pallas_call(
        functools.partial(_in_proj_kernel, rope=rope, forget=forget,
                          tiles_per_seq=tiles_per_seq),
        grid=(t // tm,),
        in_specs=in_specs, out_specs=out_specs, out_shape=out_shape,
        scratch_shapes=scratch,
        compiler_params=_cparams(1),
        name="in_proj_rope" if rope else "in_proj_fox",
    )(*args)


def _mem_kv_kernel(x_ref, g_ref, w_ref, km_ref, vm_ref):
    xn = _rms(x_ref[...], g_ref[...]).astype(jnp.bfloat16)
    kv = jnp.dot(xn, w_ref[...].astype(jnp.bfloat16), preferred_element_type=jnp.float32)
    km_ref[...] = kv[:, :D_MEMQ].astype(jnp.bfloat16)
    vm_ref[...] = kv[:, D_MEMQ:].astype(jnp.bfloat16)


def _mem_kv(mem2d, g, w, *, mem_tokens):
    rows = mem2d.shape[0]
    depth = w.shape[0]
    per_layer = lambda shape: pl.BlockSpec((None,) + shape, lambda l, b: (l, 0, 0))
    out_spec = pl.BlockSpec((None, mem_tokens, D_MEMQ), lambda l, b: (l, b, 0))
    return pl.pallas_call(
        _mem_kv_kernel,
        grid=(depth, rows // mem_tokens),
        in_specs=[pl.BlockSpec((mem_tokens, D_MODEL), lambda l, b: (b, 0)),
                  per_layer((1, D_MODEL)),
                  per_layer((D_MODEL, 2 * D_MEMQ))],
        out_specs=[out_spec, out_spec],
        out_shape=[jax.ShapeDtypeStruct((depth, rows, D_MEMQ), jnp.bfloat16)] * 2,
        compiler_params=_cparams(2),
        name="mem_kv",
    )(mem2d, g, w)


def _fox_kernel(q_ref, qx_ref, k_ref, kx_ref, v_ref, o_ref, m_scr, l_scr, acc_scr):
    tq, tk = FOX_Q_TILE, FOX_K_TILE
    n_blocks = tq // tk
    pair = pl.program_id(1)
    qi = pl.program_id(2)
    lane = lax.broadcasted_iota(jnp.int32, (1, LANES), 1)
    first = lane < HEAD_DIM
    head_lanes = (first, jnp.logical_not(first))
    q = q_ref[...]
    qx = qx_ref[...]
    zero = jnp.zeros_like(q)
    q_heads = []
    for hh in range(2):
        lo = (2 * pair + hh) * DECAY_LANES_PER_HEAD
        own_decay = (lane >= lo) & (lane < lo + DECAY_LANES_PER_HEAD)
        q_heads.append(jnp.concatenate([jnp.where(head_lanes[hh], q, zero),
                                        jnp.where(own_decay, qx, zero)], axis=1))

    m_scr[...] = jnp.full(m_scr.shape, NEG, jnp.float32)
    l_scr[...] = jnp.zeros(l_scr.shape, jnp.float32)
    acc_scr[...] = jnp.zeros(acc_scr.shape, jnp.float32)
    ones = jnp.ones((tk, LANES), jnp.bfloat16)

    def load_kv(j):
        start = pl.multiple_of(j * tk, tk)
        k = jnp.concatenate([k_ref[pl.ds(start, tk), :], kx_ref[pl.ds(start, tk), :]],
                            axis=1)
        v = jnp.concatenate([v_ref[pl.ds(start, tk), :], ones], axis=1)
        return k, v

    def chain(hh, r, k, v, diagonal):
        rows = slice(r * tk, (r + 1) * tk)
        s = lax.dot_general(q_heads[hh][rows], k, _NT, preferred_element_type=jnp.float32)
        if diagonal:
            row = lax.broadcasted_iota(jnp.int32, (tk, tk), 0)
            col = lax.broadcasted_iota(jnp.int32, (tk, tk), 1)
            s = jnp.where(col <= row, s, NEG)
        m_prev = m_scr[hh, rows, :]
        m_new = jnp.maximum(m_prev, jnp.max(s, axis=-1, keepdims=True))
        alpha = jnp.exp2(m_prev - m_new)
        p = jnp.exp2(s - jnp.concatenate([m_new] * (tk // LANES), axis=1))
        pv = jnp.dot(p.astype(jnp.bfloat16), v, preferred_element_type=jnp.float32)
        l_scr[hh, rows, :] = alpha * l_scr[hh, rows, :] + pv[:, LANES:]
        acc_scr[hh, rows, :] = alpha * acc_scr[hh, rows, :] + pv[:, :LANES]
        m_scr[hh, rows, :] = m_new

    def body(j, carry):
        for half in range(2):
            k, v = load_kv(2 * j + half)
            for r in range(n_blocks):
                for hh in range(2):
                    chain(hh, r, k, v, False)
        return carry

    assert n_blocks % 2 == 0
    n_before = qi * n_blocks
    lax.fori_loop(0, n_before // 2, body, 0)
    for d in range(n_blocks):
        k, v = load_kv(n_before + d)
        for r in range(d, n_blocks):
            for hh in range(2):
                chain(hh, r, k, v, r == d)
    o0 = acc_scr[0] * (1.0 / l_scr[0])
    o1 = acc_scr[1] * (1.0 / l_scr[1])
    o_ref[...] = jnp.where(first, o0, o1).astype(o_ref.dtype)


def _fox_attention(q, qx, k, kx, v, *, batch, seq):
    tq = FOX_Q_TILE
    as_seq = lambda a: a.reshape(batch, seq, a.shape[-1])
    q_spec = pl.BlockSpec((None, tq, LANES), lambda b, p, i: (b, i, p))
    qx_spec = pl.BlockSpec((None, tq, LANES), lambda b, p, i: (b, i, 0))
    kv_spec = pl.BlockSpec((None, seq, LANES), lambda b, p, i: (b, 0, p))
    kx_spec = pl.BlockSpec((None, seq, LANES), lambda b, p, i: (b, 0, 0))
    out = pl.pallas_call(
        _fox_kernel,
        grid=(batch, N_PAIRS, seq // tq),
        in_specs=[q_spec, qx_spec, kv_spec, kx_spec, kv_spec],
        out_specs=q_spec,
        out_shape=jax.ShapeDtypeStruct((batch, seq, D_MIX), jnp.bfloat16),
        scratch_shapes=[pltpu.VMEM((2, tq, LANES), jnp.float32),
                        pltpu.VMEM((2, tq, LANES), jnp.float32),
                        pltpu.VMEM((2, tq, LANES), jnp.float32)],
        compiler_params=_cparams(3),
        name="fox_attention",
    )(as_seq(q), as_seq(qx), as_seq(k), as_seq(kx), as_seq(v))
    return out.reshape(batch * seq, D_MIX)


def _band_bias(n_q, entry_q, entry_k):
    a = np.arange(2 * n_q)[:, None] % n_q
    c = np.arange(2 * n_q)[None, :]
    dist = entry_q(a) - entry_k(c)
    ok = (dist >= 0) & (dist <= DIL_BLOCK)
    planes = [ok, ok & (c >= n_q)]
    return jnp.asarray(np.where(np.stack(planes), 0.0, NEG), jnp.float32)


def _dilated_masks():
    blk = DIL_BLOCK
    b16 = _band_bias(blk, lambda a: a + blk, lambda c: c)
    run4 = 32
    b4 = _band_bias(blk, lambda a: blk + 4 * (a % run4) + a // run4,
                    lambda c: (c // blk) * blk + 4 * ((c % blk) % run4) + (c % blk) // run4)
    run1 = blk // DIL_RES
    b1 = _band_bias(blk, lambda a: blk + DIL_RES * (a % run1) + a // run1,
                    lambda c: (c // blk) * blk + DIL_RES * ((c % blk) % run1) + (c % blk) // run1)
    return b16, b4, b1


def _dilated_kernel(q_ref, kp_ref, kc_ref, vp_ref, vc_ref, b16_ref, b4_ref, b1_ref,
                    o_ref, m_scr, l_scr, acc_scr):
    first = _lane_first_half()
    seq_start = jnp.where(pl.program_id(1) > 0, 0, 1)

    def gather(ref, slices):
        return jnp.concatenate([ref[s, :] for s in slices], axis=0)

    def chain(q_slices, key_parts, bias, first_branch):
        q = gather(q_ref, q_slices)
        zero = jnp.zeros_like(q)
        q2 = jnp.concatenate([jnp.where(first, q, zero), jnp.where(first, zero, q)], axis=0)
        k2 = jnp.concatenate([gather(kr, sl) for kr, _, sl in key_parts], axis=0)
        v2 = jnp.concatenate([gather(vr, sl) for _, vr, sl in key_parts], axis=0)
        n_q, n_k = q.shape[0], k2.shape[0]
        v2 = jnp.concatenate([v2, jnp.ones((n_k, LANES), jnp.bfloat16)], axis=1)
        s = lax.dot_general(q2, k2, _NT, preferred_element_type=jnp.float32) + bias
        m_cur = jnp.max(s, axis=-1, keepdims=True)
        if first_branch:
            m_new = jnp.broadcast_to(m_cur, (2 * n_q, LANES))
        else:
            m_prev = jnp.concatenate([gather(m_scr.at[hh], q_slices) for hh in range(2)],
                                     axis=0)
            m_new = jnp.maximum(m_prev, m_cur)
            alpha = jnp.exp2(m_prev - m_new)
        p = jnp.exp2(s - jnp.concatenate([m_new] * (n_k // LANES), axis=1))
        pv = jnp.dot(p.astype(jnp.bfloat16), v2, preferred_element_type=jnp.float32)
        num, den = pv[:, :LANES], pv[:, LANES:]
        if not first_branch:
            l_prev = jnp.concatenate([gather(l_scr.at[hh], q_slices) for hh in range(2)],
                                     axis=0)
            a_prev = jnp.concatenate([gather(acc_scr.at[hh], q_slices) for hh in range(2)],
                                     axis=0)
            den = alpha * l_prev + den
            num = alpha * a_prev + num
        for hh in range(2):
            off = hh * n_q
            for sl in q_slices:
                n = sl.stop - sl.start
                m_scr[hh, sl, :] = m_new[off:off + n]
                l_scr[hh, sl, :] = den[off:off + n]
                acc_scr[hh, sl, :] = num[off:off + n]
                off += n

    def run(tile, r, lo, n):
        start = tile * ROW_TILE + r * DIL_RUN + lo
        return slice(start, start + n)

    def block(cur, prv, prev_chunk, bias_ref, first_branch):
        if prev_chunk:
            parts = [(kp_ref, vp_ref, prv), (kc_ref, vc_ref, cur)]
            bias = bias_ref[seq_start]
        else:
            parts = [(kc_ref, vc_ref, prv), (kc_ref, vc_ref, cur)]
            bias = bias_ref[0]
        chain(cur, parts, bias, first_branch)

    last = TILES_PER_CHUNK - 1
    for r in range(DIL_RES):
        rows = [run(t, r, 0, DIL_RUN) for t in range(TILES_PER_CHUNK)]
        block(rows, rows, True, b16_ref, True)
    for r4 in range(4):
        for t in range(TILES_PER_CHUNK):
            cur = [run(t, r4 + 4 * u, 0, DIL_RUN) for u in range(4)]
            prv = [run(t - 1 if t else last, r4 + 4 * u, 0, DIL_RUN) for u in range(4)]
            block(cur, prv, t == 0, b4_ref, False)
    quarter = DIL_BLOCK // DIL_RES
    per_tile = DIL_RUN // quarter
    for t in range(TILES_PER_CHUNK):
        for qt in range(per_tile):
            cur = [run(t, r, qt * quarter, quarter) for r in range(DIL_RES)]
            if qt:
                prv = [run(t, r, (qt - 1) * quarter, quarter) for r in range(DIL_RES)]
            else:
                prv = [run(t - 1 if t else last, r, DIL_RUN - quarter, quarter)
                       for r in range(DIL_RES)]
            block(cur, prv, t == 0 and qt == 0, b1_ref, False)

    o0 = acc_scr[0] * (1.0 / l_scr[0])
    o1 = acc_scr[1] * (1.0 / l_scr[1])
    o_ref[...] = jnp.where(first, o0, o1).astype(o_ref.dtype)


def _dilated_attention(q, k, v, *, batch, seq):
    n_chunks = seq // DIL_CHUNK
    flat = lambda a: a.reshape(batch * n_chunks, DIL_CHUNK, D_MIX)
    masks = _dilated_masks()
    cur = lambda b, c, p: (b * n_chunks + c, 0, p)
    prev = lambda b, c, p: (b * n_chunks + jnp.maximum(c - 1, 0), 0, p)
    spec = lambda f: pl.BlockSpec((None, DIL_CHUNK, LANES), f)
    mask_specs = [pl.BlockSpec(m.shape, lambda b, c, p: (0, 0, 0)) for m in masks]
    out = pl.pallas_call(
        _dilated_kernel,
        grid=(batch, n_chunks, N_PAIRS),
        in_specs=[spec(cur), spec(prev), spec(cur), spec(prev), spec(cur)] + mask_specs,
        out_specs=spec(cur),
        out_shape=jax.ShapeDtypeStruct((batch * n_chunks, DIL_CHUNK, D_MIX), jnp.bfloat16),
        scratch_shapes=[pltpu.VMEM((2, DIL_CHUNK, LANES), jnp.float32)] * 3,
        compiler_params=_cparams(3),
        name="dilated_attention",
    )(flat(q), flat(k), flat(k), flat(v), flat(v), *masks)
    return out.reshape(batch * seq, D_MIX)


def _out_proj_kernel(h_ref, mix_ref, mem_ref, w_ref, out_ref, *, residue_major):
    heads = jnp.concatenate([mix_ref[...], mem_ref[...]], axis=1)
    y = jnp.dot(heads, w_ref[...].astype(jnp.bfloat16), preferred_element_type=jnp.float32)
    if residue_major:
        for r in range(DIL_RES):
            out_ref[:, r, :] = h_ref[:, r, :] + y[r * DIL_RUN:(r + 1) * DIL_RUN, :]
    else:
        out_ref[...] = h_ref[...] + y


def _out_proj(h, mix, mem_out, w_out, layer, *, residue_major):
    t = h.shape[0]
    tm = ROW_TILE
    row = lambda i: (i, 0)
    h_spec = pl.BlockSpec((tm, D_MODEL), row)
    h_shape = (t, D_MODEL)
    if residue_major:
        h_shape = (t // DIL_RES, DIL_RES, D_MODEL)
        h_spec = pl.BlockSpec((DIL_RUN, DIL_RES, D_MODEL), lambda i: (i, 0, 0))
    out = pl.pallas_call(
        functools.partial(_out_proj_kernel, residue_major=residue_major),
        grid=(t // tm,),
        in_specs=[h_spec,
                  pl.BlockSpec((tm, D_MIX), row),
                  pl.BlockSpec((tm, D_MEMQ), row),
                  _layer_spec(w_out, layer, pipeline_mode=pl.Buffered(1))],
        out_specs=h_spec,
        out_shape=jax.ShapeDtypeStruct(h_shape, jnp.float32),
        compiler_params=_cparams(1),
        name="out_proj_residue_major" if residue_major else "out_proj",
    )(h.reshape(h_shape), mix, mem_out, w_out)
    return out.reshape(t, D_MODEL)


def _ffn_kernel(h_ref, g_ref, wup_ref, cw_ref, cb_ref, wdown_ref, gfin_ref, out_ref,
                carry_ref, act_ref, *, tiles_per_seq, final_norm):
    tm = FFN_TILE
    halo = 8
    n_chunks = D_FF // FF_CHUNK
    i = pl.program_id(0)

    @pl.when(i % tiles_per_seq == 0)
    def _():
        carry_ref[...] = jnp.zeros(carry_ref.shape, jnp.float32)

    h = h_ref[...]
    xn = _rms(h, g_ref[...]).astype(jnp.bfloat16)
    row = lax.broadcasted_iota(jnp.int32, (tm, FF_CHUNK), 0)

    def conv(u, cols):
        prev = carry_ref[:, cols]
        back1 = jnp.where(row == 0, prev[halo - 1:halo, :], pltpu.roll(u, 1, 0))
        back2 = jnp.where(row == 0, prev[halo - 2:halo - 1, :],
                          jnp.where(row == 1, prev[halo - 1:halo, :], pltpu.roll(u, 2, 0)))
        carry_ref[:, cols] = u[tm - halo:tm, :]
        w = cw_ref[:, cols]
        c = cb_ref[:, cols] + w[0:1, :] * back2
        c = c + w[1:2, :] * back1
        return c + w[2:3, :] * u

    def val_cols(j):
        return slice(j * FF_CHUNK, (j + 1) * FF_CHUNK)

    def gate_cols(j):
        return slice(D_FF + j * FF_CHUNK, D_FF + (j + 1) * FF_CHUNK)

    def up(j):
        return (jnp.dot(xn, wup_ref[:, val_cols(j)], preferred_element_type=jnp.float32),
                jnp.dot(xn, wup_ref[:, gate_cols(j)], preferred_element_type=jnp.float32))

    u = up(0)
    for j in range(n_chunks):
        u_next = up(j + 1) if j + 1 < n_chunks else None
        val = conv(u[0], val_cols(j))
        gate = conv(u[1], gate_cols(j))
        act_ref[:, val_cols(j)] = (gate * (1.0 / (1.0 + jnp.exp(-gate))) * val
                                   ).astype(jnp.bfloat16)
        u = u_next
    y = h + jnp.dot(act_ref[...], wdown_ref[...], preferred_element_type=jnp.float32)
    if final_norm:
        y = _rms(y, gfin_ref[...])
    out_ref[...] = y


def _ffn(h, g, w_up, conv_w, conv_b, w_down, g_final, layer, *, seq, final_norm):
    t = h.shape[0]
    tm = FFN_TILE
    row = lambda i: (i, 0)
    fixed = lambda i: (0, 0)
    once = dict(pipeline_mode=pl.Buffered(1))
    return pl.pallas_call(
        functools.partial(_ffn_kernel, tiles_per_seq=seq // tm, final_norm=final_norm),
        grid=(t // tm,),
        in_specs=[pl.BlockSpec((tm, D_MODEL), row),
                  _layer_spec(g, layer),
                  _layer_spec(w_up, layer, **once),
                  _layer_spec(conv_w, layer),
                  _layer_spec(conv_b, layer),
                  _layer_spec(w_down, layer, **once),
                  pl.BlockSpec((1, D_MODEL), fixed)],
        out_specs=pl.BlockSpec((tm, D_MODEL), row),
        out_shape=jax.ShapeDtypeStruct((t, D_MODEL), jnp.float32),
        scratch_shapes=[pltpu.VMEM((8, 2 * D_FF), jnp.float32),
                        pltpu.VMEM((tm, D_FF), jnp.bfloat16)],
        compiler_params=_cparams(1),
        name="conv_ffn_final" if final_norm else "conv_ffn",
    )(h, g, w_up, conv_w, conv_b, w_down, g_final)


def _rope_tables(seq):
    inv = 1.0 / (ROPE_THETA ** (jnp.arange(0, HEAD_DIM, 2, dtype=jnp.float32) / HEAD_DIM))
    ang = jnp.arange(seq, dtype=jnp.float32)[:, None] * inv[None, :]
    cos, sin = jnp.cos(ang), jnp.sin(ang)
    reps = LANES // HEAD_DIM
    cos_t = jnp.tile(jnp.concatenate([cos, cos], axis=-1), (1, reps))
    sin_t = jnp.tile(jnp.concatenate([-sin, sin], axis=-1), (1, reps))

    def residue_major(tab):
        tab = tab.reshape(seq // ROW_TILE, DIL_RUN, DIL_RES, LANES)
        return tab.transpose(0, 2, 1, 3).reshape(seq, LANES)

    return residue_major(cos_t), residue_major(sin_t)


def kernel(x, mem, norm_mix, norm_mem, norm_ffn, w_in_fox, b_forget, w_in_dil,
           w_mem_kv, w_out, w_up, conv_w, conv_b, w_down, norm_final):
    batch, seq, _ = x.shape
    assert seq % DIL_CHUNK == 0 and seq % FOX_Q_TILE == 0
    mem_tokens = mem.shape[1]
    depth = norm_mix.shape[0]
    bf = lambda a: a.astype(jnp.bfloat16)
    row3 = lambda a: a.reshape(a.shape[0], 1, a.shape[1])
    h = x.reshape(batch * seq, D_MODEL)
    mem2d = mem.reshape(batch * mem_tokens, D_MODEL)
    rope_tabs = _rope_tables(seq)
    g_mix, g_ffn, g_final = row3(norm_mix), row3(norm_ffn), norm_final.reshape(1, D_MODEL)

    w_up_b, w_down_b = bf(w_up), bf(w_down)
    conv_b3 = row3(conv_b)
    n_fox = w_in_fox.shape[0]
    tail = w_in_fox[:, :, 3 * D_MIX:]
    forget_params = (
        bf(tail[:, :, N_MIX_HEADS:]),
        bf(_forget_lanes(tail[:, :, :N_MIX_HEADS].reshape(n_fox * D_MODEL, N_MIX_HEADS))
           ).reshape(n_fox, D_MODEL, LANES),
        _forget_lanes(b_forget).reshape(n_fox, 1, LANES))
    km, vm = _mem_kv(mem2d, row3(norm_mem), w_mem_kv, mem_tokens=mem_tokens)

    for layer in range(depth):
        kind, slot = layer % 2, layer // 2
        if kind == 0:
            q, k, v, mem_out, qx, kx = _in_proj(h, g_mix, w_in_fox, slot, layer, km, vm,
                                                forget_params=forget_params, seq=seq)
            mix = _fox_attention(q, qx, k, kx, v, batch=batch, seq=seq)
        else:
            q, k, v, mem_out = _in_proj(h, g_mix, w_in_dil, slot, layer, km, vm,
                                        rope_tabs=rope_tabs, seq=seq)
            mix = _dilated_attention(q, k, v, batch=batch, seq=seq)
        h = _out_proj(h, mix, mem_out, w_out, layer, residue_major=(kind == 1))
        h = _ffn(h, g_ffn, w_up_b, conv_w, conv_b3, w_down_b, g_final, layer,
                 seq=seq, final_norm=(layer == depth - 1))
    return h.reshape(batch, seq, D_MODEL)
```

```python
import functools

import numpy as np

import jax
import jax.numpy as jnp
from jax import lax
from jax.experimental import pallas as pl
from jax.experimental.pallas import tpu as pltpu

D_MODEL = 1024
HEAD_DIM = 64
N_MEM_HEADS = 4
N_MIX_HEADS = 12
D_MIX = N_MIX_HEADS * HEAD_DIM
D_MEMQ = N_MEM_HEADS * HEAD_DIM
D_FF = 2816
CONV_WIDTH = 3
ROPE_THETA = 10000.0
DIL_BLOCK = 128
NORM_EPS = 1e-6
NEG = -1e30
SCALE = HEAD_DIM ** -0.5
LOG2E = 1.4426950408889634

LANES = 128
N_PAIRS = D_MIX // LANES
VMEM_LIMIT = 48 * 1024 * 1024

ROW_TILE = 512
FFN_TILE = 1024
FOX_Q_TILE = 2048
FOX_K_TILE = 512
FF_CHUNK = 256
DECAY_PIECES = 3
DECAY_LANES_PER_HEAD = 2 * DECAY_PIECES
PIECE_LANES = 16
DIL_CHUNK = 2048
DIL_RES = 16
DIL_RUN = ROW_TILE // DIL_RES
TILES_PER_CHUNK = DIL_CHUNK // ROW_TILE

_NT = (((1,), (1,)), ((), ()))


def _cparams(n_axes):
    return pltpu.CompilerParams(
        dimension_semantics=("arbitrary",) * n_axes,
        vmem_limit_bytes=VMEM_LIMIT)


def _rms(x, g):
    y = x * lax.rsqrt(jnp.mean(x * x, axis=-1, keepdims=True) + NORM_EPS)
    return y * g


def _lane_first_half():
    lane = lax.broadcasted_iota(jnp.int32, (1, LANES), 1)
    return lane < HEAD_DIM


def _split3(x):
    hi = x.astype(jnp.bfloat16)
    rem = x - hi.astype(jnp.float32)
    mid = rem.astype(jnp.bfloat16)
    lo = (rem - mid.astype(jnp.float32)).astype(jnp.bfloat16)
    return hi, mid, lo


def _in_proj_kernel(*refs, rope, forget, tiles_per_seq):
    x_ref, g_ref, w_ref = refs[:3]
    pos = 3
    if forget:
        wm_ref, wf_ref, bf_ref, tri_ref, sel_ref, ones_ref = refs[pos:pos + 6]; pos += 6
    if rope:
        cos_ref, sin_ref = refs[pos:pos + 2]; pos += 2
    q_ref, k_ref, v_ref, qm_ref = refs[pos:pos + 4]; pos += 4
    if forget:
        qx_ref, kx_ref, carry_ref = refs[pos:pos + 3]

    if rope:
        x = jnp.concatenate([x_ref[:, r, :] for r in range(DIL_RES)], axis=0)
    else:
        x = x_ref[...]
    xn = _rms(x, g_ref[...]).astype(jnp.bfloat16)
    q_scale = SCALE * LOG2E

    def proj(index):
        width = D_MIX if index < 3 else D_MEMQ
        w = w_ref[:, index * D_MIX:index * D_MIX + width].astype(jnp.bfloat16)
        return jnp.dot(xn, w, preferred_element_type=jnp.float32)

    def rotate(t):
        lane = lax.broadcasted_iota(jnp.int32, (1, LANES), 1)
        first = (lane % HEAD_DIM) < (HEAD_DIM // 2)
        partner = jnp.where(first, pltpu.roll(t, LANES - HEAD_DIM // 2, 1),
                            pltpu.roll(t, HEAD_DIM // 2, 1))
        return t * cos_ref[...] + partner * sin_ref[...]

    if rope:
        q = proj(0)
        for p in range(N_PAIRS):
            cols = slice(p * LANES, (p + 1) * LANES)
            q_ref[:, cols] = (rotate(q[:, cols]) * q_scale).astype(jnp.bfloat16)
        k = proj(1)
        for p in range(N_PAIRS):
            cols = slice(p * LANES, (p + 1) * LANES)
            k_ref[:, cols] = rotate(k[:, cols]).astype(jnp.bfloat16)
        v_ref[...] = proj(2).astype(jnp.bfloat16)
        qm_ref[...] = (proj(3) * q_scale).astype(jnp.bfloat16)
        return

    @pl.when(pl.program_id(0) % tiles_per_seq == 0)
    def _():
        carry_ref[...] = jnp.zeros(carry_ref.shape, jnp.float32)

    f = jnp.dot(xn, wf_ref[...], preferred_element_type=jnp.float32) + bf_ref[...]
    q_ref[...] = (proj(0) * q_scale).astype(jnp.bfloat16)
    log_f = jnp.minimum(f, 0.0) - jnp.log1p(jnp.exp(-jnp.abs(f)))
    lane = lax.broadcasted_iota(jnp.int32, (1, LANES), 1)
    hi, mid, lo = _split3(log_f)
    packed = jnp.where(lane < PIECE_LANES, hi, jnp.where(lane < 2 * PIECE_LANES, mid, lo))
    c3 = carry_ref[0:1, :] + jnp.dot(tri_ref[...], packed, preferred_element_type=jnp.float32)
    carry_ref[0:1, :] = c3[c3.shape[0] - 1:, :]
    c = (c3 + pltpu.roll(c3, LANES - PIECE_LANES, 1)) + pltpu.roll(c3, LANES - 2 * PIECE_LANES, 1)
    k_ref[...] = proj(1).astype(jnp.bfloat16)
    hi, mid, lo = _split3(c * LOG2E)
    packed = jnp.where(lane < PIECE_LANES, hi,
                       jnp.where(lane < 2 * PIECE_LANES, pltpu.roll(mid, PIECE_LANES, 1),
                                 pltpu.roll(lo, 2 * PIECE_LANES, 1)))
    x_decay = ones_ref[...] + jnp.dot(packed, sel_ref[...], preferred_element_type=jnp.float32)
    qx_ref[...] = x_decay[:, :LANES].astype(jnp.bfloat16)
    kx_ref[...] = x_decay[:, LANES:].astype(jnp.bfloat16)
    v_ref[...] = proj(2).astype(jnp.bfloat16)
    qm_ref[...] = (jnp.dot(xn, wm_ref[...], preferred_element_type=jnp.float32) * q_scale
                   ).astype(jnp.bfloat16)


def _decay_selectors():
    sel = np.zeros((LANES, 2 * LANES), np.float32)
    ones = np.zeros((1, 2 * LANES), np.float32)
    for h in range(N_MIX_HEADS):
        base = h * DECAY_LANES_PER_HEAD
        for piece in range(DECAY_PIECES):
            sel[piece * PIECE_LANES + h, base + piece] = 1.0
            sel[piece * PIECE_LANES + h, LANES + base + DECAY_PIECES + piece] = -1.0
            ones[0, base + DECAY_PIECES + piece] = 1.0
            ones[0, LANES + base + piece] = 1.0
    return jnp.asarray(sel, jnp.bfloat16), jnp.asarray(ones)


def _forget_lanes(a):
    a = jnp.pad(a, ((0, 0), (0, PIECE_LANES - N_MIX_HEADS)))
    a = jnp.tile(a, (1, DECAY_PIECES))
    return jnp.pad(a, ((0, 0), (0, LANES - DECAY_PIECES * PIECE_LANES)))


def _layer_spec(stacked, index, **kwargs):
    return pl.BlockSpec((None,) + stacked.shape[1:], lambda *_: (index, 0, 0), **kwargs)


def _in_proj(h, g, w, slot, layer, forget_params=None, rope_tabs=None, *, seq):
    t = h.shape[0]
    tm = ROW_TILE
    forget = forget_params is not None
    rope = rope_tabs is not None
    row = lambda i: (i, 0)
    fixed = lambda i: (0, 0)
    x_spec = pl.BlockSpec((tm, D_MODEL), row)
    if rope:
        h = h.reshape(t // DIL_RES, DIL_RES, D_MODEL)
        x_spec = pl.BlockSpec((DIL_RUN, DIL_RES, D_MODEL), lambda i: (i, 0, 0))
    tiles_per_seq = seq // tm
    once = dict(pipeline_mode=pl.Buffered(1))
    in_specs = [x_spec, _layer_spec(g, layer), _layer_spec(w, slot, **once)]
    args = [h, g, w]
    out_shape = [jax.ShapeDtypeStruct((t, D_MIX), jnp.bfloat16)] * 3
    out_shape += [jax.ShapeDtypeStruct((t, D_MEMQ), jnp.bfloat16)]
    out_specs = [pl.BlockSpec((tm, D_MIX), row)] * 3
    out_specs += [pl.BlockSpec((tm, D_MEMQ), row)]
    scratch = []
    if forget:
        w_m, w_f, b_f = forget_params
        sel, ones = _decay_selectors()
        tri = jnp.asarray(np.tril(np.ones((tm, tm), np.float32)), jnp.bfloat16)
        in_specs += [_layer_spec(w_m, slot), _layer_spec(w_f, slot), _layer_spec(b_f, slot),
                     pl.BlockSpec((tm, tm), fixed),
                     pl.BlockSpec((LANES, 2 * LANES), fixed),
                     pl.BlockSpec((1, 2 * LANES), fixed)]
        args += [w_m, w_f, b_f, tri, sel, ones]
        out_shape += [jax.ShapeDtypeStruct((t, LANES), jnp.bfloat16)] * 2
        out_specs += [pl.BlockSpec((tm, LANES), row)] * 2
        scratch.append(pltpu.VMEM((8, LANES), jnp.float32))
    if rope:
        tab = lambda i: (i % tiles_per_seq, 0)
        in_specs += [pl.BlockSpec((tm, LANES), tab), pl.BlockSpec((tm, LANES), tab)]
        args += list(rope_tabs)
    return pl.pallas_call(
        functools.partial(_in_proj_kernel, rope=rope, forget=forget,
                          tiles_per_seq=tiles_per_seq),
        grid=(t // tm,),
        in_specs=in_specs, out_specs=out_specs, out_shape=out_shape,
        scratch_shapes=scratch,
        compiler_params=_cparams(1),
        name="in_proj_rope" if rope else "in_proj_fox",
    )(*args)


def _mem_kv_kernel(x_ref, g_ref, w_ref, km_ref, vm_ref):
    xn = _rms(x_ref[...], g_ref[...]).astype(jnp.bfloat16)
    kv = jnp.dot(xn, w_ref[...].astype(jnp.bfloat16), preferred_element_type=jnp.float32)
    km_ref[...] = kv[:, :D_MEMQ].astype(jnp.bfloat16)
    vm_ref[...] = kv[:, D_MEMQ:].astype(jnp.bfloat16)


def _mem_kv(mem2d, g, w, *, mem_tokens):
    rows = mem2d.shape[0]
    depth = w.shape[0]
    per_layer = lambda shape: pl.BlockSpec((None,) + shape, lambda l, b: (l, 0, 0))
    out_spec = pl.BlockSpec((None, mem_tokens, D_MEMQ), lambda l, b: (l, b, 0))
    return pl.pallas_call(
        _mem_kv_kernel,
        grid=(depth, rows // mem_tokens),
        in_specs=[pl.BlockSpec((mem_tokens, D_MODEL), lambda l, b: (b, 0)),
                  per_layer((1, D_MODEL)),
                  per_layer((D_MODEL, 2 * D_MEMQ))],
        out_specs=[out_spec, out_spec],
        out_shape=[jax.ShapeDtypeStruct((depth, rows, D_MEMQ), jnp.bfloat16)] * 2,
        compiler_params=_cparams(2),
        name="mem_kv",
    )(mem2d, g, w)


def _fox_kernel(q_ref, qx_ref, k_ref, kx_ref, v_ref, o_ref, m_scr, l_scr, acc_scr):
    tq, tk = FOX_Q_TILE, FOX_K_TILE
    n_blocks = tq // tk
    pair = pl.program_id(1)
    qi = pl.program_id(2)
    lane = lax.broadcasted_iota(jnp.int32, (1, LANES), 1)
    first = lane < HEAD_DIM
    head_lanes = (first, jnp.logical_not(first))
    q = q_ref[...]
    qx = qx_ref[...]
    zero = jnp.zeros_like(q)
    q_heads = []
    for hh in range(2):
        lo = (2 * pair + hh) * DECAY_LANES_PER_HEAD
        own_decay = (lane >= lo) & (lane < lo + DECAY_LANES_PER_HEAD)
        q_heads.append(jnp.concatenate([jnp.where(head_lanes[hh], q, zero),
                                        jnp.where(own_decay, qx, zero)], axis=1))

    m_scr[...] = jnp.full(m_scr.shape, NEG, jnp.float32)
    l_scr[...] = jnp.zeros(l_scr.shape, jnp.float32)
    acc_scr[...] = jnp.zeros(acc_scr.shape, jnp.float32)
    ones = jnp.ones((tk, LANES), jnp.bfloat16)

    def load_kv(j):
        start = pl.multiple_of(j * tk, tk)
        k = jnp.concatenate([k_ref[pl.ds(start, tk), :], kx_ref[pl.ds(start, tk), :]],
                            axis=1)
        v = jnp.concatenate([v_ref[pl.ds(start, tk), :], ones], axis=1)
        return k, v

    def chain(hh, r, k, v, diagonal):
        rows = slice(r * tk, (r + 1) * tk)
        s = lax.dot_general(q_heads[hh][rows], k, _NT, preferred_element_type=jnp.float32)
        if diagonal:
            row = lax.broadcasted_iota(jnp.int32, (tk, tk), 0)
            col = lax.broadcasted_iota(jnp.int32, (tk, tk), 1)
            s = jnp.where(col <= row, s, NEG)
        m_prev = m_scr[hh, rows, :]
        m_new = jnp.maximum(m_prev, jnp.max(s, axis=-1, keepdims=True))
        alpha = jnp.exp2(m_prev - m_new)
        p = jnp.exp2(s - jnp.concatenate([m_new] * (tk // LANES), axis=1))
        pv = jnp.dot(p.astype(jnp.bfloat16), v, preferred_element_type=jnp.float32)
        l_scr[hh, rows, :] = alpha * l_scr[hh, rows, :] + pv[:, LANES:]
        acc_scr[hh, rows, :] = alpha * acc_scr[hh, rows, :] + pv[:, :LANES]
        m_scr[hh, rows, :] = m_new

    def body(j, carry):
        for half in range(2):
            k, v = load_kv(2 * j + half)
            for r in range(n_blocks):
                for hh in range(2):
                    chain(hh, r, k, v, False)
        return carry

    assert n_blocks % 2 == 0
    n_before = qi * n_blocks
    lax.fori_loop(0, n_before // 2, body, 0)
    for d in range(n_blocks):
        k, v = load_kv(n_before + d)
        for r in range(d, n_blocks):
            for hh in range(2):
                chain(hh, r, k, v, r == d)
    o0 = acc_scr[0] * (1.0 / l_scr[0])
    o1 = acc_scr[1] * (1.0 / l_scr[1])
    o_ref[...] = jnp.where(first, o0, o1).astype(o_ref.dtype)


def _fox_attention(q, qx, k, kx, v, *, batch, seq):
    tq = FOX_Q_TILE
    as_seq = lambda a: a.reshape(batch, seq, a.shape[-1])
    q_spec = pl.BlockSpec((None, tq, LANES), lambda b, p, i: (b, i, p))
    qx_spec = pl.BlockSpec((None, tq, LANES), lambda b, p, i: (b, i, 0))
    kv_spec = pl.BlockSpec((None, seq, LANES), lambda b, p, i: (b, 0, p))
    kx_spec = pl.BlockSpec((None, seq, LANES), lambda b, p, i: (b, 0, 0))
    out = pl.pallas_call(
        _fox_kernel,
        grid=(batch, N_PAIRS, seq // tq),
        in_specs=[q_spec, qx_spec, kv_spec, kx_spec, kv_spec],
        out_specs=q_spec,
        out_shape=jax.ShapeDtypeStruct((batch, seq, D_MIX), jnp.bfloat16),
        scratch_shapes=[pltpu.VMEM((2, tq, LANES), jnp.float32),
                        pltpu.VMEM((2, tq, LANES), jnp.float32),
                        pltpu.VMEM((2, tq, LANES), jnp.float32)],
        compiler_params=_cparams(3),
        name="fox_attention",
    )(as_seq(q), as_seq(qx), as_seq(k), as_seq(kx), as_seq(v))
    return out.reshape(batch * seq, D_MIX)


def _band_bias(n_q, entry_q, entry_k):
    a = np.arange(2 * n_q)[:, None] % n_q
    c = np.arange(2 * n_q)[None, :]
    dist = entry_q(a) - entry_k(c)
    ok = (dist >= 0) & (dist <= DIL_BLOCK)
    planes = [ok, ok & (c >= n_q)]
    return jnp.asarray(np.where(np.stack(planes), 0.0, NEG), jnp.float32)


def _dilated_masks():
    blk = DIL_BLOCK
    b16 = _band_bias(blk, lambda a: a + blk, lambda c: c)
    run4 = 32
    b4 = _band_bias(blk, lambda a: blk + 4 * (a % run4) + a // run4,
                    lambda c: (c // blk) * blk + 4 * ((c % blk) % run4) + (c % blk) // run4)
    run1 = blk // DIL_RES
    b1 = _band_bias(blk, lambda a: blk + DIL_RES * (a % run1) + a // run1,
                    lambda c: (c // blk) * blk + DIL_RES * ((c % blk) % run1) + (c % blk) // run1)
    return b16, b4, b1


def _dilated_kernel(q_ref, kp_ref, kc_ref, vp_ref, vc_ref, b16_ref, b4_ref, b1_ref,
                    o_ref, m_scr, l_scr, acc_scr):
    first = _lane_first_half()
    seq_start = jnp.where(pl.program_id(1) > 0, 0, 1)

    def gather(ref, slices):
        return jnp.concatenate([ref[s, :] for s in slices], axis=0)

    def chain(q_slices, key_parts, bias, first_branch):
        q = gather(q_ref, q_slices)
        zero = jnp.zeros_like(q)
        q2 = jnp.concatenate([jnp.where(first, q, zero), jnp.where(first, zero, q)], axis=0)
        k2 = jnp.concatenate([gather(kr, sl) for kr, _, sl in key_parts], axis=0)
        v2 = jnp.concatenate([gather(vr, sl) for _, vr, sl in key_parts], axis=0)
        n_q, n_k = q.shape[0], k2.shape[0]
        v2 = jnp.concatenate([v2, jnp.ones((n_k, LANES), jnp.bfloat16)], axis=1)
        s = lax.dot_general(q2, k2, _NT, preferred_element_type=jnp.float32) + bias
        m_cur = jnp.max(s, axis=-1, keepdims=True)
        if first_branch:
            m_new = jnp.broadcast_to(m_cur, (2 * n_q, LANES))
        else:
            m_prev = jnp.concatenate([gather(m_scr.at[hh], q_slices) for hh in range(2)],
                                     axis=0)
            m_new = jnp.maximum(m_prev, m_cur)
            alpha = jnp.exp2(m_prev - m_new)
        p = jnp.exp2(s - jnp.concatenate([m_new] * (n_k // LANES), axis=1))
        pv = jnp.dot(p.astype(jnp.bfloat16), v2, preferred_element_type=jnp.float32)
        num, den = pv[:, :LANES], pv[:, LANES:]
        if not first_branch:
            l_prev = jnp.concatenate([gather(l_scr.at[hh], q_slices) for hh in range(2)],
                                     axis=0)
            a_prev = jnp.concatenate([gather(acc_scr.at[hh], q_slices) for hh in range(2)],
                                     axis=0)
            den = alpha * l_prev + den
            num = alpha * a_prev + num
        for hh in range(2):
            off = hh * n_q
            for sl in q_slices:
                n = sl.stop - sl.start
                m_scr[hh, sl, :] = m_new[off:off + n]
                l_scr[hh, sl, :] = den[off:off + n]
                acc_scr[hh, sl, :] = num[off:off + n]
                off += n

    def run(tile, r, lo, n):
        start = tile * ROW_TILE + r * DIL_RUN + lo
        return slice(start, start + n)

    def block(cur, prv, prev_chunk, bias_ref, first_branch):
        if prev_chunk:
            parts = [(kp_ref, vp_ref, prv), (kc_ref, vc_ref, cur)]
            bias = bias_ref[seq_start]
        else:
            parts = [(kc_ref, vc_ref, prv), (kc_ref, vc_ref, cur)]
            bias = bias_ref[0]
        chain(cur, parts, bias, first_branch)

    last = TILES_PER_CHUNK - 1
    for r in range(DIL_RES):
        rows = [run(t, r, 0, DIL_RUN) for t in range(TILES_PER_CHUNK)]
        block(rows, rows, True, b16_ref, True)
    for r4 in range(4):
        for t in range(TILES_PER_CHUNK):
            cur = [run(t, r4 + 4 * u, 0, DIL_RUN) for u in range(4)]
            prv = [run(t - 1 if t else last, r4 + 4 * u, 0, DIL_RUN) for u in range(4)]
            block(cur, prv, t == 0, b4_ref, False)
    quarter = DIL_BLOCK // DIL_RES
    per_tile = DIL_RUN // quarter
    for t in range(TILES_PER_CHUNK):
        for qt in range(per_tile):
            cur = [run(t, r, qt * quarter, quarter) for r in range(DIL_RES)]
            if qt:
                prv = [run(t, r, (qt - 1) * quarter, quarter) for r in range(DIL_RES)]
            else:
                prv = [run(t - 1 if t else last, r, DIL_RUN - quarter, quarter)
                       for r in range(DIL_RES)]
            block(cur, prv, t == 0 and qt == 0, b1_ref, False)

    o0 = acc_scr[0] * (1.0 / l_scr[0])
    o1 = acc_scr[1] * (1.0 / l_scr[1])
    o_ref[...] = jnp.where(first, o0, o1).astype(o_ref.dtype)


def _dilated_attention(q, k, v, *, batch, seq):
    n_chunks = seq // DIL_CHUNK
    flat = lambda a: a.reshape(batch * n_chunks, DIL_CHUNK, D_MIX)
    masks = _dilated_masks()
    cur = lambda b, c, p: (b * n_chunks + c, 0, p)
    prev = lambda b, c, p: (b * n_chunks + jnp.maximum(c - 1, 0), 0, p)
    spec = lambda f: pl.BlockSpec((None, DIL_CHUNK, LANES), f)
    mask_specs = [pl.BlockSpec(m.shape, lambda b, c, p: (0, 0, 0)) for m in masks]
    out = pl.pallas_call(
        _dilated_kernel,
        grid=(batch, n_chunks, N_PAIRS),
        in_specs=[spec(cur), spec(prev), spec(cur), spec(prev), spec(cur)] + mask_specs,
        out_specs=spec(cur),
        out_shape=jax.ShapeDtypeStruct((batch * n_chunks, DIL_CHUNK, D_MIX), jnp.bfloat16),
        scratch_shapes=[pltpu.VMEM((2, DIL_CHUNK, LANES), jnp.float32)] * 3,
        compiler_params=_cparams(3),
        name="dilated_attention",
    )(flat(q), flat(k), flat(k), flat(v), flat(v), *masks)
    return out.reshape(batch * seq, D_MIX)


def _out_proj_kernel(h_ref, mix_ref, qm_ref, km_ref, vm_ref, w_ref, out_ref, *,
                     residue_major):
    first = _lane_first_half()
    qm = qm_ref[...]
    ones = jnp.ones((km_ref.shape[0], LANES), jnp.bfloat16)
    pairs = []
    for p in range(D_MEMQ // LANES):
        cols = slice(p * LANES, (p + 1) * LANES)
        qp = qm[:, cols]
        n = qp.shape[0]
        zero = jnp.zeros_like(qp)
        q2 = jnp.concatenate([jnp.where(first, qp, zero), jnp.where(first, zero, qp)],
                             axis=0)
        v2 = jnp.concatenate([vm_ref[:, cols], ones], axis=1)
        s = lax.dot_general(q2, km_ref[:, cols], _NT, preferred_element_type=jnp.float32)
        e = jnp.exp2(s - jnp.max(s, axis=-1, keepdims=True))
        pv = jnp.dot(e.astype(jnp.bfloat16), v2, preferred_element_type=jnp.float32)
        o = pv[:, :LANES] * (1.0 / pv[:, LANES:])
        pairs.append(jnp.where(first, o[:n], o[n:]).astype(jnp.bfloat16))
    mem_out = jnp.concatenate(pairs, axis=-1)

    y = jnp.dot(mix_ref[...], w_ref[:D_MIX, :].astype(jnp.bfloat16),
                preferred_element_type=jnp.float32)
    y = y + jnp.dot(mem_out, w_ref[D_MIX:, :].astype(jnp.bfloat16),
                    preferred_element_type=jnp.float32)
    if residue_major:
        for r in range(DIL_RES):
            out_ref[:, r, :] = h_ref[:, r, :] + y[r * DIL_RUN:(r + 1) * DIL_RUN, :]
    else:
        out_ref[...] = h_ref[...] + y


def _out_proj(h, mix, qm, km, vm, w_out, layer, *, seq, mem_tokens, residue_major):
    t = h.shape[0]
    tm = ROW_TILE
    tiles_per_seq = seq // tm
    row = lambda i: (i, 0)
    mem_spec = pl.BlockSpec((None, mem_tokens, D_MEMQ),
                            lambda i: (layer, i // tiles_per_seq, 0))
    h_spec = pl.BlockSpec((tm, D_MODEL), row)
    h_shape = (t, D_MODEL)
    if residue_major:
        h_shape = (t // DIL_RES, DIL_RES, D_MODEL)
        h_spec = pl.BlockSpec((DIL_RUN, DIL_RES, D_MODEL), lambda i: (i, 0, 0))
    out = pl.pallas_call(
        functools.partial(_out_proj_kernel, residue_major=residue_major),
        grid=(t // tm,),
        in_specs=[h_spec,
                  pl.BlockSpec((tm, D_MIX), row),
                  pl.BlockSpec((tm, D_MEMQ), row),
                  mem_spec, mem_spec,
                  _layer_spec(w_out, layer, pipeline_mode=pl.Buffered(1))],
        out_specs=h_spec,
        out_shape=jax.ShapeDtypeStruct(h_shape, jnp.float32),
        compiler_params=_cparams(1),
        name="out_proj_residue_major" if residue_major else "out_proj",
    )(h.reshape(h_shape), mix, qm, km, vm, w_out)
    return out.reshape(t, D_MODEL)


def _ffn_kernel(h_ref, g_ref, wup_ref, cw_ref, cb_ref, wdown_ref, gfin_ref, out_ref,
                carry_ref, act_ref, *, tiles_per_seq, final_norm):
    tm = FFN_TILE
    halo = 8
    n_chunks = D_FF // FF_CHUNK
    i = pl.program_id(0)

    @pl.when(i % tiles_per_seq == 0)
    def _():
        carry_ref[...] = jnp.zeros(carry_ref.shape, jnp.float32)

    h = h_ref[...]
    xn = _rms(h, g_ref[...]).astype(jnp.bfloat16)
    row = lax.broadcasted_iota(jnp.int32, (tm, FF_CHUNK), 0)

    def conv(u, cols):
        prev = carry_ref[:, cols]
        back1 = jnp.where(row == 0, prev[halo - 1:halo, :], pltpu.roll(u, 1, 0))
        back2 = jnp.where(row == 0, prev[halo - 2:halo - 1, :],
                          jnp.where(row == 1, prev[halo - 1:halo, :], pltpu.roll(u, 2, 0)))
        carry_ref[:, cols] = u[tm - halo:tm, :]
        w = cw_ref[:, cols]
        c = cb_ref[:, cols] + w[0:1, :] * back2
        c = c + w[1:2, :] * back1
        return c + w[2:3, :] * u

    def val_cols(j):
        return slice(j * FF_CHUNK, (j + 1) * FF_CHUNK)

    def gate_cols(j):
        return slice(D_FF + j * FF_CHUNK, D_FF + (j + 1) * FF_CHUNK)

    def up(j):
        return (jnp.dot(xn, wup_ref[:, val_cols(j)], preferred_element_type=jnp.float32),
                jnp.dot(xn, wup_ref[:, gate_cols(j)], preferred_element_type=jnp.float32))

    u = up(0)
    for j in range(n_chunks):
        u_next = up(j + 1) if j + 1 < n_chunks else None
        val = conv(u[0], val_cols(j))
        gate = conv(u[1], gate_cols(j))
        act_ref[:, val_cols(j)] = (gate * (1.0 / (1.0 + jnp.exp(-gate))) * val
                                   ).astype(jnp.bfloat16)
        u = u_next
    y = h + jnp.dot(act_ref[...], wdown_ref[...], preferred_element_type=jnp.float32)
    if final_norm:
        y = _rms(y, gfin_ref[...])
    out_ref[...] = y


def _ffn(h, g, w_up, conv_w, conv_b, w_down, g_final, layer, *, seq, final_norm):
    t = h.shape[0]
    tm = FFN_TILE
    row = lambda i: (i, 0)
    fixed = lambda i: (0, 0)
    once = dict(pipeline_mode=pl.Buffered(1))
    return pl.pallas_call(
        functools.partial(_ffn_kernel, tiles_per_seq=seq // tm, final_norm=final_norm),
        grid=(t // tm,),
        in_specs=[pl.BlockSpec((tm, D_MODEL), row),
                  _layer_spec(g, layer),
                  _layer_spec(w_up, layer, **once),
                  _layer_spec(conv_w, layer),
                  _layer_spec(conv_b, layer),
                  _layer_spec(w_down, layer, **once),
                  pl.BlockSpec((1, D_MODEL), fixed)],
        out_specs=pl.BlockSpec((tm, D_MODEL), row),
        out_shape=jax.ShapeDtypeStruct((t, D_MODEL), jnp.float32),
        scratch_shapes=[pltpu.VMEM((8, 2 * D_FF), jnp.float32),
                        pltpu.VMEM((tm, D_FF), jnp.bfloat16)],
        compiler_params=_cparams(1),
        name="conv_ffn_final" if final_norm else "conv_ffn",
    )(h, g, w_up, conv_w, conv_b, w_down, g_final)


def _rope_tables(seq):
    def residue_major(tab):
        tab = tab.reshape(seq // ROW_TILE, DIL_RUN, DIL_RES, LANES)
        return tab.transpose(0, 2, 1, 3).reshape(seq, LANES)

    with jax.ensure_compile_time_eval():
        inv = 1.0 / (ROPE_THETA ** (jnp.arange(0, HEAD_DIM, 2, dtype=jnp.float32) / HEAD_DIM))
        ang = jnp.arange(seq, dtype=jnp.float32)[:, None] * inv[None, :]
        cos, sin = jnp.cos(ang), jnp.sin(ang)
        reps = LANES // HEAD_DIM
        cos_t = jnp.tile(jnp.concatenate([cos, cos], axis=-1), (1, reps))
        sin_t = jnp.tile(jnp.concatenate([-sin, sin], axis=-1), (1, reps))
        return residue_major(cos_t), residue_major(sin_t)


def kernel(x, mem, norm_mix, norm_mem, norm_ffn, w_in_fox, b_forget, w_in_dil,
           w_mem_kv, w_out, w_up, conv_w, conv_b, w_down, norm_final):
    batch, seq, _ = x.shape
    assert seq % DIL_CHUNK == 0 and seq % FOX_Q_TILE == 0
    mem_tokens = mem.shape[1]
    depth = norm_mix.shape[0]
    bf = lambda a: a.astype(jnp.bfloat16)
    row3 = lambda a: a.reshape(a.shape[0], 1, a.shape[1])
    h = x.reshape(batch * seq, D_MODEL)
    mem2d = mem.reshape(batch * mem_tokens, D_MODEL)
    rope_tabs = _rope_tables(seq)
    g_mix, g_ffn, g_final = row3(norm_mix), row3(norm_ffn), norm_final.reshape(1, D_MODEL)

    w_up_b, w_down_b = bf(w_up), bf(w_down)
    conv_b3 = row3(conv_b)
    n_fox = w_in_fox.shape[0]
    w_fox = w_in_fox[:, :, :3 * D_MIX]
    tail = w_in_fox[:, :, 3 * D_MIX:]
    forget_params = (
        bf(tail[:, :, N_MIX_HEADS:]),
        bf(_forget_lanes(tail[:, :, :N_MIX_HEADS].reshape(n_fox * D_MODEL, N_MIX_HEADS))
           ).reshape(n_fox, D_MODEL, LANES),
        _forget_lanes(b_forget).reshape(n_fox, 1, LANES))
    km, vm = _mem_kv(mem2d, row3(norm_mem), w_mem_kv, mem_tokens=mem_tokens)

    for layer in range(depth):
        kind, slot = layer % 2, layer // 2
        if kind == 0:
            q, k, v, qm, qx, kx = _in_proj(h, g_mix, w_fox, slot, layer,
                                           forget_params=forget_params, seq=seq)
            mix = _fox_attention(q, qx, k, kx, v, batch=batch, seq=seq)
        else:
            q, k, v, qm = _in_proj(h, g_mix, w_in_dil, slot, layer, rope_tabs=rope_tabs, seq=seq)
            mix = _dilated_attention(q, k, v, batch=batch, seq=seq)
        h = _out_proj(h, mix, qm, km, vm, w_out, layer,
                      seq=seq, mem_tokens=mem_tokens, residue_major=(kind == 1))
        h = _ffn(h, g_ffn, w_up_b, conv_w, conv_b3, w_down_b, g_final, layer,
                 seq=seq, final_norm=(layer == depth - 1))
    return h.reshape(batch, seq, D_MODEL)
```

```python
import functools

import numpy as np

import jax
import jax.numpy as jnp
from jax import lax
from jax.experimental import pallas as pl
from jax.experimental.pallas import tpu as pltpu

D_MODEL = 1024
HEAD_DIM = 64
N_MEM_HEADS = 4
N_MIX_HEADS = 12
D_MIX = N_MIX_HEADS * HEAD_DIM
D_MEMQ = N_MEM_HEADS * HEAD_DIM
D_FF = 2816
CONV_WIDTH = 3
ROPE_THETA = 10000.0
DIL_BLOCK = 128
NORM_EPS = 1e-6
NEG = -1e30
SCALE = HEAD_DIM ** -0.5
LOG2E = 1.4426950408889634

LANES = 128
N_PAIRS = D_MIX // LANES
VMEM_LIMIT = 48 * 1024 * 1024
TAIL_VMEM_LIMIT = 56 * 1024 * 1024

ROW_TILE = 512
FFN_TILE = 1024
FOX_Q_TILE = 2048
FOX_K_TILE = 512
FF_CHUNK = 256
DECAY_PIECES = 3
DECAY_LANES_PER_HEAD = 2 * DECAY_PIECES
PIECE_LANES = 16
DIL_CHUNK = 2048
DIL_RES = 16
DIL_RUN = ROW_TILE // DIL_RES
TILES_PER_CHUNK = DIL_CHUNK // ROW_TILE

_NT = (((1,), (1,)), ((), ()))


def _cparams(n_axes):
    return pltpu.CompilerParams(
        dimension_semantics=("arbitrary",) * n_axes,
        vmem_limit_bytes=VMEM_LIMIT)


def _rms(x, g):
    y = x * lax.rsqrt(jnp.mean(x * x, axis=-1, keepdims=True) + NORM_EPS)
    return y * g


def _lane_first_half():
    lane = lax.broadcasted_iota(jnp.int32, (1, LANES), 1)
    return lane < HEAD_DIM


def _split3(x):
    hi = x.astype(jnp.bfloat16)
    rem = x - hi.astype(jnp.float32)
    mid = rem.astype(jnp.bfloat16)
    lo = (rem - mid.astype(jnp.float32)).astype(jnp.bfloat16)
    return hi, mid, lo


def _in_proj_kernel(*refs, rope, forget, tiles_per_seq):
    x_ref, g_ref, w_ref = refs[:3]
    pos = 3
    if forget:
        wm_ref, wf_ref, bf_ref, tri_ref, sel_ref, ones_ref = refs[pos:pos + 6]; pos += 6
    if rope:
        cos_ref, sin_ref = refs[pos:pos + 2]; pos += 2
    q_ref, k_ref, v_ref, qm_ref = refs[pos:pos + 4]; pos += 4
    if forget:
        qx_ref, kx_ref, carry_ref = refs[pos:pos + 3]

    if rope:
        x = jnp.concatenate([x_ref[:, r, :] for r in range(DIL_RES)], axis=0)
    else:
        x = x_ref[...]
    xn = _rms(x, g_ref[...]).astype(jnp.bfloat16)
    q_scale = SCALE * LOG2E

    def proj(index):
        width = D_MIX if index < 3 else D_MEMQ
        w = w_ref[:, index * D_MIX:index * D_MIX + width].astype(jnp.bfloat16)
        return jnp.dot(xn, w, preferred_element_type=jnp.float32)

    def rotate(t):
        lane = lax.broadcasted_iota(jnp.int32, (1, LANES), 1)
        first = (lane % HEAD_DIM) < (HEAD_DIM // 2)
        partner = jnp.where(first, pltpu.roll(t, LANES - HEAD_DIM // 2, 1),
                            pltpu.roll(t, HEAD_DIM // 2, 1))
        return t * cos_ref[...] + partner * sin_ref[...]

    if rope:
        q = proj(0)
        for p in range(N_PAIRS):
            cols = slice(p * LANES, (p + 1) * LANES)
            q_ref[:, cols] = (rotate(q[:, cols]) * q_scale).astype(jnp.bfloat16)
        k = proj(1)
        for p in range(N_PAIRS):
            cols = slice(p * LANES, (p + 1) * LANES)
            k_ref[:, cols] = rotate(k[:, cols]).astype(jnp.bfloat16)
        v_ref[...] = proj(2).astype(jnp.bfloat16)
        qm_ref[...] = (proj(3) * q_scale).astype(jnp.bfloat16)
        return

    @pl.when(pl.program_id(0) % tiles_per_seq == 0)
    def _():
        carry_ref[...] = jnp.zeros(carry_ref.shape, jnp.float32)

    f = jnp.dot(xn, wf_ref[...], preferred_element_type=jnp.float32) + bf_ref[...]
    q_ref[...] = (proj(0) * q_scale).astype(jnp.bfloat16)
    log_f = jnp.minimum(f, 0.0) - jnp.log1p(jnp.exp(-jnp.abs(f)))
    lane = lax.broadcasted_iota(jnp.int32, (1, LANES), 1)
    hi, mid, lo = _split3(log_f)
    packed = jnp.where(lane < PIECE_LANES, hi, jnp.where(lane < 2 * PIECE_LANES, mid, lo))
    c3 = carry_ref[0:1, :] + jnp.dot(tri_ref[...], packed, preferred_element_type=jnp.float32)
    carry_ref[0:1, :] = c3[c3.shape[0] - 1:, :]
    c = (c3 + pltpu.roll(c3, LANES - PIECE_LANES, 1)) + pltpu.roll(c3, LANES - 2 * PIECE_LANES, 1)
    k_ref[...] = proj(1).astype(jnp.bfloat16)
    hi, mid, lo = _split3(c * LOG2E)
    packed = jnp.where(lane < PIECE_LANES, hi,
                       jnp.where(lane < 2 * PIECE_LANES, pltpu.roll(mid, PIECE_LANES, 1),
                                 pltpu.roll(lo, 2 * PIECE_LANES, 1)))
    x_decay = ones_ref[...] + jnp.dot(packed, sel_ref[...], preferred_element_type=jnp.float32)
    qx_ref[...] = x_decay[:, :LANES].astype(jnp.bfloat16)
    kx_ref[...] = x_decay[:, LANES:].astype(jnp.bfloat16)
    v_ref[...] = proj(2).astype(jnp.bfloat16)
    qm_ref[...] = (jnp.dot(xn, wm_ref[...], preferred_element_type=jnp.float32) * q_scale
                   ).astype(jnp.bfloat16)


def _decay_selectors():
    sel = np.zeros((LANES, 2 * LANES), np.float32)
    ones = np.zeros((1, 2 * LANES), np.float32)
    for h in range(N_MIX_HEADS):
        base = h * DECAY_LANES_PER_HEAD
        for piece in range(DECAY_PIECES):
            sel[piece * PIECE_LANES + h, base + piece] = 1.0
            sel[piece * PIECE_LANES + h, LANES + base + DECAY_PIECES + piece] = -1.0
            ones[0, base + DECAY_PIECES + piece] = 1.0
            ones[0, LANES + base + piece] = 1.0
    return jnp.asarray(sel, jnp.bfloat16), jnp.asarray(ones)


def _forget_lanes(a):
    a = jnp.pad(a, ((0, 0), (0, PIECE_LANES - N_MIX_HEADS)))
    a = jnp.tile(a, (1, DECAY_PIECES))
    return jnp.pad(a, ((0, 0), (0, LANES - DECAY_PIECES * PIECE_LANES)))


def _layer_spec(stacked, index, **kwargs):
    return pl.BlockSpec((None,) + stacked.shape[1:], lambda *_: (index, 0, 0), **kwargs)


def _in_proj(h, g, w, slot, layer, forget_params=None, rope_tabs=None, *, seq):
    t = h.shape[0]
    tm = ROW_TILE
    forget = forget_params is not None
    rope = rope_tabs is not None
    row = lambda i: (i, 0)
    fixed = lambda i: (0, 0)
    x_spec = pl.BlockSpec((tm, D_MODEL), row)
    if rope:
        h = h.reshape(t // DIL_RES, DIL_RES, D_MODEL)
        x_spec = pl.BlockSpec((DIL_RUN, DIL_RES, D_MODEL), lambda i: (i, 0, 0))
    tiles_per_seq = seq // tm
    once = dict(pipeline_mode=pl.Buffered(1))
    in_specs = [x_spec, _layer_spec(g, layer), _layer_spec(w, slot, **once)]
    args = [h, g, w]
    out_shape = [jax.ShapeDtypeStruct((t, D_MIX), jnp.bfloat16)] * 3
    out_shape += [jax.ShapeDtypeStruct((t, D_MEMQ), jnp.bfloat16)]
    out_specs = [pl.BlockSpec((tm, D_MIX), row)] * 3
    out_specs += [pl.BlockSpec((tm, D_MEMQ), row)]
    scratch = []
    if forget:
        w_m, w_f, b_f = forget_params
        sel, ones = _decay_selectors()
        tri = jnp.asarray(np.tril(np.ones((tm, tm), np.float32)), jnp.bfloat16)
        in_specs += [_layer_spec(w_m, slot), _layer_spec(w_f, slot), _layer_spec(b_f, slot),
                     pl.BlockSpec((tm, tm), fixed),
                     pl.BlockSpec((LANES, 2 * LANES), fixed),
                     pl.BlockSpec((1, 2 * LANES), fixed)]
        args += [w_m, w_f, b_f, tri, sel, ones]
        out_shape += [jax.ShapeDtypeStruct((t, LANES), jnp.bfloat16)] * 2
        out_specs += [pl.BlockSpec((tm, LANES), row)] * 2
        scratch.append(pltpu.VMEM((8, LANES), jnp.float32))
    if rope:
        tab = lambda i: (i % tiles_per_seq, 0)
        in_specs += [pl.BlockSpec((tm, LANES), tab), pl.BlockSpec((tm, LANES), tab)]
        args += list(rope_tabs)
    return pl.pallas_call(
        functools.partial(_in_proj_kernel, rope=rope, forget=forget,
                          tiles_per_seq=tiles_per_seq),
        grid=(t // tm,),
        in_specs=in_specs, out_specs=out_specs, out_shape=out_shape,
        scratch_shapes=scratch,
        compiler_params=_cparams(1),
        name="in_proj_rope" if rope else "in_proj_fox",
    )(*args)


def _mem_kv_kernel(x_ref, g_ref, w_ref, km_ref, vm_ref):
    xn = _rms(x_ref[...], g_ref[...]).astype(jnp.bfloat16)
    kv = jnp.dot(xn, w_ref[...].astype(jnp.bfloat16), preferred_element_type=jnp.float32)
    km_ref[...] = kv[:, :D_MEMQ].astype(jnp.bfloat16)
    vm_ref[...] = kv[:, D_MEMQ:].astype(jnp.bfloat16)


def _mem_kv(mem2d, g, w, *, mem_tokens):
    rows = mem2d.shape[0]
    depth = w.shape[0]
    per_layer = lambda shape: pl.BlockSpec((None,) + shape, lambda l, b: (l, 0, 0))
    out_spec = pl.BlockSpec((None, mem_tokens, D_MEMQ), lambda l, b: (l, b, 0))
    return pl.pallas_call(
        _mem_kv_kernel,
        grid=(depth, rows // mem_tokens),
        in_specs=[pl.BlockSpec((mem_tokens, D_MODEL), lambda l, b: (b, 0)),
                  per_layer((1, D_MODEL)),
                  per_layer((D_MODEL, 2 * D_MEMQ))],
        out_specs=[out_spec, out_spec],
        out_shape=[jax.ShapeDtypeStruct((depth, rows, D_MEMQ), jnp.bfloat16)] * 2,
        compiler_params=_cparams(2),
        name="mem_kv",
    )(mem2d, g, w)


def _fox_kernel(q_ref, qx_ref, k_ref, kx_ref, v_ref, o_ref, m_scr, l_scr, acc_scr):
    tq, tk = FOX_Q_TILE, FOX_K_TILE
    n_blocks = tq // tk
    pair = pl.program_id(1)
    qi = pl.program_id(2)
    lane = lax.broadcasted_iota(jnp.int32, (1, LANES), 1)
    first = lane < HEAD_DIM
    head_lanes = (first, jnp.logical_not(first))
    q = q_ref[...]
    qx = qx_ref[...]
    zero = jnp.zeros_like(q)
    q_heads = []
    for hh in range(2):
        lo = (2 * pair + hh) * DECAY_LANES_PER_HEAD
        own_decay = (lane >= lo) & (lane < lo + DECAY_LANES_PER_HEAD)
        q_heads.append(jnp.concatenate([jnp.where(head_lanes[hh], q, zero),
                                        jnp.where(own_decay, qx, zero)], axis=1))

    m_scr[...] = jnp.full(m_scr.shape, NEG, jnp.float32)
    l_scr[...] = jnp.zeros(l_scr.shape, jnp.float32)
    acc_scr[...] = jnp.zeros(acc_scr.shape, jnp.float32)
    ones = jnp.ones((tk, LANES), jnp.bfloat16)

    def load_kv(j):
        start = pl.multiple_of(j * tk, tk)
        k = jnp.concatenate([k_ref[pl.ds(start, tk), :], kx_ref[pl.ds(start, tk), :]],
                            axis=1)
        v = jnp.concatenate([v_ref[pl.ds(start, tk), :], ones], axis=1)
        return k, v

    def chain(hh, r, k, v, diagonal):
        rows = slice(r * tk, (r + 1) * tk)
        s = lax.dot_general(q_heads[hh][rows], k, _NT, preferred_element_type=jnp.float32)
        if diagonal:
            row = lax.broadcasted_iota(jnp.int32, (tk, tk), 0)
            col = lax.broadcasted_iota(jnp.int32, (tk, tk), 1)
            s = jnp.where(col <= row, s, NEG)
        m_prev = m_scr[hh, rows, :]
        m_new = jnp.maximum(m_prev, jnp.max(s, axis=-1, keepdims=True))
        alpha = jnp.exp2(m_prev - m_new)
        p = jnp.exp2(s - jnp.concatenate([m_new] * (tk // LANES), axis=1))
        pv = jnp.dot(p.astype(jnp.bfloat16), v, preferred_element_type=jnp.float32)
        l_scr[hh, rows, :] = alpha * l_scr[hh, rows, :] + pv[:, LANES:]
        acc_scr[hh, rows, :] = alpha * acc_scr[hh, rows, :] + pv[:, :LANES]
        m_scr[hh, rows, :] = m_new

    def body(j, carry):
        for half in range(2):
            k, v = load_kv(2 * j + half)
            for r in range(n_blocks):
                for hh in range(2):
                    chain(hh, r, k, v, False)
        return carry

    assert n_blocks % 2 == 0
    n_before = qi * n_blocks
    lax.fori_loop(0, n_before // 2, body, 0)
    for d in range(n_blocks):
        k, v = load_kv(n_before + d)
        for r in range(d, n_blocks):
            for hh in range(2):
                chain(hh, r, k, v, r == d)
    o0 = acc_scr[0] * (1.0 / l_scr[0])
    o1 = acc_scr[1] * (1.0 / l_scr[1])
    o_ref[...] = jnp.where(first, o0, o1).astype(o_ref.dtype)


def _fox_attention(q, qx, k, kx, v, *, batch, seq):
    tq = FOX_Q_TILE
    as_seq = lambda a: a.reshape(batch, seq, a.shape[-1])
    q_spec = pl.BlockSpec((None, tq, LANES), lambda b, p, i: (b, i, p))
    qx_spec = pl.BlockSpec((None, tq, LANES), lambda b, p, i: (b, i, 0))
    kv_spec = pl.BlockSpec((None, seq, LANES), lambda b, p, i: (b, 0, p))
    kx_spec = pl.BlockSpec((None, seq, LANES), lambda b, p, i: (b, 0, 0))
    out = pl.pallas_call(
        _fox_kernel,
        grid=(batch, N_PAIRS, seq // tq),
        in_specs=[q_spec, qx_spec, kv_spec, kx_spec, kv_spec],
        out_specs=q_spec,
        out_shape=jax.ShapeDtypeStruct((batch, seq, D_MIX), jnp.bfloat16),
        scratch_shapes=[pltpu.VMEM((2, tq, LANES), jnp.float32),
                        pltpu.VMEM((2, tq, LANES), jnp.float32),
                        pltpu.VMEM((2, tq, LANES), jnp.float32)],
        compiler_params=_cparams(3),
        name="fox_attention",
    )(as_seq(q), as_seq(qx), as_seq(k), as_seq(kx), as_seq(v))
    return out.reshape(batch * seq, D_MIX)


def _band_bias(n_q, entry_q, entry_k):
    a = np.arange(2 * n_q)[:, None] % n_q
    c = np.arange(2 * n_q)[None, :]
    dist = entry_q(a) - entry_k(c)
    ok = (dist >= 0) & (dist <= DIL_BLOCK)
    planes = [ok, ok & (c >= n_q)]
    return jnp.asarray(np.where(np.stack(planes), 0.0, NEG), jnp.float32)


def _dilated_masks():
    blk = DIL_BLOCK
    b16 = _band_bias(blk, lambda a: a + blk, lambda c: c)
    run4 = 32
    b4 = _band_bias(blk, lambda a: blk + 4 * (a % run4) + a // run4,
                    lambda c: (c // blk) * blk + 4 * ((c % blk) % run4) + (c % blk) // run4)
    run1 = blk // DIL_RES
    b1 = _band_bias(blk, lambda a: blk + DIL_RES * (a % run1) + a // run1,
                    lambda c: (c // blk) * blk + DIL_RES * ((c % blk) % run1) + (c % blk) // run1)
    return b16, b4, b1


def _dilated_kernel(q_ref, kp_ref, kc_ref, vp_ref, vc_ref, b16_ref, b4_ref, b1_ref,
                    o_ref, m_scr, l_scr, acc_scr):
    first = _lane_first_half()
    seq_start = jnp.where(pl.program_id(1) > 0, 0, 1)

    def gather(ref, slices):
        return jnp.concatenate([ref[s, :] for s in slices], axis=0)

    def chain(q_slices, key_parts, bias, first_branch):
        q = gather(q_ref, q_slices)
        zero = jnp.zeros_like(q)
        q2 = jnp.concatenate([jnp.where(first, q, zero), jnp.where(first, zero, q)], axis=0)
        k2 = jnp.concatenate([gather(kr, sl) for kr, _, sl in key_parts], axis=0)
        v2 = jnp.concatenate([gather(vr, sl) for _, vr, sl in key_parts], axis=0)
        n_q, n_k = q.shape[0], k2.shape[0]
        v2 = jnp.concatenate([v2, jnp.ones((n_k, LANES), jnp.bfloat16)], axis=1)
        s = lax.dot_general(q2, k2, _NT, preferred_element_type=jnp.float32) + bias
        m_cur = jnp.max(s, axis=-1, keepdims=True)
        if first_branch:
            m_new = jnp.broadcast_to(m_cur, (2 * n_q, LANES))
        else:
            m_prev = jnp.concatenate([gather(m_scr.at[hh], q_slices) for hh in range(2)],
                                     axis=0)
            m_new = jnp.maximum(m_prev, m_cur)
            alpha = jnp.exp2(m_prev - m_new)
        p = jnp.exp2(s - jnp.concatenate([m_new] * (n_k // LANES), axis=1))
        pv = jnp.dot(p.astype(jnp.bfloat16), v2, preferred_element_type=jnp.float32)
        num, den = pv[:, :LANES], pv[:, LANES:]
        if not first_branch:
            l_prev = jnp.concatenate([gather(l_scr.at[hh], q_slices) for hh in range(2)],
                                     axis=0)
            a_prev = jnp.concatenate([gather(acc_scr.at[hh], q_slices) for hh in range(2)],
                                     axis=0)
            den = alpha * l_prev + den
            num = alpha * a_prev + num
        for hh in range(2):
            off = hh * n_q
            for sl in q_slices:
                n = sl.stop - sl.start
                m_scr[hh, sl, :] = m_new[off:off + n]
                l_scr[hh, sl, :] = den[off:off + n]
                acc_scr[hh, sl, :] = num[off:off + n]
                off += n

    def run(tile, r, lo, n):
        start = tile * ROW_TILE + r * DIL_RUN + lo
        return slice(start, start + n)

    def block(cur, prv, prev_chunk, bias_ref, first_branch):
        if prev_chunk:
            parts = [(kp_ref, vp_ref, prv), (kc_ref, vc_ref, cur)]
            bias = bias_ref[seq_start]
        else:
            parts = [(kc_ref, vc_ref, prv), (kc_ref, vc_ref, cur)]
            bias = bias_ref[0]
        chain(cur, parts, bias, first_branch)

    last = TILES_PER_CHUNK - 1
    for r in range(DIL_RES):
        rows = [run(t, r, 0, DIL_RUN) for t in range(TILES_PER_CHUNK)]
        block(rows, rows, True, b16_ref, True)
    for r4 in range(4):
        for t in range(TILES_PER_CHUNK):
            cur = [run(t, r4 + 4 * u, 0, DIL_RUN) for u in range(4)]
            prv = [run(t - 1 if t else last, r4 + 4 * u, 0, DIL_RUN) for u in range(4)]
            block(cur, prv, t == 0, b4_ref, False)
    quarter = DIL_BLOCK // DIL_RES
    per_tile = DIL_RUN // quarter
    for t in range(TILES_PER_CHUNK):
        for qt in range(per_tile):
            cur = [run(t, r, qt * quarter, quarter) for r in range(DIL_RES)]
            if qt:
                prv = [run(t, r, (qt - 1) * quarter, quarter) for r in range(DIL_RES)]
            else:
                prv = [run(t - 1 if t else last, r, DIL_RUN - quarter, quarter)
                       for r in range(DIL_RES)]
            block(cur, prv, t == 0 and qt == 0, b1_ref, False)

    o0 = acc_scr[0] * (1.0 / l_scr[0])
    o1 = acc_scr[1] * (1.0 / l_scr[1])
    o_ref[...] = jnp.where(first, o0, o1).astype(o_ref.dtype)


def _dilated_attention(q, k, v, *, batch, seq):
    n_chunks = seq // DIL_CHUNK
    flat = lambda a: a.reshape(batch * n_chunks, DIL_CHUNK, D_MIX)
    masks = _dilated_masks()
    cur = lambda b, c, p: (b * n_chunks + c, 0, p)
    prev = lambda b, c, p: (b * n_chunks + jnp.maximum(c - 1, 0), 0, p)
    spec = lambda f: pl.BlockSpec((None, DIL_CHUNK, LANES), f)
    mask_specs = [pl.BlockSpec(m.shape, lambda b, c, p: (0, 0, 0)) for m in masks]
    out = pl.pallas_call(
        _dilated_kernel,
        grid=(batch, n_chunks, N_PAIRS),
        in_specs=[spec(cur), spec(prev), spec(cur), spec(prev), spec(cur)] + mask_specs,
        out_specs=spec(cur),
        out_shape=jax.ShapeDtypeStruct((batch * n_chunks, DIL_CHUNK, D_MIX), jnp.bfloat16),
        scratch_shapes=[pltpu.VMEM((2, DIL_CHUNK, LANES), jnp.float32)] * 3,
        compiler_params=_cparams(3),
        name="dilated_attention",
    )(flat(q), flat(k), flat(k), flat(v), flat(v), *masks)
    return out.reshape(batch * seq, D_MIX)


def _mem_attention(qm_ref, km_ref, vm_ref):
    first = _lane_first_half()
    qm = qm_ref[...]
    ones = jnp.ones((km_ref.shape[0], LANES), jnp.bfloat16)
    pairs = []
    for p in range(D_MEMQ // LANES):
        cols = slice(p * LANES, (p + 1) * LANES)
        qp = qm[:, cols]
        n = qp.shape[0]
        zero = jnp.zeros_like(qp)
        q2 = jnp.concatenate([jnp.where(first, qp, zero), jnp.where(first, zero, qp)],
                             axis=0)
        v2 = jnp.concatenate([vm_ref[:, cols], ones], axis=1)
        s = lax.dot_general(q2, km_ref[:, cols], _NT, preferred_element_type=jnp.float32)
        e = jnp.exp2(s - jnp.max(s, axis=-1, keepdims=True))
        pv = jnp.dot(e.astype(jnp.bfloat16), v2, preferred_element_type=jnp.float32)
        o = pv[:, :LANES] * (1.0 / pv[:, LANES:])
        pairs.append(jnp.where(first, o[:n], o[n:]).astype(jnp.bfloat16))
    return jnp.concatenate(pairs, axis=-1)


def _layer_tail_kernel(*refs, tiles_per_seq, final_norm, residue_major):
    (h_ref, mix_ref, qm_ref, km_ref, vm_ref, wo_ref, g_ref, wup_ref, cw_ref, cb_ref,
     wdown_ref, gfin_ref, out_ref, carry_ref, act_ref) = refs[:15]
    tm = FFN_TILE
    halo = 8
    n_chunks = D_FF // FF_CHUNK
    i = pl.program_id(0)

    @pl.when(i % tiles_per_seq == 0)
    def _():
        carry_ref[...] = jnp.zeros(carry_ref.shape, jnp.float32)

    mem_out = _mem_attention(qm_ref, km_ref, vm_ref)
    y = jnp.dot(mix_ref[...], wo_ref[:D_MIX, :].astype(jnp.bfloat16),
                preferred_element_type=jnp.float32)
    y = y + jnp.dot(mem_out, wo_ref[D_MIX:, :].astype(jnp.bfloat16),
                    preferred_element_type=jnp.float32)
    if residue_major:
        stage_ref = refs[15]
        for r in range(DIL_RES):
            piece = jnp.concatenate(
                [y[t * ROW_TILE + r * DIL_RUN:t * ROW_TILE + (r + 1) * DIL_RUN]
                 for t in range(tm // ROW_TILE)], axis=0)
            stage_ref[:, r, :] = h_ref[:, r, :] + piece
        h = stage_ref[...].reshape(tm, D_MODEL)
    else:
        h = h_ref[...] + y

    xn = _rms(h, g_ref[...]).astype(jnp.bfloat16)
    row = lax.broadcasted_iota(jnp.int32, (tm, FF_CHUNK), 0)

    def conv(u, cols):
        prev = carry_ref[:, cols]
        back1 = jnp.where(row == 0, prev[halo - 1:halo, :], pltpu.roll(u, 1, 0))
        back2 = jnp.where(row == 0, prev[halo - 2:halo - 1, :],
                          jnp.where(row == 1, prev[halo - 1:halo, :], pltpu.roll(u, 2, 0)))
        carry_ref[:, cols] = u[tm - halo:tm, :]
        w = cw_ref[:, cols]
        c = cb_ref[:, cols] + w[0:1, :] * back2
        c = c + w[1:2, :] * back1
        return c + w[2:3, :] * u

    def val_cols(j):
        return slice(j * FF_CHUNK, (j + 1) * FF_CHUNK)

    def gate_cols(j):
        return slice(D_FF + j * FF_CHUNK, D_FF + (j + 1) * FF_CHUNK)

    def up(j):
        return (jnp.dot(xn, wup_ref[:, val_cols(j)], preferred_element_type=jnp.float32),
                jnp.dot(xn, wup_ref[:, gate_cols(j)], preferred_element_type=jnp.float32))

    u = up(0)
    for j in range(n_chunks):
        u_next = up(j + 1) if j + 1 < n_chunks else None
        val = conv(u[0], val_cols(j))
        gate = conv(u[1], gate_cols(j))
        act_ref[:, val_cols(j)] = (gate * (1.0 / (1.0 + jnp.exp(-gate))) * val
                                   ).astype(jnp.bfloat16)
        u = u_next
    y = h + jnp.dot(act_ref[...], wdown_ref[...], preferred_element_type=jnp.float32)
    if final_norm:
        y = _rms(y, gfin_ref[...])
    out_ref[...] = y


def _layer_tail(h, mix, qm, km, vm, w_out, g, w_up, conv_w, conv_b, w_down, g_final, layer, *,
                seq, mem_tokens, final_norm, residue_major):
    t = h.shape[0]
    tm = FFN_TILE
    tiles_per_seq = seq // tm
    row = lambda i: (i, 0)
    once = dict(pipeline_mode=pl.Buffered(1))
    mem_spec = pl.BlockSpec((None, mem_tokens, D_MEMQ),
                            lambda i: (layer, i // tiles_per_seq, 0))
    h_spec = pl.BlockSpec((tm, D_MODEL), row)
    scratch = [pltpu.VMEM((8, 2 * D_FF), jnp.float32),
               pltpu.VMEM((tm, D_FF), jnp.bfloat16)]
    if residue_major:
        h = h.reshape(t // DIL_RES, DIL_RES, D_MODEL)
        h_spec = pl.BlockSpec((tm // DIL_RES, DIL_RES, D_MODEL), lambda i: (i, 0, 0))
        scratch.append(pltpu.VMEM((tm // DIL_RES, DIL_RES, D_MODEL), jnp.float32))
    return pl.pallas_call(
        functools.partial(_layer_tail_kernel, tiles_per_seq=tiles_per_seq,
                          final_norm=final_norm, residue_major=residue_major),
        grid=(t // tm,),
        in_specs=[h_spec,
                  pl.BlockSpec((tm, D_MIX), row),
                  pl.BlockSpec((tm, D_MEMQ), row),
                  mem_spec, mem_spec,
                  _layer_spec(w_out, layer, **once),
                  _layer_spec(g, layer),
                  _layer_spec(w_up, layer, **once),
                  _layer_spec(conv_w, layer),
                  _layer_spec(conv_b, layer),
                  _layer_spec(w_down, layer, **once),
                  pl.BlockSpec((1, D_MODEL), lambda i: (0, 0))],
        out_specs=pl.BlockSpec((tm, D_MODEL), row),
        out_shape=jax.ShapeDtypeStruct((t, D_MODEL), jnp.float32),
        scratch_shapes=scratch,
        compiler_params=pltpu.CompilerParams(dimension_semantics=("arbitrary",),
                                             vmem_limit_bytes=TAIL_VMEM_LIMIT),
        name="layer_tail_residue_major" if residue_major else "layer_tail",
    )(h, mix, qm, km, vm, w_out, g, w_up, conv_w, conv_b, w_down, g_final)


def _rope_tables(seq):
    def residue_major(tab):
        tab = tab.reshape(seq // ROW_TILE, DIL_RUN, DIL_RES, LANES)
        return tab.transpose(0, 2, 1, 3).reshape(seq, LANES)

    with jax.ensure_compile_time_eval():
        inv = 1.0 / (ROPE_THETA ** (jnp.arange(0, HEAD_DIM, 2, dtype=jnp.float32) / HEAD_DIM))
        ang = jnp.arange(seq, dtype=jnp.float32)[:, None] * inv[None, :]
        cos, sin = jnp.cos(ang), jnp.sin(ang)
        reps = LANES // HEAD_DIM
        cos_t = jnp.tile(jnp.concatenate([cos, cos], axis=-1), (1, reps))
        sin_t = jnp.tile(jnp.concatenate([-sin, sin], axis=-1), (1, reps))
        return residue_major(cos_t), residue_major(sin_t)


def kernel(x, mem, norm_mix, norm_mem, norm_ffn, w_in_fox, b_forget, w_in_dil,
           w_mem_kv, w_out, w_up, conv_w, conv_b, w_down, norm_final):
    batch, seq, _ = x.shape
    assert seq % DIL_CHUNK == 0 and seq % FOX_Q_TILE == 0
    mem_tokens = mem.shape[1]
    depth = norm_mix.shape[0]
    bf = lambda a: a.astype(jnp.bfloat16)
    row3 = lambda a: a.reshape(a.shape[0], 1, a.shape[1])
    h = x.reshape(batch * seq, D_MODEL)
    mem2d = mem.reshape(batch * mem_tokens, D_MODEL)
    rope_tabs = _rope_tables(seq)
    g_mix, g_ffn, g_final = row3(norm_mix), row3(norm_ffn), norm_final.reshape(1, D_MODEL)

    w_up_b, w_down_b = bf(w_up), bf(w_down)
    conv_b3 = row3(conv_b)
    n_fox = w_in_fox.shape[0]
    w_fox = w_in_fox[:, :, :3 * D_MIX]
    tail = w_in_fox[:, :, 3 * D_MIX:]
    forget_params = (
        bf(tail[:, :, N_MIX_HEADS:]),
        bf(_forget_lanes(tail[:, :, :N_MIX_HEADS].reshape(n_fox * D_MODEL, N_MIX_HEADS))
           ).reshape(n_fox, D_MODEL, LANES),
        _forget_lanes(b_forget).reshape(n_fox, 1, LANES))
    km, vm = _mem_kv(mem2d, row3(norm_mem), w_mem_kv, mem_tokens=mem_tokens)

    for layer in range(depth):
        kind, slot = layer % 2, layer // 2
        if kind == 0:
            q, k, v, qm, qx, kx = _in_proj(h, g_mix, w_fox, slot, layer,
                                           forget_params=forget_params, seq=seq)
            mix = _fox_attention(q, qx, k, kx, v, batch=batch, seq=seq)
        else:
            q, k, v, qm = _in_proj(h, g_mix, w_in_dil, slot, layer, rope_tabs=rope_tabs, seq=seq)
            mix = _dilated_attention(q, k, v, batch=batch, seq=seq)
        h = _layer_tail(h, mix, qm, km, vm, w_out, g_ffn, w_up_b, conv_w, conv_b3, w_down_b,
                        g_final, layer, seq=seq, mem_tokens=mem_tokens,
                        final_norm=(layer == depth - 1), residue_major=(kind == 1))
    return h.reshape(batch, seq, D_MODEL)
```

```python
import functools

import numpy as np

import jax
import jax.numpy as jnp
from jax import lax
from jax.experimental import pallas as pl
from jax.experimental.pallas import tpu as pltpu

D_MODEL = 1024
HEAD_DIM = 64
N_MEM_HEADS = 4
N_MIX_HEADS = 12
D_MIX = N_MIX_HEADS * HEAD_DIM
D_MEMQ = N_MEM_HEADS * HEAD_DIM
D_FF = 2816
CONV_WIDTH = 3
ROPE_THETA = 10000.0
DIL_BLOCK = 128
NORM_EPS = 1e-6
NEG = -1e30
SCALE = HEAD_DIM ** -0.5
LOG2E = 1.4426950408889634

LANES = 128
N_PAIRS = D_MIX // LANES
VMEM_LIMIT = 48 * 1024 * 1024
TAIL_VMEM_LIMIT = 56 * 1024 * 1024

ROW_TILE = 512
FFN_TILE = 1024
FOX_Q_TILE = 2048
FOX_K_TILE = 512
FF_CHUNK = 256
DECAY_PIECES = 3
DECAY_LANES_PER_HEAD = 2 * DECAY_PIECES
PIECE_LANES = 16
DIL_CHUNK = 2048
DIL_RES = 16
DIL_RUN = ROW_TILE // DIL_RES
TILES_PER_CHUNK = DIL_CHUNK // ROW_TILE

_NT = (((1,), (1,)), ((), ()))


def _cparams(n_axes):
    return pltpu.CompilerParams(
        dimension_semantics=("arbitrary",) * n_axes,
        vmem_limit_bytes=VMEM_LIMIT)


def _rms(x, g):
    y = x * lax.rsqrt(jnp.mean(x * x, axis=-1, keepdims=True) + NORM_EPS)
    return y * g


def _lane_first_half():
    lane = lax.broadcasted_iota(jnp.int32, (1, LANES), 1)
    return lane < HEAD_DIM


def _split3(x):
    hi = x.astype(jnp.bfloat16)
    rem = x - hi.astype(jnp.float32)
    mid = rem.astype(jnp.bfloat16)
    lo = (rem - mid.astype(jnp.float32)).astype(jnp.bfloat16)
    return hi, mid, lo


def _in_proj_kernel(*refs, rope, forget, tiles_per_seq):
    x_ref, g_ref, w_ref = refs[:3]
    pos = 3
    if forget:
        wm_ref, wf_ref, bf_ref, tri_ref, sel_ref, ones_ref = refs[pos:pos + 6]; pos += 6
    if rope:
        cos_ref, sin_ref = refs[pos:pos + 2]; pos += 2
    q_ref, k_ref, v_ref, qm_ref = refs[pos:pos + 4]; pos += 4
    if forget:
        qx_ref, kx_ref, carry_ref = refs[pos:pos + 3]

    if rope:
        x = jnp.concatenate([x_ref[:, r, :] for r in range(DIL_RES)], axis=0)
    else:
        x = x_ref[...]
    xn = _rms(x, g_ref[...]).astype(jnp.bfloat16)
    q_scale = SCALE * LOG2E

    def proj(index):
        width = D_MIX if index < 3 else D_MEMQ
        w = w_ref[:, index * D_MIX:index * D_MIX + width].astype(jnp.bfloat16)
        return jnp.dot(xn, w, preferred_element_type=jnp.float32)

    def rotate(t):
        lane = lax.broadcasted_iota(jnp.int32, (1, LANES), 1)
        first = (lane % HEAD_DIM) < (HEAD_DIM // 2)
        partner = jnp.where(first, pltpu.roll(t, LANES - HEAD_DIM // 2, 1),
                            pltpu.roll(t, HEAD_DIM // 2, 1))
        return t * cos_ref[...] + partner * sin_ref[...]

    if rope:
        q = proj(0)
        for p in range(N_PAIRS):
            cols = slice(p * LANES, (p + 1) * LANES)
            q_ref[:, cols] = (rotate(q[:, cols]) * q_scale).astype(jnp.bfloat16)
        k = proj(1)
        for p in range(N_PAIRS):
            cols = slice(p * LANES, (p + 1) * LANES)
            k_ref[:, cols] = rotate(k[:, cols]).astype(jnp.bfloat16)
        v_ref[...] = proj(2).astype(jnp.bfloat16)
        qm_ref[...] = (proj(3) * q_scale).astype(jnp.bfloat16)
        return

    @pl.when(pl.program_id(0) % tiles_per_seq == 0)
    def _():
        carry_ref[...] = jnp.zeros(carry_ref.shape, jnp.float32)

    f = jnp.dot(xn, wf_ref[...], preferred_element_type=jnp.float32) + bf_ref[...]
    q_ref[...] = (proj(0) * q_scale).astype(jnp.bfloat16)
    log_f = jnp.minimum(f, 0.0) - jnp.log1p(jnp.exp(-jnp.abs(f)))
    lane = lax.broadcasted_iota(jnp.int32, (1, LANES), 1)
    hi, mid, lo = _split3(log_f)
    packed = jnp.where(lane < PIECE_LANES, hi, jnp.where(lane < 2 * PIECE_LANES, mid, lo))
    c3 = carry_ref[0:1, :] + jnp.dot(tri_ref[...], packed, preferred_element_type=jnp.float32)
    carry_ref[0:1, :] = c3[c3.shape[0] - 1:, :]
    c = (c3 + pltpu.roll(c3, LANES - PIECE_LANES, 1)) + pltpu.roll(c3, LANES - 2 * PIECE_LANES, 1)
    k_ref[...] = proj(1).astype(jnp.bfloat16)
    hi, mid, lo = _split3(c * LOG2E)
    packed = jnp.where(lane < PIECE_LANES, hi,
                       jnp.where(lane < 2 * PIECE_LANES, pltpu.roll(mid, PIECE_LANES, 1),
                                 pltpu.roll(lo, 2 * PIECE_LANES, 1)))
    x_decay = ones_ref[...] + jnp.dot(packed, sel_ref[...], preferred_element_type=jnp.float32)
    qx_ref[...] = x_decay[:, :LANES].astype(jnp.bfloat16)
    kx_ref[...] = x_decay[:, LANES:].astype(jnp.bfloat16)
    v_ref[...] = proj(2).astype(jnp.bfloat16)
    qm_ref[...] = (jnp.dot(xn, wm_ref[...], preferred_element_type=jnp.float32) * q_scale
                   ).astype(jnp.bfloat16)


def _decay_selectors():
    sel = np.zeros((LANES, 2 * LANES), np.float32)
    ones = np.zeros((1, 2 * LANES), np.float32)
    for h in range(N_MIX_HEADS):
        base = h * DECAY_LANES_PER_HEAD
        for piece in range(DECAY_PIECES):
            sel[piece * PIECE_LANES + h, base + piece] = 1.0
            sel[piece * PIECE_LANES + h, LANES + base + DECAY_PIECES + piece] = -1.0
            ones[0, base + DECAY_PIECES + piece] = 1.0
            ones[0, LANES + base + piece] = 1.0
    return jnp.asarray(sel, jnp.bfloat16), jnp.asarray(ones)


def _forget_lanes(a):
    a = jnp.pad(a, ((0, 0), (0, PIECE_LANES - N_MIX_HEADS)))
    a = jnp.tile(a, (1, DECAY_PIECES))
    return jnp.pad(a, ((0, 0), (0, LANES - DECAY_PIECES * PIECE_LANES)))


def _layer_spec(stacked, index, **kwargs):
    return pl.BlockSpec((None,) + stacked.shape[1:], lambda *_: (index, 0, 0), **kwargs)


def _in_proj(h, g, w, slot, layer, forget_params=None, rope_tabs=None, *, seq):
    t = h.shape[0]
    tm = ROW_TILE
    forget = forget_params is not None
    rope = rope_tabs is not None
    row = lambda i: (i, 0)
    fixed = lambda i: (0, 0)
    x_spec = pl.BlockSpec((tm, D_MODEL), row)
    if rope:
        h = h.reshape(t // DIL_RES, DIL_RES, D_MODEL)
        x_spec = pl.BlockSpec((DIL_RUN, DIL_RES, D_MODEL), lambda i: (i, 0, 0))
    tiles_per_seq = seq // tm
    once = dict(pipeline_mode=pl.Buffered(1))
    in_specs = [x_spec, _layer_spec(g, layer), _layer_spec(w, slot, **once)]
    args = [h, g, w]
    out_shape = [jax.ShapeDtypeStruct((t, D_MIX), jnp.bfloat16)] * 3
    out_shape += [jax.ShapeDtypeStruct((t, D_MEMQ), jnp.bfloat16)]
    out_specs = [pl.BlockSpec((tm, D_MIX), row)] * 3
    out_specs += [pl.BlockSpec((tm, D_MEMQ), row)]
    scratch = []
    if forget:
        w_m, w_f, b_f = forget_params
        sel, ones = _decay_selectors()
        tri = jnp.asarray(np.tril(np.ones((tm, tm), np.float32)), jnp.bfloat16)
        in_specs += [_layer_spec(w_m, slot), _layer_spec(w_f, slot), _layer_spec(b_f, slot),
                     pl.BlockSpec((tm, tm), fixed),
                     pl.BlockSpec((LANES, 2 * LANES), fixed),
                     pl.BlockSpec((1, 2 * LANES), fixed)]
        args += [w_m, w_f, b_f, tri, sel, ones]
        out_shape += [jax.ShapeDtypeStruct((t, LANES), jnp.bfloat16)] * 2
        out_specs += [pl.BlockSpec((tm, LANES), row)] * 2
        scratch.append(pltpu.VMEM((8, LANES), jnp.float32))
    if rope:
        tab = lambda i: (i % tiles_per_seq, 0)
        in_specs += [pl.BlockSpec((tm, LANES), tab), pl.BlockSpec((tm, LANES), tab)]
        args += list(rope_tabs)
    return pl.pallas_call(
        functools.partial(_in_proj_kernel, rope=rope, forget=forget,
                          tiles_per_seq=tiles_per_seq),
        grid=(t // tm,),
        in_specs=in_specs, out_specs=out_specs, out_shape=out_shape,
        scratch_shapes=scratch,
        compiler_params=_cparams(1),
        name="in_proj_rope" if rope else "in_proj_fox",
    )(*args)


def _mem_kv_kernel(x_ref, g_ref, w_ref, km_ref, vm_ref):
    xn = _rms(x_ref[...], g_ref[...]).astype(jnp.bfloat16)
    kv = jnp.dot(xn, w_ref[...].astype(jnp.bfloat16), preferred_element_type=jnp.float32)
    km_ref[...] = kv[:, :D_MEMQ].astype(jnp.bfloat16)
    vm_ref[...] = kv[:, D_MEMQ:].astype(jnp.bfloat16)


def _mem_kv(mem2d, g, w, *, mem_tokens):
    rows = mem2d.shape[0]
    depth = w.shape[0]
    per_layer = lambda shape: pl.BlockSpec((None,) + shape, lambda l, b: (l, 0, 0))
    out_spec = pl.BlockSpec((None, mem_tokens, D_MEMQ), lambda l, b: (l, b, 0))
    return pl.pallas_call(
        _mem_kv_kernel,
        grid=(depth, rows // mem_tokens),
        in_specs=[pl.BlockSpec((mem_tokens, D_MODEL), lambda l, b: (b, 0)),
                  per_layer((1, D_MODEL)),
                  per_layer((D_MODEL, 2 * D_MEMQ))],
        out_specs=[out_spec, out_spec],
        out_shape=[jax.ShapeDtypeStruct((depth, rows, D_MEMQ), jnp.bfloat16)] * 2,
        compiler_params=_cparams(2),
        name="mem_kv",
    )(mem2d, g, w)


def _fox_kernel(q_ref, qx_ref, k_ref, kx_ref, v_ref, o_ref, m_scr, l_scr, acc_scr):
    tq, tk = FOX_Q_TILE, FOX_K_TILE
    n_blocks = tq // tk
    pair = pl.program_id(1)
    qi = pl.program_id(2)
    lane = lax.broadcasted_iota(jnp.int32, (1, LANES), 1)
    first = lane < HEAD_DIM
    head_lanes = (first, jnp.logical_not(first))
    q = q_ref[...]
    qx = qx_ref[...]
    zero = jnp.zeros_like(q)
    q_heads = []
    for hh in range(2):
        lo = (2 * pair + hh) * DECAY_LANES_PER_HEAD
        own_decay = (lane >= lo) & (lane < lo + DECAY_LANES_PER_HEAD)
        q_heads.append(jnp.concatenate([jnp.where(head_lanes[hh], q, zero),
                                        jnp.where(own_decay, qx, zero)], axis=1))

    m_scr[...] = jnp.full(m_scr.shape, NEG, jnp.float32)
    l_scr[...] = jnp.zeros(l_scr.shape, jnp.float32)
    acc_scr[...] = jnp.zeros(acc_scr.shape, jnp.float32)
    ones = jnp.ones((tk, LANES), jnp.bfloat16)

    def load_kv(j):
        start = pl.multiple_of(j * tk, tk)
        k = jnp.concatenate([k_ref[pl.ds(start, tk), :], kx_ref[pl.ds(start, tk), :]],
                            axis=1)
        v = jnp.concatenate([v_ref[pl.ds(start, tk), :], ones], axis=1)
        return k, v

    def chain(hh, r, k, v, diagonal):
        rows = slice(r * tk, (r + 1) * tk)
        s = lax.dot_general(q_heads[hh][rows], k, _NT, preferred_element_type=jnp.float32)
        if diagonal:
            row = lax.broadcasted_iota(jnp.int32, (tk, tk), 0)
            col = lax.broadcasted_iota(jnp.int32, (tk, tk), 1)
            s = jnp.where(col <= row, s, NEG)
        m_prev = m_scr[hh, rows, :]
        m_new = jnp.maximum(m_prev, jnp.max(s, axis=-1, keepdims=True))
        alpha = jnp.exp2(m_prev - m_new)
        p = jnp.exp2(s - jnp.concatenate([m_new] * (tk // LANES), axis=1))
        pv = jnp.dot(p.astype(jnp.bfloat16), v, preferred_element_type=jnp.float32)
        l_scr[hh, rows, :] = alpha * l_scr[hh, rows, :] + pv[:, LANES:]
        acc_scr[hh, rows, :] = alpha * acc_scr[hh, rows, :] + pv[:, :LANES]
        m_scr[hh, rows, :] = m_new

    def body(j, carry):
        for half in range(2):
            k, v = load_kv(2 * j + half)
            for r in range(n_blocks):
                for hh in range(2):
                    chain(hh, r, k, v, False)
        return carry

    assert n_blocks % 2 == 0
    n_before = qi * n_blocks
    lax.fori_loop(0, n_before // 2, body, 0)
    for d in range(n_blocks):
        k, v = load_kv(n_before + d)
        for r in range(d, n_blocks):
            for hh in range(2):
                chain(hh, r, k, v, r == d)
    o0 = acc_scr[0] * (1.0 / l_scr[0])
    o1 = acc_scr[1] * (1.0 / l_scr[1])
    o_ref[...] = jnp.where(first, o0, o1).astype(o_ref.dtype)


def _fox_attention(q, qx, k, kx, v, *, batch, seq):
    tq = FOX_Q_TILE
    as_seq = lambda a: a.reshape(batch, seq, a.shape[-1])
    q_spec = pl.BlockSpec((None, tq, LANES), lambda b, p, i: (b, i, p))
    qx_spec = pl.BlockSpec((None, tq, LANES), lambda b, p, i: (b, i, 0))
    kv_spec = pl.BlockSpec((None, seq, LANES), lambda b, p, i: (b, 0, p))
    kx_spec = pl.BlockSpec((None, seq, LANES), lambda b, p, i: (b, 0, 0))
    out = pl.pallas_call(
        _fox_kernel,
        grid=(batch, N_PAIRS, seq // tq),
        in_specs=[q_spec, qx_spec, kv_spec, kx_spec, kv_spec],
        out_specs=q_spec,
        out_shape=jax.ShapeDtypeStruct((batch, seq, D_MIX), jnp.bfloat16),
        scratch_shapes=[pltpu.VMEM((2, tq, LANES), jnp.float32),
                        pltpu.VMEM((2, tq, LANES), jnp.float32),
                        pltpu.VMEM((2, tq, LANES), jnp.float32)],
        compiler_params=_cparams(3),
        name="fox_attention",
    )(as_seq(q), as_seq(qx), as_seq(k), as_seq(kx), as_seq(v))
    return out.reshape(batch * seq, D_MIX)


def _band_bias(n_q, entry_q, entry_k):
    a = np.arange(2 * n_q)[:, None] % n_q
    c = np.arange(2 * n_q)[None, :]
    dist = entry_q(a) - entry_k(c)
    ok = (dist >= 0) & (dist <= DIL_BLOCK)
    planes = [ok, ok & (c >= n_q)]
    return jnp.asarray(np.where(np.stack(planes), 0.0, NEG), jnp.float32)


def _dilated_masks():
    blk = DIL_BLOCK
    b16 = _band_bias(blk, lambda a: a + blk, lambda c: c)
    run4 = 32
    b4 = _band_bias(blk, lambda a: blk + 4 * (a % run4) + a // run4,
                    lambda c: (c // blk) * blk + 4 * ((c % blk) % run4) + (c % blk) // run4)
    run1 = blk // DIL_RES
    b1 = _band_bias(blk, lambda a: blk + DIL_RES * (a % run1) + a // run1,
                    lambda c: (c // blk) * blk + DIL_RES * ((c % blk) % run1) + (c % blk) // run1)
    return b16, b4, b1


def _dilated_kernel(q_ref, kp_ref, kc_ref, vp_ref, vc_ref, b16_ref, b4_ref, b1_ref,
                    o_ref, m_scr, l_scr, acc_scr):
    first = _lane_first_half()
    seq_start = jnp.where(pl.program_id(1) > 0, 0, 1)

    def gather(ref, slices):
        return jnp.concatenate([ref[s, :] for s in slices], axis=0)

    def chain(q_slices, key_parts, bias, first_branch):
        q = gather(q_ref, q_slices)
        zero = jnp.zeros_like(q)
        q2 = jnp.concatenate([jnp.where(first, q, zero), jnp.where(first, zero, q)], axis=0)
        k2 = jnp.concatenate([gather(kr, sl) for kr, _, sl in key_parts], axis=0)
        v2 = jnp.concatenate([gather(vr, sl) for _, vr, sl in key_parts], axis=0)
        n_q, n_k = q.shape[0], k2.shape[0]
        v2 = jnp.concatenate([v2, jnp.ones((n_k, LANES), jnp.bfloat16)], axis=1)
        s = lax.dot_general(q2, k2, _NT, preferred_element_type=jnp.float32) + bias
        m_cur = jnp.max(s, axis=-1, keepdims=True)
        if first_branch:
            m_new = jnp.broadcast_to(m_cur, (2 * n_q, LANES))
        else:
            m_prev = jnp.concatenate([gather(m_scr.at[hh], q_slices) for hh in range(2)],
                                     axis=0)
            m_new = jnp.maximum(m_prev, m_cur)
            alpha = jnp.exp2(m_prev - m_new)
        p = jnp.exp2(s - jnp.concatenate([m_new] * (n_k // LANES), axis=1))
        pv = jnp.dot(p.astype(jnp.bfloat16), v2, preferred_element_type=jnp.float32)
        num, den = pv[:, :LANES], pv[:, LANES:]
        if not first_branch:
            l_prev = jnp.concatenate([gather(l_scr.at[hh], q_slices) for hh in range(2)],
                                     axis=0)
            a_prev = jnp.concatenate([gather(acc_scr.at[hh], q_slices) for hh in range(2)],
                                     axis=0)
            den = alpha * l_prev + den
            num = alpha * a_prev + num
        for hh in range(2):
            off = hh * n_q
            for sl in q_slices:
                n = sl.stop - sl.start
                m_scr[hh, sl, :] = m_new[off:off + n]
                l_scr[hh, sl, :] = den[off:off + n]
                acc_scr[hh, sl, :] = num[off:off + n]
                off += n

    def run(tile, r, lo, n):
        start = tile * ROW_TILE + r * DIL_RUN + lo
        return slice(start, start + n)

    def block(cur, prv, prev_chunk, bias_ref, first_branch):
        if prev_chunk:
            parts = [(kp_ref, vp_ref, prv), (kc_ref, vc_ref, cur)]
            bias = bias_ref[seq_start]
        else:
            parts = [(kc_ref, vc_ref, prv), (kc_ref, vc_ref, cur)]
            bias = bias_ref[0]
        chain(cur, parts, bias, first_branch)

    last = TILES_PER_CHUNK - 1
    for r in range(DIL_RES):
        rows = [run(t, r, 0, DIL_RUN) for t in range(TILES_PER_CHUNK)]
        block(rows, rows, True, b16_ref, True)
    for r4 in range(4):
        for t in range(TILES_PER_CHUNK):
            cur = [run(t, r4 + 4 * u, 0, DIL_RUN) for u in range(4)]
            prv = [run(t - 1 if t else last, r4 + 4 * u, 0, DIL_RUN) for u in range(4)]
            block(cur, prv, t == 0, b4_ref, False)
    quarter = DIL_BLOCK // DIL_RES
    per_tile = DIL_RUN // quarter
    for t in range(TILES_PER_CHUNK):
        for qt in range(per_tile):
            cur = [run(t, r, qt * quarter, quarter) for r in range(DIL_RES)]
            if qt:
                prv = [run(t, r, (qt - 1) * quarter, quarter) for r in range(DIL_RES)]
            else:
                prv = [run(t - 1 if t else last, r, DIL_RUN - quarter, quarter)
                       for r in range(DIL_RES)]
            block(cur, prv, t == 0 and qt == 0, b1_ref, False)

    o0 = acc_scr[0] * (1.0 / l_scr[0])
    o1 = acc_scr[1] * (1.0 / l_scr[1])
    o_ref[...] = jnp.where(first, o0, o1).astype(o_ref.dtype)


def _dilated_attention(q, k, v, *, batch, seq):
    n_chunks = seq // DIL_CHUNK
    flat = lambda a: a.reshape(batch * n_chunks, DIL_CHUNK, D_MIX)
    masks = _dilated_masks()
    cur = lambda b, c, p: (b * n_chunks + c, 0, p)
    prev = lambda b, c, p: (b * n_chunks + jnp.maximum(c - 1, 0), 0, p)
    spec = lambda f: pl.BlockSpec((None, DIL_CHUNK, LANES), f)
    mask_specs = [pl.BlockSpec(m.shape, lambda b, c, p: (0, 0, 0)) for m in masks]
    out = pl.pallas_call(
        _dilated_kernel,
        grid=(batch, n_chunks, N_PAIRS),
        in_specs=[spec(cur), spec(prev), spec(cur), spec(prev), spec(cur)] + mask_specs,
        out_specs=spec(cur),
        out_shape=jax.ShapeDtypeStruct((batch * n_chunks, DIL_CHUNK, D_MIX), jnp.bfloat16),
        scratch_shapes=[pltpu.VMEM((2, DIL_CHUNK, LANES), jnp.float32)] * 3,
        compiler_params=_cparams(3),
        name="dilated_attention",
    )(flat(q), flat(k), flat(k), flat(v), flat(v), *masks)
    return out.reshape(batch * seq, D_MIX)


def _mem_attention(qm_ref, km_ref, vm_ref):
    first = _lane_first_half()
    qm = qm_ref[...]
    ones = jnp.ones((km_ref.shape[0], LANES), jnp.bfloat16)
    pairs = []
    for p in range(D_MEMQ // LANES):
        cols = slice(p * LANES, (p + 1) * LANES)
        qp = qm[:, cols]
        n = qp.shape[0]
        zero = jnp.zeros_like(qp)
        q2 = jnp.concatenate([jnp.where(first, qp, zero), jnp.where(first, zero, qp)],
                             axis=0)
        v2 = jnp.concatenate([vm_ref[:, cols], ones], axis=1)
        s = lax.dot_general(q2, km_ref[:, cols], _NT, preferred_element_type=jnp.float32)
        e = jnp.exp2(s - jnp.max(s, axis=-1, keepdims=True))
        pv = jnp.dot(e.astype(jnp.bfloat16), v2, preferred_element_type=jnp.float32)
        o = pv[:, :LANES] * (1.0 / pv[:, LANES:])
        pairs.append(jnp.where(first, o[:n], o[n:]).astype(jnp.bfloat16))
    return jnp.concatenate(pairs, axis=-1)


def _layer_tail_kernel(*refs, tiles_per_seq, final_norm, residue_major):
    (h_ref, mix_ref, qm_ref, km_ref, vm_ref, wo_ref, g_ref, wup_ref, cw_ref, cb_ref,
     wdown_ref, gfin_ref, out_ref, carry_ref, act_ref) = refs
    tm = FFN_TILE
    halo = 8
    n_chunks = D_FF // FF_CHUNK
    i = pl.program_id(0)

    @pl.when(i % tiles_per_seq == 0)
    def _():
        carry_ref[...] = jnp.zeros(carry_ref.shape, jnp.float32)

    mem_out = _mem_attention(qm_ref, km_ref, vm_ref)
    y = jnp.dot(mix_ref[...], wo_ref[:D_MIX, :].astype(jnp.bfloat16),
                preferred_element_type=jnp.float32)
    y = y + jnp.dot(mem_out, wo_ref[D_MIX:, :].astype(jnp.bfloat16),
                    preferred_element_type=jnp.float32)
    n_tiles = tm // ROW_TILE
    if residue_major:
        h = jnp.concatenate([h_ref[t * DIL_RUN:(t + 1) * DIL_RUN, r, :]
                             for t in range(n_tiles) for r in range(DIL_RES)], axis=0) + y
    else:
        h = h_ref[...] + y

    xn = _rms(h, g_ref[...]).astype(jnp.bfloat16)
    row = lax.broadcasted_iota(jnp.int32, (tm, FF_CHUNK), 0)
    run_row = lax.broadcasted_iota(jnp.int32, (DIL_RUN, FF_CHUNK), 0)

    def history_natural(u, cols):
        prev = carry_ref[0:halo, cols]
        back1 = jnp.where(row == 0, prev[halo - 1:halo, :], pltpu.roll(u, 1, 0))
        back2 = jnp.where(row == 0, prev[halo - 2:halo - 1, :],
                          jnp.where(row == 1, prev[halo - 1:halo, :], pltpu.roll(u, 2, 0)))
        carry_ref[0:halo, cols] = u[tm - halo:tm, :]
        return back1, back2

    def history_residue_major(u, cols):
        def shifted(run, before):
            return jnp.where(run_row == 0, before, pltpu.roll(run, 1, 0))

        last15 = carry_ref[halo - 1:halo, cols]
        last14 = carry_ref[2 * halo - 1:2 * halo, cols]
        back1, back2 = [], []
        for t in range(n_tiles):
            ut = u[t * ROW_TILE:(t + 1) * ROW_TILE]
            run14 = ut[14 * DIL_RUN:15 * DIL_RUN]
            run15 = ut[15 * DIL_RUN:]
            s15, s14 = shifted(run15, last15), shifted(run14, last14)
            back1 += [s15, ut[:15 * DIL_RUN]]
            back2 += [s14, s15, ut[:14 * DIL_RUN]]
            last15, last14 = run15[DIL_RUN - 1:], run14[DIL_RUN - 1:]
        carry_ref[0:halo, cols] = u[tm - halo:tm, :]
        carry_ref[halo:2 * halo, cols] = u[tm - DIL_RUN - halo:tm - DIL_RUN, :]
        return jnp.concatenate(back1, axis=0), jnp.concatenate(back2, axis=0)

    def conv(u, cols):
        back1, back2 = (history_residue_major if residue_major else history_natural)(u, cols)
        w = cw_ref[:, cols]
        c = cb_ref[:, cols] + w[0:1, :] * back2
        c = c + w[1:2, :] * back1
        return c + w[2:3, :] * u

    def val_cols(j):
        return slice(j * FF_CHUNK, (j + 1) * FF_CHUNK)

    def gate_cols(j):
        return slice(D_FF + j * FF_CHUNK, D_FF + (j + 1) * FF_CHUNK)

    def up(j):
        return (jnp.dot(xn, wup_ref[:, val_cols(j)], preferred_element_type=jnp.float32),
                jnp.dot(xn, wup_ref[:, gate_cols(j)], preferred_element_type=jnp.float32))

    u = up(0)
    for j in range(n_chunks):
        u_next = up(j + 1) if j + 1 < n_chunks else None
        val = conv(u[0], val_cols(j))
        gate = conv(u[1], gate_cols(j))
        act_ref[:, val_cols(j)] = (gate * (1.0 / (1.0 + jnp.exp(-gate))) * val
                                   ).astype(jnp.bfloat16)
        u = u_next
    y = h + jnp.dot(act_ref[...], wdown_ref[...], preferred_element_type=jnp.float32)
    if final_norm:
        y = _rms(y, gfin_ref[...])
    if residue_major:
        for r in range(DIL_RES):
            out_ref[:, r, :] = jnp.concatenate(
                [y[t * ROW_TILE + r * DIL_RUN:t * ROW_TILE + (r + 1) * DIL_RUN]
                 for t in range(n_tiles)], axis=0)
    else:
        out_ref[...] = y


def _layer_tail(h, mix, qm, km, vm, w_out, g, w_up, conv_w, conv_b, w_down, g_final, layer, *,
                seq, mem_tokens, final_norm, residue_major):
    t = h.shape[0]
    tm = FFN_TILE
    tiles_per_seq = seq // tm
    row = lambda i: (i, 0)
    once = dict(pipeline_mode=pl.Buffered(1))
    mem_spec = pl.BlockSpec((None, mem_tokens, D_MEMQ),
                            lambda i: (layer, i // tiles_per_seq, 0))
    h_spec = pl.BlockSpec((tm, D_MODEL), row)
    scratch = [pltpu.VMEM((16, 2 * D_FF), jnp.float32),
               pltpu.VMEM((tm, D_FF), jnp.bfloat16)]
    if residue_major:
        h = h.reshape(t // DIL_RES, DIL_RES, D_MODEL)
        h_spec = pl.BlockSpec((tm // DIL_RES, DIL_RES, D_MODEL), lambda i: (i, 0, 0))
    out = pl.pallas_call(
        functools.partial(_layer_tail_kernel, tiles_per_seq=tiles_per_seq,
                          final_norm=final_norm, residue_major=residue_major),
        grid=(t // tm,),
        in_specs=[h_spec,
                  pl.BlockSpec((tm, D_MIX), row),
                  pl.BlockSpec((tm, D_MEMQ), row),
                  mem_spec, mem_spec,
                  _layer_spec(w_out, layer, **once),
                  _layer_spec(g, layer),
                  _layer_spec(w_up, layer, **once),
                  _layer_spec(conv_w, layer),
                  _layer_spec(conv_b, layer),
                  _layer_spec(w_down, layer, **once),
                  pl.BlockSpec((1, D_MODEL), lambda i: (0, 0))],
        out_specs=h_spec,
        out_shape=jax.ShapeDtypeStruct(h.shape, jnp.float32),
        scratch_shapes=scratch,
        compiler_params=pltpu.CompilerParams(dimension_semantics=("arbitrary",),
                                             vmem_limit_bytes=TAIL_VMEM_LIMIT),
        name="layer_tail_residue_major" if residue_major else "layer_tail",
    )(h, mix, qm, km, vm, w_out, g, w_up, conv_w, conv_b, w_down, g_final)
    return out.reshape(t, D_MODEL)


def _rope_tables(seq):
    def residue_major(tab):
        tab = tab.reshape(seq // ROW_TILE, DIL_RUN, DIL_RES, LANES)
        return tab.transpose(0, 2, 1, 3).reshape(seq, LANES)

    with jax.ensure_compile_time_eval():
        inv = 1.0 / (ROPE_THETA ** (jnp.arange(0, HEAD_DIM, 2, dtype=jnp.float32) / HEAD_DIM))
        ang = jnp.arange(seq, dtype=jnp.float32)[:, None] * inv[None, :]
        cos, sin = jnp.cos(ang), jnp.sin(ang)
        reps = LANES // HEAD_DIM
        cos_t = jnp.tile(jnp.concatenate([cos, cos], axis=-1), (1, reps))
        sin_t = jnp.tile(jnp.concatenate([-sin, sin], axis=-1), (1, reps))
        return residue_major(cos_t), residue_major(sin_t)


def kernel(x, mem, norm_mix, norm_mem, norm_ffn, w_in_fox, b_forget, w_in_dil,
           w_mem_kv, w_out, w_up, conv_w, conv_b, w_down, norm_final):
    batch, seq, _ = x.shape
    assert seq % DIL_CHUNK == 0 and seq % FOX_Q_TILE == 0
    mem_tokens = mem.shape[1]
    depth = norm_mix.shape[0]
    bf = lambda a: a.astype(jnp.bfloat16)
    row3 = lambda a: a.reshape(a.shape[0], 1, a.shape[1])
    h = x.reshape(batch * seq, D_MODEL)
    mem2d = mem.reshape(batch * mem_tokens, D_MODEL)
    rope_tabs = _rope_tables(seq)
    g_mix, g_ffn, g_final = row3(norm_mix), row3(norm_ffn), norm_final.reshape(1, D_MODEL)

    w_up_b, w_down_b = bf(w_up), bf(w_down)
    conv_b3 = row3(conv_b)
    n_fox = w_in_fox.shape[0]
    w_fox = w_in_fox[:, :, :3 * D_MIX]
    tail = w_in_fox[:, :, 3 * D_MIX:]
    forget_params = (
        bf(tail[:, :, N_MIX_HEADS:]),
        bf(_forget_lanes(tail[:, :, :N_MIX_HEADS].reshape(n_fox * D_MODEL, N_MIX_HEADS))
           ).reshape(n_fox, D_MODEL, LANES),
        _forget_lanes(b_forget).reshape(n_fox, 1, LANES))
    km, vm = _mem_kv(mem2d, row3(norm_mem), w_mem_kv, mem_tokens=mem_tokens)

    for layer in range(depth):
        kind, slot = layer % 2, layer // 2
        if kind == 0:
            q, k, v, qm, qx, kx = _in_proj(h, g_mix, w_fox, slot, layer,
                                           forget_params=forget_params, seq=seq)
            mix = _fox_attention(q, qx, k, kx, v, batch=batch, seq=seq)
        else:
            q, k, v, qm = _in_proj(h, g_mix, w_in_dil, slot, layer, rope_tabs=rope_tabs, seq=seq)
            mix = _dilated_attention(q, k, v, batch=batch, seq=seq)
        h = _layer_tail(h, mix, qm, km, vm, w_out, g_ffn, w_up_b, conv_w, conv_b3, w_down_b,
                        g_final, layer, seq=seq, mem_tokens=mem_tokens,
                        final_norm=(layer == depth - 1), residue_major=(kind == 1))
    return h.reshape(batch, seq, D_MODEL)
```

```python
import functools

import numpy as np

import jax
import jax.numpy as jnp
from jax import lax
from jax.experimental import pallas as pl
from jax.experimental.pallas import tpu as pltpu

D_MODEL = 1024
HEAD_DIM = 64
N_MEM_HEADS = 4
N_MIX_HEADS = 12
D_MIX = N_MIX_HEADS * HEAD_DIM
D_MEMQ = N_MEM_HEADS * HEAD_DIM
D_FF = 2816
CONV_WIDTH = 3
ROPE_THETA = 10000.0
DIL_BLOCK = 128
NORM_EPS = 1e-6
NEG = -1e30
SCALE = HEAD_DIM ** -0.5
LOG2E = 1.4426950408889634

LANES = 128
N_PAIRS = D_MIX // LANES
VMEM_LIMIT = 48 * 1024 * 1024
TAIL_VMEM_LIMIT = 56 * 1024 * 1024

ROW_TILE = 512
FFN_TILE = 1024
FOX_Q_TILE = 2048
FOX_K_TILE = 512
FF_CHUNK = 256
DECAY_PIECES = 3
DECAY_LANES_PER_HEAD = 2 * DECAY_PIECES
PIECE_LANES = 16
DIL_CHUNK = 2048
DIL_RES = 16
DIL_RUN = ROW_TILE // DIL_RES
TILES_PER_CHUNK = DIL_CHUNK // ROW_TILE

_NT = (((1,), (1,)), ((), ()))


def _cparams(n_axes):
    return pltpu.CompilerParams(
        dimension_semantics=("arbitrary",) * n_axes,
        vmem_limit_bytes=VMEM_LIMIT)


def _rms(x, g):
    y = x * lax.rsqrt(jnp.mean(x * x, axis=-1, keepdims=True) + NORM_EPS)
    return y * g


def _lane_first_half():
    lane = lax.broadcasted_iota(jnp.int32, (1, LANES), 1)
    return lane < HEAD_DIM


def _split3(x):
    hi = x.astype(jnp.bfloat16)
    rem = x - hi.astype(jnp.float32)
    mid = rem.astype(jnp.bfloat16)
    lo = (rem - mid.astype(jnp.float32)).astype(jnp.bfloat16)
    return hi, mid, lo


def _in_proj_kernel(*refs, rope, forget, gather, tiles_per_seq):
    x_ref, g_ref, w_ref = refs[:3]
    pos = 3
    if forget:
        wm_ref, wf_ref, bf_ref, tri_ref, sel_ref, ones_ref = refs[pos:pos + 6]; pos += 6
    if rope:
        cos_ref, sin_ref = refs[pos:pos + 2]; pos += 2
    q_ref, k_ref, v_ref, qm_ref = refs[pos:pos + 4]; pos += 4
    if forget:
        qx_ref, kx_ref, carry_ref = refs[pos:pos + 3]

    if gather:
        x = jnp.concatenate([x_ref[:, r, :] for r in range(DIL_RES)], axis=0)
    else:
        x = x_ref[...]
    xn = _rms(x, g_ref[...]).astype(jnp.bfloat16)
    q_scale = SCALE * LOG2E

    def proj(index):
        width = D_MIX if index < 3 else D_MEMQ
        w = w_ref[:, index * D_MIX:index * D_MIX + width].astype(jnp.bfloat16)
        return jnp.dot(xn, w, preferred_element_type=jnp.float32)

    def rotate(t):
        lane = lax.broadcasted_iota(jnp.int32, (1, LANES), 1)
        first = (lane % HEAD_DIM) < (HEAD_DIM // 2)
        partner = jnp.where(first, pltpu.roll(t, LANES - HEAD_DIM // 2, 1),
                            pltpu.roll(t, HEAD_DIM // 2, 1))
        return t * cos_ref[...] + partner * sin_ref[...]

    if rope:
        q = proj(0)
        for p in range(N_PAIRS):
            cols = slice(p * LANES, (p + 1) * LANES)
            q_ref[:, cols] = (rotate(q[:, cols]) * q_scale).astype(jnp.bfloat16)
        k = proj(1)
        for p in range(N_PAIRS):
            cols = slice(p * LANES, (p + 1) * LANES)
            k_ref[:, cols] = rotate(k[:, cols]).astype(jnp.bfloat16)
        v_ref[...] = proj(2).astype(jnp.bfloat16)
        qm_ref[...] = (proj(3) * q_scale).astype(jnp.bfloat16)
        return

    @pl.when(pl.program_id(0) % tiles_per_seq == 0)
    def _():
        carry_ref[...] = jnp.zeros(carry_ref.shape, jnp.float32)

    f = jnp.dot(xn, wf_ref[...], preferred_element_type=jnp.float32) + bf_ref[...]
    q_ref[...] = (proj(0) * q_scale).astype(jnp.bfloat16)
    log_f = jnp.minimum(f, 0.0) - jnp.log1p(jnp.exp(-jnp.abs(f)))
    lane = lax.broadcasted_iota(jnp.int32, (1, LANES), 1)
    hi, mid, lo = _split3(log_f)
    packed = jnp.where(lane < PIECE_LANES, hi, jnp.where(lane < 2 * PIECE_LANES, mid, lo))
    c3 = carry_ref[0:1, :] + jnp.dot(tri_ref[...], packed, preferred_element_type=jnp.float32)
    carry_ref[0:1, :] = c3[c3.shape[0] - 1:, :]
    c = (c3 + pltpu.roll(c3, LANES - PIECE_LANES, 1)) + pltpu.roll(c3, LANES - 2 * PIECE_LANES, 1)
    k_ref[...] = proj(1).astype(jnp.bfloat16)
    hi, mid, lo = _split3(c * LOG2E)
    packed = jnp.where(lane < PIECE_LANES, hi,
                       jnp.where(lane < 2 * PIECE_LANES, pltpu.roll(mid, PIECE_LANES, 1),
                                 pltpu.roll(lo, 2 * PIECE_LANES, 1)))
    x_decay = ones_ref[...] + jnp.dot(packed, sel_ref[...], preferred_element_type=jnp.float32)
    qx_ref[...] = x_decay[:, :LANES].astype(jnp.bfloat16)
    kx_ref[...] = x_decay[:, LANES:].astype(jnp.bfloat16)
    v_ref[...] = proj(2).astype(jnp.bfloat16)
    qm_ref[...] = (jnp.dot(xn, wm_ref[...], preferred_element_type=jnp.float32) * q_scale
                   ).astype(jnp.bfloat16)


def _decay_selectors():
    sel = np.zeros((LANES, 2 * LANES), np.float32)
    ones = np.zeros((1, 2 * LANES), np.float32)
    for h in range(N_MIX_HEADS):
        base = h * DECAY_LANES_PER_HEAD
        for piece in range(DECAY_PIECES):
            sel[piece * PIECE_LANES + h, base + piece] = 1.0
            sel[piece * PIECE_LANES + h, LANES + base + DECAY_PIECES + piece] = -1.0
            ones[0, base + DECAY_PIECES + piece] = 1.0
            ones[0, LANES + base + piece] = 1.0
    return jnp.asarray(sel, jnp.bfloat16), jnp.asarray(ones)


def _tile_positions():
    rows = np.arange(ROW_TILE)
    return DIL_RES * (rows % DIL_RUN) + rows // DIL_RUN


def _forget_lanes(a):
    a = jnp.pad(a, ((0, 0), (0, PIECE_LANES - N_MIX_HEADS)))
    a = jnp.tile(a, (1, DECAY_PIECES))
    return jnp.pad(a, ((0, 0), (0, LANES - DECAY_PIECES * PIECE_LANES)))


def _layer_spec(stacked, index, **kwargs):
    return pl.BlockSpec((None,) + stacked.shape[1:], lambda *_: (index, 0, 0), **kwargs)


def _in_proj(h, g, w, slot, layer, forget_params=None, rope_tabs=None, *, seq, gather):
    t = h.shape[0]
    tm = ROW_TILE
    forget = forget_params is not None
    rope = rope_tabs is not None
    row = lambda i: (i, 0)
    fixed = lambda i: (0, 0)
    x_spec = pl.BlockSpec((tm, D_MODEL), row)
    if gather:
        h = h.reshape(t // DIL_RES, DIL_RES, D_MODEL)
        x_spec = pl.BlockSpec((DIL_RUN, DIL_RES, D_MODEL), lambda i: (i, 0, 0))
    tiles_per_seq = seq // tm
    once = dict(pipeline_mode=pl.Buffered(1))
    in_specs = [x_spec, _layer_spec(g, layer), _layer_spec(w, slot, **once)]
    args = [h, g, w]
    out_shape = [jax.ShapeDtypeStruct((t, D_MIX), jnp.bfloat16)] * 3
    out_shape += [jax.ShapeDtypeStruct((t, D_MEMQ), jnp.bfloat16)]
    out_specs = [pl.BlockSpec((tm, D_MIX), row)] * 3
    out_specs += [pl.BlockSpec((tm, D_MEMQ), row)]
    scratch = []
    if forget:
        w_m, w_f, b_f = forget_params
        sel, ones = _decay_selectors()
        pos = _tile_positions()
        tri = jnp.asarray(pos[None, :] <= pos[:, None], jnp.bfloat16)
        in_specs += [_layer_spec(w_m, slot), _layer_spec(w_f, slot), _layer_spec(b_f, slot),
                     pl.BlockSpec((tm, tm), fixed),
                     pl.BlockSpec((LANES, 2 * LANES), fixed),
                     pl.BlockSpec((1, 2 * LANES), fixed)]
        args += [w_m, w_f, b_f, tri, sel, ones]
        out_shape += [jax.ShapeDtypeStruct((t, LANES), jnp.bfloat16)] * 2
        out_specs += [pl.BlockSpec((tm, LANES), row)] * 2
        scratch.append(pltpu.VMEM((8, LANES), jnp.float32))
    if rope:
        tab = lambda i: (i % tiles_per_seq, 0)
        in_specs += [pl.BlockSpec((tm, LANES), tab), pl.BlockSpec((tm, LANES), tab)]
        args += list(rope_tabs)
    return pl.pallas_call(
        functools.partial(_in_proj_kernel, rope=rope, forget=forget, gather=gather,
                          tiles_per_seq=tiles_per_seq),
        grid=(t // tm,),
        in_specs=in_specs, out_specs=out_specs, out_shape=out_shape,
        scratch_shapes=scratch,
        compiler_params=_cparams(1),
        name="in_proj_rope" if rope else "in_proj_fox",
    )(*args)


def _mem_kv_kernel(x_ref, g_ref, w_ref, km_ref, vm_ref):
    xn = _rms(x_ref[...], g_ref[...]).astype(jnp.bfloat16)
    kv = jnp.dot(xn, w_ref[...].astype(jnp.bfloat16), preferred_element_type=jnp.float32)
    km_ref[...] = kv[:, :D_MEMQ].astype(jnp.bfloat16)
    vm_ref[...] = kv[:, D_MEMQ:].astype(jnp.bfloat16)


def _mem_kv(mem2d, g, w, *, mem_tokens):
    rows = mem2d.shape[0]
    depth = w.shape[0]
    per_layer = lambda shape: pl.BlockSpec((None,) + shape, lambda l, b: (l, 0, 0))
    out_spec = pl.BlockSpec((None, mem_tokens, D_MEMQ), lambda l, b: (l, b, 0))
    return pl.pallas_call(
        _mem_kv_kernel,
        grid=(depth, rows // mem_tokens),
        in_specs=[pl.BlockSpec((mem_tokens, D_MODEL), lambda l, b: (b, 0)),
                  per_layer((1, D_MODEL)),
                  per_layer((D_MODEL, 2 * D_MEMQ))],
        out_specs=[out_spec, out_spec],
        out_shape=[jax.ShapeDtypeStruct((depth, rows, D_MEMQ), jnp.bfloat16)] * 2,
        compiler_params=_cparams(2),
        name="mem_kv",
    )(mem2d, g, w)


def _fox_kernel(q_ref, qx_ref, k_ref, kx_ref, v_ref, o_ref, m_scr, l_scr, acc_scr):
    tq, tk = FOX_Q_TILE, FOX_K_TILE
    n_blocks = tq // tk
    pair = pl.program_id(1)
    qi = pl.program_id(2)
    lane = lax.broadcasted_iota(jnp.int32, (1, LANES), 1)
    first = lane < HEAD_DIM
    head_lanes = (first, jnp.logical_not(first))
    q = q_ref[...]
    qx = qx_ref[...]
    zero = jnp.zeros_like(q)
    q_heads = []
    for hh in range(2):
        lo = (2 * pair + hh) * DECAY_LANES_PER_HEAD
        own_decay = (lane >= lo) & (lane < lo + DECAY_LANES_PER_HEAD)
        q_heads.append(jnp.concatenate([jnp.where(head_lanes[hh], q, zero),
                                        jnp.where(own_decay, qx, zero)], axis=1))

    m_scr[...] = jnp.full(m_scr.shape, NEG, jnp.float32)
    l_scr[...] = jnp.zeros(l_scr.shape, jnp.float32)
    acc_scr[...] = jnp.zeros(acc_scr.shape, jnp.float32)
    ones = jnp.ones((tk, LANES), jnp.bfloat16)

    def load_kv(j):
        start = pl.multiple_of(j * tk, tk)
        k = jnp.concatenate([k_ref[pl.ds(start, tk), :], kx_ref[pl.ds(start, tk), :]],
                            axis=1)
        v = jnp.concatenate([v_ref[pl.ds(start, tk), :], ones], axis=1)
        return k, v

    def chain(hh, r, k, v, diagonal):
        rows = slice(r * tk, (r + 1) * tk)
        s = lax.dot_general(q_heads[hh][rows], k, _NT, preferred_element_type=jnp.float32)
        if diagonal:
            row = lax.broadcasted_iota(jnp.int32, (tk, tk), 0)
            col = lax.broadcasted_iota(jnp.int32, (tk, tk), 1)
            row_pos = DIL_RES * (row % DIL_RUN) + row // DIL_RUN
            col_pos = DIL_RES * (col % DIL_RUN) + col // DIL_RUN
            s = jnp.where(col_pos <= row_pos, s, NEG)
        m_prev = m_scr[hh, rows, :]
        m_new = jnp.maximum(m_prev, jnp.max(s, axis=-1, keepdims=True))
        alpha = jnp.exp2(m_prev - m_new)
        p = jnp.exp2(s - jnp.concatenate([m_new] * (tk // LANES), axis=1))
        pv = jnp.dot(p.astype(jnp.bfloat16), v, preferred_element_type=jnp.float32)
        l_scr[hh, rows, :] = alpha * l_scr[hh, rows, :] + pv[:, LANES:]
        acc_scr[hh, rows, :] = alpha * acc_scr[hh, rows, :] + pv[:, :LANES]
        m_scr[hh, rows, :] = m_new

    def body(j, carry):
        for half in range(2):
            k, v = load_kv(2 * j + half)
            for r in range(n_blocks):
                for hh in range(2):
                    chain(hh, r, k, v, False)
        return carry

    assert n_blocks % 2 == 0
    n_before = qi * n_blocks
    lax.fori_loop(0, n_before // 2, body, 0)
    for d in range(n_blocks):
        k, v = load_kv(n_before + d)
        for r in range(d, n_blocks):
            for hh in range(2):
                chain(hh, r, k, v, r == d)
    o0 = acc_scr[0] * (1.0 / l_scr[0])
    o1 = acc_scr[1] * (1.0 / l_scr[1])
    o_ref[...] = jnp.where(first, o0, o1).astype(o_ref.dtype)


def _fox_attention(q, qx, k, kx, v, *, batch, seq):
    tq = FOX_Q_TILE
    as_seq = lambda a: a.reshape(batch, seq, a.shape[-1])
    q_spec = pl.BlockSpec((None, tq, LANES), lambda b, p, i: (b, i, p))
    qx_spec = pl.BlockSpec((None, tq, LANES), lambda b, p, i: (b, i, 0))
    kv_spec = pl.BlockSpec((None, seq, LANES), lambda b, p, i: (b, 0, p))
    kx_spec = pl.BlockSpec((None, seq, LANES), lambda b, p, i: (b, 0, 0))
    out = pl.pallas_call(
        _fox_kernel,
        grid=(batch, N_PAIRS, seq // tq),
        in_specs=[q_spec, qx_spec, kv_spec, kx_spec, kv_spec],
        out_specs=q_spec,
        out_shape=jax.ShapeDtypeStruct((batch, seq, D_MIX), jnp.bfloat16),
        scratch_shapes=[pltpu.VMEM((2, tq, LANES), jnp.float32),
                        pltpu.VMEM((2, tq, LANES), jnp.float32),
                        pltpu.VMEM((2, tq, LANES), jnp.float32)],
        compiler_params=_cparams(3),
        name="fox_attention",
    )(as_seq(q), as_seq(qx), as_seq(k), as_seq(kx), as_seq(v))
    return out.reshape(batch * seq, D_MIX)


def _band_bias(n_q, entry_q, entry_k):
    a = np.arange(2 * n_q)[:, None] % n_q
    c = np.arange(2 * n_q)[None, :]
    dist = entry_q(a) - entry_k(c)
    ok = (dist >= 0) & (dist <= DIL_BLOCK)
    planes = [ok, ok & (c >= n_q)]
    return jnp.asarray(np.where(np.stack(planes), 0.0, NEG), jnp.float32)


def _dilated_masks():
    blk = DIL_BLOCK
    b16 = _band_bias(blk, lambda a: a + blk, lambda c: c)
    run4 = 32
    b4 = _band_bias(blk, lambda a: blk + 4 * (a % run4) + a // run4,
                    lambda c: (c // blk) * blk + 4 * ((c % blk) % run4) + (c % blk) // run4)
    run1 = blk // DIL_RES
    b1 = _band_bias(blk, lambda a: blk + DIL_RES * (a % run1) + a // run1,
                    lambda c: (c // blk) * blk + DIL_RES * ((c % blk) % run1) + (c % blk) // run1)
    return b16, b4, b1


def _dilated_kernel(q_ref, kp_ref, kc_ref, vp_ref, vc_ref, b16_ref, b4_ref, b1_ref,
                    o_ref, m_scr, l_scr, acc_scr):
    first = _lane_first_half()
    seq_start = jnp.where(pl.program_id(1) > 0, 0, 1)

    def gather(ref, slices):
        return jnp.concatenate([ref[s, :] for s in slices], axis=0)

    def chain(q_slices, key_parts, bias, first_branch):
        q = gather(q_ref, q_slices)
        zero = jnp.zeros_like(q)
        q2 = jnp.concatenate([jnp.where(first, q, zero), jnp.where(first, zero, q)], axis=0)
        k2 = jnp.concatenate([gather(kr, sl) for kr, _, sl in key_parts], axis=0)
        v2 = jnp.concatenate([gather(vr, sl) for _, vr, sl in key_parts], axis=0)
        n_q, n_k = q.shape[0], k2.shape[0]
        v2 = jnp.concatenate([v2, jnp.ones((n_k, LANES), jnp.bfloat16)], axis=1)
        s = lax.dot_general(q2, k2, _NT, preferred_element_type=jnp.float32) + bias
        m_cur = jnp.max(s, axis=-1, keepdims=True)
        if first_branch:
            m_new = jnp.broadcast_to(m_cur, (2 * n_q, LANES))
        else:
            m_prev = jnp.concatenate([gather(m_scr.at[hh], q_slices) for hh in range(2)],
                                     axis=0)
            m_new = jnp.maximum(m_prev, m_cur)
            alpha = jnp.exp2(m_prev - m_new)
        p = jnp.exp2(s - jnp.concatenate([m_new] * (n_k // LANES), axis=1))
        pv = jnp.dot(p.astype(jnp.bfloat16), v2, preferred_element_type=jnp.float32)
        num, den = pv[:, :LANES], pv[:, LANES:]
        if not first_branch:
            l_prev = jnp.concatenate([gather(l_scr.at[hh], q_slices) for hh in range(2)],
                                     axis=0)
            a_prev = jnp.concatenate([gather(acc_scr.at[hh], q_slices) for hh in range(2)],
                                     axis=0)
            den = alpha * l_prev + den
            num = alpha * a_prev + num
        for hh in range(2):
            off = hh * n_q
            for sl in q_slices:
                n = sl.stop - sl.start
                m_scr[hh, sl, :] = m_new[off:off + n]
                l_scr[hh, sl, :] = den[off:off + n]
                acc_scr[hh, sl, :] = num[off:off + n]
                off += n

    def run(tile, r, lo, n):
        start = tile * ROW_TILE + r * DIL_RUN + lo
        return slice(start, start + n)

    def block(cur, prv, prev_chunk, bias_ref, first_branch):
        if prev_chunk:
            parts = [(kp_ref, vp_ref, prv), (kc_ref, vc_ref, cur)]
            bias = bias_ref[seq_start]
        else:
            parts = [(kc_ref, vc_ref, prv), (kc_ref, vc_ref, cur)]
            bias = bias_ref[0]
        chain(cur, parts, bias, first_branch)

    last = TILES_PER_CHUNK - 1
    for r in range(DIL_RES):
        rows = [run(t, r, 0, DIL_RUN) for t in range(TILES_PER_CHUNK)]
        block(rows, rows, True, b16_ref, True)
    for r4 in range(4):
        for t in range(TILES_PER_CHUNK):
            cur = [run(t, r4 + 4 * u, 0, DIL_RUN) for u in range(4)]
            prv = [run(t - 1 if t else last, r4 + 4 * u, 0, DIL_RUN) for u in range(4)]
            block(cur, prv, t == 0, b4_ref, False)
    quarter = DIL_BLOCK // DIL_RES
    per_tile = DIL_RUN // quarter
    for t in range(TILES_PER_CHUNK):
        for qt in range(per_tile):
            cur = [run(t, r, qt * quarter, quarter) for r in range(DIL_RES)]
            if qt:
                prv = [run(t, r, (qt - 1) * quarter, quarter) for r in range(DIL_RES)]
            else:
                prv = [run(t - 1 if t else last, r, DIL_RUN - quarter, quarter)
                       for r in range(DIL_RES)]
            block(cur, prv, t == 0 and qt == 0, b1_ref, False)

    o0 = acc_scr[0] * (1.0 / l_scr[0])
    o1 = acc_scr[1] * (1.0 / l_scr[1])
    o_ref[...] = jnp.where(first, o0, o1).astype(o_ref.dtype)


def _dilated_attention(q, k, v, *, batch, seq):
    n_chunks = seq // DIL_CHUNK
    flat = lambda a: a.reshape(batch * n_chunks, DIL_CHUNK, D_MIX)
    masks = _dilated_masks()
    cur = lambda b, c, p: (b * n_chunks + c, 0, p)
    prev = lambda b, c, p: (b * n_chunks + jnp.maximum(c - 1, 0), 0, p)
    spec = lambda f: pl.BlockSpec((None, DIL_CHUNK, LANES), f)
    mask_specs = [pl.BlockSpec(m.shape, lambda b, c, p: (0, 0, 0)) for m in masks]
    out = pl.pallas_call(
        _dilated_kernel,
        grid=(batch, n_chunks, N_PAIRS),
        in_specs=[spec(cur), spec(prev), spec(cur), spec(prev), spec(cur)] + mask_specs,
        out_specs=spec(cur),
        out_shape=jax.ShapeDtypeStruct((batch * n_chunks, DIL_CHUNK, D_MIX), jnp.bfloat16),
        scratch_shapes=[pltpu.VMEM((2, DIL_CHUNK, LANES), jnp.float32)] * 3,
        compiler_params=_cparams(3),
        name="dilated_attention",
    )(flat(q), flat(k), flat(k), flat(v), flat(v), *masks)
    return out.reshape(batch * seq, D_MIX)


def _mem_attention(qm_ref, km_ref, vm_ref):
    first = _lane_first_half()
    qm = qm_ref[...]
    ones = jnp.ones((km_ref.shape[0], LANES), jnp.bfloat16)
    pairs = []
    for p in range(D_MEMQ // LANES):
        cols = slice(p * LANES, (p + 1) * LANES)
        qp = qm[:, cols]
        n = qp.shape[0]
        zero = jnp.zeros_like(qp)
        q2 = jnp.concatenate([jnp.where(first, qp, zero), jnp.where(first, zero, qp)],
                             axis=0)
        v2 = jnp.concatenate([vm_ref[:, cols], ones], axis=1)
        s = lax.dot_general(q2, km_ref[:, cols], _NT, preferred_element_type=jnp.float32)
        e = jnp.exp2(s - jnp.max(s, axis=-1, keepdims=True))
        pv = jnp.dot(e.astype(jnp.bfloat16), v2, preferred_element_type=jnp.float32)
        o = pv[:, :LANES] * (1.0 / pv[:, LANES:])
        pairs.append(jnp.where(first, o[:n], o[n:]).astype(jnp.bfloat16))
    return jnp.concatenate(pairs, axis=-1)


def _layer_tail_kernel(*refs, tiles_per_seq, final_norm, gather_in, scatter_out):
    (h_ref, mix_ref, qm_ref, km_ref, vm_ref, wo_ref, g_ref, wup_ref, cw_ref, cb_ref,
     wdown_ref, gfin_ref, out_ref, carry_ref, act_ref) = refs
    tm = FFN_TILE
    halo = 8
    n_chunks = D_FF // FF_CHUNK
    i = pl.program_id(0)

    @pl.when(i % tiles_per_seq == 0)
    def _():
        carry_ref[...] = jnp.zeros(carry_ref.shape, jnp.float32)

    mem_out = _mem_attention(qm_ref, km_ref, vm_ref)
    y = jnp.dot(mix_ref[...], wo_ref[:D_MIX, :].astype(jnp.bfloat16),
                preferred_element_type=jnp.float32)
    y = y + jnp.dot(mem_out, wo_ref[D_MIX:, :].astype(jnp.bfloat16),
                    preferred_element_type=jnp.float32)
    n_tiles = tm // ROW_TILE
    if gather_in:
        h = jnp.concatenate([h_ref[t * DIL_RUN:(t + 1) * DIL_RUN, r, :]
                             for t in range(n_tiles) for r in range(DIL_RES)], axis=0) + y
    else:
        h = h_ref[...] + y

    xn = _rms(h, g_ref[...]).astype(jnp.bfloat16)
    run_row = lax.broadcasted_iota(jnp.int32, (DIL_RUN, FF_CHUNK), 0)

    def history(u, cols):
        def shifted(run, before):
            return jnp.where(run_row == 0, before, pltpu.roll(run, 1, 0))

        last15 = carry_ref[halo - 1:halo, cols]
        last14 = carry_ref[2 * halo - 1:2 * halo, cols]
        back1, back2 = [], []
        for t in range(n_tiles):
            ut = u[t * ROW_TILE:(t + 1) * ROW_TILE]
            run14 = ut[14 * DIL_RUN:15 * DIL_RUN]
            run15 = ut[15 * DIL_RUN:]
            s15, s14 = shifted(run15, last15), shifted(run14, last14)
            back1 += [s15, ut[:15 * DIL_RUN]]
            back2 += [s14, s15, ut[:14 * DIL_RUN]]
            last15, last14 = run15[DIL_RUN - 1:], run14[DIL_RUN - 1:]
        carry_ref[0:halo, cols] = u[tm - halo:tm, :]
        carry_ref[halo:2 * halo, cols] = u[tm - DIL_RUN - halo:tm - DIL_RUN, :]
        return jnp.concatenate(back1, axis=0), jnp.concatenate(back2, axis=0)

    def conv(u, cols):
        back1, back2 = history(u, cols)
        w = cw_ref[:, cols]
        c = cb_ref[:, cols] + w[0:1, :] * back2
        c = c + w[1:2, :] * back1
        return c + w[2:3, :] * u

    def val_cols(j):
        return slice(j * FF_CHUNK, (j + 1) * FF_CHUNK)

    def gate_cols(j):
        return slice(D_FF + j * FF_CHUNK, D_FF + (j + 1) * FF_CHUNK)

    def up(j):
        return (jnp.dot(xn, wup_ref[:, val_cols(j)], preferred_element_type=jnp.float32),
                jnp.dot(xn, wup_ref[:, gate_cols(j)], preferred_element_type=jnp.float32))

    u = up(0)
    for j in range(n_chunks):
        u_next = up(j + 1) if j + 1 < n_chunks else None
        val = conv(u[0], val_cols(j))
        gate = conv(u[1], gate_cols(j))
        act_ref[:, val_cols(j)] = (gate * (1.0 / (1.0 + jnp.exp(-gate))) * val
                                   ).astype(jnp.bfloat16)
        u = u_next
    y = h + jnp.dot(act_ref[...], wdown_ref[...], preferred_element_type=jnp.float32)
    if final_norm:
        y = _rms(y, gfin_ref[...])
    if scatter_out:
        for r in range(DIL_RES):
            out_ref[:, r, :] = jnp.concatenate(
                [y[t * ROW_TILE + r * DIL_RUN:t * ROW_TILE + (r + 1) * DIL_RUN]
                 for t in range(n_tiles)], axis=0)
    else:
        out_ref[...] = y


def _layer_tail(h, mix, qm, km, vm, w_out, g, w_up, conv_w, conv_b, w_down, g_final, layer, *,
                seq, mem_tokens, final_norm, gather_in, scatter_out):
    t = h.shape[0]
    tm = FFN_TILE
    tiles_per_seq = seq // tm
    row = lambda i: (i, 0)
    once = dict(pipeline_mode=pl.Buffered(1))
    mem_spec = pl.BlockSpec((None, mem_tokens, D_MEMQ),
                            lambda i: (layer, i // tiles_per_seq, 0))
    plain_spec = pl.BlockSpec((tm, D_MODEL), row)
    view_shape = (t // DIL_RES, DIL_RES, D_MODEL)
    view_spec = pl.BlockSpec((tm // DIL_RES, DIL_RES, D_MODEL), lambda i: (i, 0, 0))
    scratch = [pltpu.VMEM((16, 2 * D_FF), jnp.float32),
               pltpu.VMEM((tm, D_FF), jnp.bfloat16)]
    if gather_in:
        h = h.reshape(view_shape)
    out = pl.pallas_call(
        functools.partial(_layer_tail_kernel, tiles_per_seq=tiles_per_seq, final_norm=final_norm,
                          gather_in=gather_in, scatter_out=scatter_out),
        grid=(t // tm,),
        in_specs=[view_spec if gather_in else plain_spec,
                  pl.BlockSpec((tm, D_MIX), row),
                  pl.BlockSpec((tm, D_MEMQ), row),
                  mem_spec, mem_spec,
                  _layer_spec(w_out, layer, **once),
                  _layer_spec(g, layer),
                  _layer_spec(w_up, layer, **once),
                  _layer_spec(conv_w, layer),
                  _layer_spec(conv_b, layer),
                  _layer_spec(w_down, layer, **once),
                  pl.BlockSpec((1, D_MODEL), lambda i: (0, 0))],
        out_specs=view_spec if scatter_out else plain_spec,
        out_shape=jax.ShapeDtypeStruct(view_shape if scatter_out else (t, D_MODEL), jnp.float32),
        scratch_shapes=scratch,
        compiler_params=pltpu.CompilerParams(dimension_semantics=("arbitrary",),
                                             vmem_limit_bytes=TAIL_VMEM_LIMIT),
        name="layer_tail" + ("_in" if gather_in else "") + ("_out" if scatter_out else ""),
    )(h, mix, qm, km, vm, w_out, g, w_up, conv_w, conv_b, w_down, g_final)
    return out.reshape(t, D_MODEL)


def _rope_tables(seq):
    def residue_major(tab):
        tab = tab.reshape(seq // ROW_TILE, DIL_RUN, DIL_RES, LANES)
        return tab.transpose(0, 2, 1, 3).reshape(seq, LANES)

    with jax.ensure_compile_time_eval():
        inv = 1.0 / (ROPE_THETA ** (jnp.arange(0, HEAD_DIM, 2, dtype=jnp.float32) / HEAD_DIM))
        ang = jnp.arange(seq, dtype=jnp.float32)[:, None] * inv[None, :]
        cos, sin = jnp.cos(ang), jnp.sin(ang)
        reps = LANES // HEAD_DIM
        cos_t = jnp.tile(jnp.concatenate([cos, cos], axis=-1), (1, reps))
        sin_t = jnp.tile(jnp.concatenate([-sin, sin], axis=-1), (1, reps))
        return residue_major(cos_t), residue_major(sin_t)


def kernel(x, mem, norm_mix, norm_mem, norm_ffn, w_in_fox, b_forget, w_in_dil,
           w_mem_kv, w_out, w_up, conv_w, conv_b, w_down, norm_final):
    batch, seq, _ = x.shape
    assert seq % DIL_CHUNK == 0 and seq % FOX_Q_TILE == 0 and FOX_K_TILE == ROW_TILE
    mem_tokens = mem.shape[1]
    depth = norm_mix.shape[0]
    bf = lambda a: a.astype(jnp.bfloat16)
    row3 = lambda a: a.reshape(a.shape[0], 1, a.shape[1])
    h = x.reshape(batch * seq, D_MODEL)
    mem2d = mem.reshape(batch * mem_tokens, D_MODEL)
    rope_tabs = _rope_tables(seq)
    g_mix, g_ffn, g_final = row3(norm_mix), row3(norm_ffn), norm_final.reshape(1, D_MODEL)

    w_up_b, w_down_b = bf(w_up), bf(w_down)
    conv_b3 = row3(conv_b)
    n_fox = w_in_fox.shape[0]
    w_fox = w_in_fox[:, :, :3 * D_MIX]
    tail = w_in_fox[:, :, 3 * D_MIX:]
    forget_params = (
        bf(tail[:, :, N_MIX_HEADS:]),
        bf(_forget_lanes(tail[:, :, :N_MIX_HEADS].reshape(n_fox * D_MODEL, N_MIX_HEADS))
           ).reshape(n_fox, D_MODEL, LANES),
        _forget_lanes(b_forget).reshape(n_fox, 1, LANES))
    km, vm = _mem_kv(mem2d, row3(norm_mem), w_mem_kv, mem_tokens=mem_tokens)

    for layer in range(depth):
        kind, slot = layer % 2, layer // 2
        if kind == 0:
            q, k, v, qm, qx, kx = _in_proj(h, g_mix, w_fox, slot, layer,
                                           forget_params=forget_params, seq=seq,
                                           gather=(layer == 0))
            mix = _fox_attention(q, qx, k, kx, v, batch=batch, seq=seq)
        else:
            q, k, v, qm = _in_proj(h, g_mix, w_in_dil, slot, layer, rope_tabs=rope_tabs, seq=seq,
                                   gather=(layer == 0))
            mix = _dilated_attention(q, k, v, batch=batch, seq=seq)
        h = _layer_tail(h, mix, qm, km, vm, w_out, g_ffn, w_up_b, conv_w, conv_b3, w_down_b,
                        g_final, layer, seq=seq, mem_tokens=mem_tokens,
                        final_norm=(layer == depth - 1), gather_in=(layer == 0),
                        scatter_out=(layer == depth - 1))
    return h.reshape(batch, seq, D_MODEL)
```

```python
import functools

import numpy as np

import jax
import jax.numpy as jnp
from jax import lax
from jax.experimental import pallas as pl
from jax.experimental.pallas import tpu as pltpu

D_MODEL = 1024
HEAD_DIM = 64
N_MEM_HEADS = 4
N_MIX_HEADS = 12
D_MIX = N_MIX_HEADS * HEAD_DIM
D_MEMQ = N_MEM_HEADS * HEAD_DIM
D_FF = 2816
CONV_WIDTH = 3
ROPE_THETA = 10000.0
DIL_BLOCK = 128
NORM_EPS = 1e-6
NEG = -1e30
SCALE = HEAD_DIM ** -0.5
LOG2E = 1.4426950408889634

LANES = 128
N_PAIRS = D_MIX // LANES
VMEM_LIMIT = 48 * 1024 * 1024
TAIL_VMEM_LIMIT = 56 * 1024 * 1024

ROW_TILE = 512
FFN_TILE = 1024
FOX_Q_TILE = 2048
FOX_K_TILE = 512
FF_CHUNK = 256
DECAY_PIECES = 3
DECAY_LANES_PER_HEAD = 2 * DECAY_PIECES
PIECE_LANES = 16
DIL_CHUNK = 2048
DIL_RES = 16
DIL_RUN = ROW_TILE // DIL_RES
TILES_PER_CHUNK = DIL_CHUNK // ROW_TILE

_NT = (((1,), (1,)), ((), ()))


def _cparams(n_axes):
    return pltpu.CompilerParams(
        dimension_semantics=("arbitrary",) * n_axes,
        vmem_limit_bytes=VMEM_LIMIT)


def _rms(x, g):
    y = x * lax.rsqrt(jnp.mean(x * x, axis=-1, keepdims=True) + NORM_EPS)
    return y * g


def _lane_first_half():
    lane = lax.broadcasted_iota(jnp.int32, (1, LANES), 1)
    return lane < HEAD_DIM


def _split3(x):
    hi = x.astype(jnp.bfloat16)
    rem = x - hi.astype(jnp.float32)
    mid = rem.astype(jnp.bfloat16)
    lo = (rem - mid.astype(jnp.float32)).astype(jnp.bfloat16)
    return hi, mid, lo


def _in_proj_kernel(*refs, rope, forget, gather, tiles_per_seq):
    x_ref, g_ref, w_ref = refs[:3]
    pos = 3
    if forget:
        wm_ref, wf_ref, bf_ref, tri_ref, sel_ref, ones_ref = refs[pos:pos + 6]; pos += 6
    if rope:
        cos_ref, sin_ref = refs[pos:pos + 2]; pos += 2
    q_ref, k_ref, v_ref, qm_ref = refs[pos:pos + 4]; pos += 4
    if forget:
        qx_ref, kx_ref, carry_ref = refs[pos:pos + 3]

    if gather:
        x = jnp.concatenate([x_ref[:, r, :] for r in range(DIL_RES)], axis=0)
    else:
        x = x_ref[...]
    xn = _rms(x, g_ref[...]).astype(jnp.bfloat16)
    q_scale = SCALE * LOG2E

    def proj(index):
        width = D_MIX if index < 3 else D_MEMQ
        w = w_ref[:, index * D_MIX:index * D_MIX + width].astype(jnp.bfloat16)
        return jnp.dot(xn, w, preferred_element_type=jnp.float32)

    def rotate(t):
        lane = lax.broadcasted_iota(jnp.int32, (1, LANES), 1)
        first = (lane % HEAD_DIM) < (HEAD_DIM // 2)
        partner = jnp.where(first, pltpu.roll(t, LANES - HEAD_DIM // 2, 1),
                            pltpu.roll(t, HEAD_DIM // 2, 1))
        return t * cos_ref[...] + partner * sin_ref[...]

    if rope:
        q = proj(0)
        for p in range(N_PAIRS):
            cols = slice(p * LANES, (p + 1) * LANES)
            q_ref[:, cols] = (rotate(q[:, cols]) * q_scale).astype(jnp.bfloat16)
        k = proj(1)
        for p in range(N_PAIRS):
            cols = slice(p * LANES, (p + 1) * LANES)
            k_ref[:, cols] = rotate(k[:, cols]).astype(jnp.bfloat16)
        v_ref[...] = proj(2).astype(jnp.bfloat16)
        qm_ref[...] = (proj(3) * q_scale).astype(jnp.bfloat16)
        return

    @pl.when(pl.program_id(0) % tiles_per_seq == 0)
    def _():
        carry_ref[...] = jnp.zeros(carry_ref.shape, jnp.float32)

    f = jnp.dot(xn, wf_ref[...], preferred_element_type=jnp.float32) + bf_ref[...]
    q_ref[...] = (proj(0) * q_scale).astype(jnp.bfloat16)
    log_f = jnp.minimum(f, 0.0) - jnp.log1p(jnp.exp(-jnp.abs(f)))
    lane = lax.broadcasted_iota(jnp.int32, (1, LANES), 1)
    hi, mid, lo = _split3(log_f)
    packed = jnp.where(lane < PIECE_LANES, hi, jnp.where(lane < 2 * PIECE_LANES, mid, lo))
    c3 = carry_ref[0:1, :] + jnp.dot(tri_ref[...], packed, preferred_element_type=jnp.float32)
    carry_ref[0:1, :] = c3[c3.shape[0] - 1:, :]
    c = (c3 + pltpu.roll(c3, LANES - PIECE_LANES, 1)) + pltpu.roll(c3, LANES - 2 * PIECE_LANES, 1)
    k_ref[...] = proj(1).astype(jnp.bfloat16)
    hi, mid, lo = _split3(c * LOG2E)
    packed = jnp.where(lane < PIECE_LANES, hi,
                       jnp.where(lane < 2 * PIECE_LANES, pltpu.roll(mid, PIECE_LANES, 1),
                                 pltpu.roll(lo, 2 * PIECE_LANES, 1)))
    x_decay = ones_ref[...] + jnp.dot(packed, sel_ref[...], preferred_element_type=jnp.float32)
    qx_ref[...] = x_decay[:, :LANES].astype(jnp.bfloat16)
    kx_ref[...] = x_decay[:, LANES:].astype(jnp.bfloat16)
    v_ref[...] = proj(2).astype(jnp.bfloat16)
    qm_ref[...] = (jnp.dot(xn, wm_ref[...], preferred_element_type=jnp.float32) * q_scale
                   ).astype(jnp.bfloat16)


def _decay_selectors():
    sel = np.zeros((LANES, 2 * LANES), np.float32)
    ones = np.zeros((1, 2 * LANES), np.float32)
    for h in range(N_MIX_HEADS):
        base = h * DECAY_LANES_PER_HEAD
        for piece in range(DECAY_PIECES):
            sel[piece * PIECE_LANES + h, base + piece] = 1.0
            sel[piece * PIECE_LANES + h, LANES + base + DECAY_PIECES + piece] = -1.0
            ones[0, base + DECAY_PIECES + piece] = 1.0
            ones[0, LANES + base + piece] = 1.0
    return jnp.asarray(sel, jnp.bfloat16), jnp.asarray(ones)


def _tile_positions():
    rows = np.arange(ROW_TILE)
    return DIL_RES * (rows % DIL_RUN) + rows // DIL_RUN


def _forget_lanes(a):
    a = jnp.pad(a, ((0, 0), (0, PIECE_LANES - N_MIX_HEADS)))
    a = jnp.tile(a, (1, DECAY_PIECES))
    return jnp.pad(a, ((0, 0), (0, LANES - DECAY_PIECES * PIECE_LANES)))


def _layer_spec(stacked, index, **kwargs):
    return pl.BlockSpec((None,) + stacked.shape[1:], lambda *_: (index, 0, 0), **kwargs)


def _in_proj(h, g, w, slot, layer, forget_params=None, rope_tabs=None, *, seq, gather):
    t = h.shape[0]
    tm = ROW_TILE
    forget = forget_params is not None
    rope = rope_tabs is not None
    row = lambda i: (i, 0)
    fixed = lambda i: (0, 0)
    x_spec = pl.BlockSpec((tm, D_MODEL), row)
    if gather:
        h = h.reshape(t // DIL_RES, DIL_RES, D_MODEL)
        x_spec = pl.BlockSpec((DIL_RUN, DIL_RES, D_MODEL), lambda i: (i, 0, 0))
    tiles_per_seq = seq // tm
    once = dict(pipeline_mode=pl.Buffered(1))
    in_specs = [x_spec, _layer_spec(g, layer), _layer_spec(w, slot, **once)]
    args = [h, g, w]
    out_shape = [jax.ShapeDtypeStruct((t, D_MIX), jnp.bfloat16)] * 3
    out_shape += [jax.ShapeDtypeStruct((t, D_MEMQ), jnp.bfloat16)]
    out_specs = [pl.BlockSpec((tm, D_MIX), row)] * 3
    out_specs += [pl.BlockSpec((tm, D_MEMQ), row)]
    scratch = []
    if forget:
        w_m, w_f, b_f = forget_params
        sel, ones = _decay_selectors()
        pos = _tile_positions()
        tri = jnp.asarray(pos[None, :] <= pos[:, None], jnp.bfloat16)
        in_specs += [_layer_spec(w_m, slot), _layer_spec(w_f, slot), _layer_spec(b_f, slot),
                     pl.BlockSpec((tm, tm), fixed),
                     pl.BlockSpec((LANES, 2 * LANES), fixed),
                     pl.BlockSpec((1, 2 * LANES), fixed)]
        args += [w_m, w_f, b_f, tri, sel, ones]
        out_shape += [jax.ShapeDtypeStruct((t, LANES), jnp.bfloat16)] * 2
        out_specs += [pl.BlockSpec((tm, LANES), row)] * 2
        scratch.append(pltpu.VMEM((8, LANES), jnp.float32))
    if rope:
        tab = lambda i: (i % tiles_per_seq, 0)
        in_specs += [pl.BlockSpec((tm, LANES), tab), pl.BlockSpec((tm, LANES), tab)]
        args += list(rope_tabs)
    return pl.pallas_call(
        functools.partial(_in_proj_kernel, rope=rope, forget=forget, gather=gather,
                          tiles_per_seq=tiles_per_seq),
        grid=(t // tm,),
        in_specs=in_specs, out_specs=out_specs, out_shape=out_shape,
        scratch_shapes=scratch,
        compiler_params=_cparams(1),
        name="in_proj_rope" if rope else "in_proj_fox",
    )(*args)


def _mem_kv_kernel(x_ref, g_ref, w_ref, km_ref, vm_ref):
    xn = _rms(x_ref[...], g_ref[...]).astype(jnp.bfloat16)
    kv = jnp.dot(xn, w_ref[...].astype(jnp.bfloat16), preferred_element_type=jnp.float32)
    km_ref[...] = kv[:, :D_MEMQ].astype(jnp.bfloat16)
    vm_ref[...] = kv[:, D_MEMQ:].astype(jnp.bfloat16)


def _mem_kv(mem2d, g, w, *, mem_tokens):
    rows = mem2d.shape[0]
    depth = w.shape[0]
    per_layer = lambda shape: pl.BlockSpec((None,) + shape, lambda l, b: (l, 0, 0))
    out_spec = pl.BlockSpec((None, mem_tokens, D_MEMQ), lambda l, b: (l, b, 0))
    return pl.pallas_call(
        _mem_kv_kernel,
        grid=(depth, rows // mem_tokens),
        in_specs=[pl.BlockSpec((mem_tokens, D_MODEL), lambda l, b: (b, 0)),
                  per_layer((1, D_MODEL)),
                  per_layer((D_MODEL, 2 * D_MEMQ))],
        out_specs=[out_spec, out_spec],
        out_shape=[jax.ShapeDtypeStruct((depth, rows, D_MEMQ), jnp.bfloat16)] * 2,
        compiler_params=_cparams(2),
        name="mem_kv",
    )(mem2d, g, w)


def _fox_kernel(q_ref, qx_ref, k_ref, kx_ref, v_ref, o_ref, m_scr, l_scr, acc_scr):
    tq, tk = FOX_Q_TILE, FOX_K_TILE
    n_blocks = tq // tk
    pair = pl.program_id(1)
    qi = pl.program_id(2)
    lane = lax.broadcasted_iota(jnp.int32, (1, LANES), 1)
    first = lane < HEAD_DIM
    head_lanes = (first, jnp.logical_not(first))
    q = q_ref[...]
    qx = qx_ref[...]
    zero = jnp.zeros_like(q)
    q_heads = []
    for hh in range(2):
        lo = (2 * pair + hh) * DECAY_LANES_PER_HEAD
        own_decay = (lane >= lo) & (lane < lo + DECAY_LANES_PER_HEAD)
        q_heads.append(jnp.concatenate([jnp.where(head_lanes[hh], q, zero),
                                        jnp.where(own_decay, qx, zero)], axis=1))

    m_scr[...] = jnp.full(m_scr.shape, NEG, jnp.float32)
    l_scr[...] = jnp.zeros(l_scr.shape, jnp.float32)
    acc_scr[...] = jnp.zeros(acc_scr.shape, jnp.float32)
    ones = jnp.ones((tk, LANES), jnp.bfloat16)

    def load_kv(j):
        start = pl.multiple_of(j * tk, tk)
        k = jnp.concatenate([k_ref[pl.ds(start, tk), :], kx_ref[pl.ds(start, tk), :]],
                            axis=1)
        v = jnp.concatenate([v_ref[pl.ds(start, tk), :], ones], axis=1)
        return k, v

    def chain(hh, r, k, v, diagonal):
        rows = slice(r * tk, (r + 1) * tk)
        s = lax.dot_general(q_heads[hh][rows], k, _NT, preferred_element_type=jnp.float32)
        if diagonal:
            row = lax.broadcasted_iota(jnp.int32, (tk, 1), 0)
            col = lax.broadcasted_iota(jnp.int32, (1, tk), 1)
            row_pos = DIL_RES * (row % DIL_RUN) + row // DIL_RUN
            col_pos = DIL_RES * (col % DIL_RUN) + col // DIL_RUN
            s = jnp.where(col_pos <= row_pos, s, NEG)
        m_prev = m_scr[hh, rows, :]
        m_new = jnp.maximum(m_prev, jnp.max(s, axis=-1, keepdims=True))
        alpha = jnp.exp2(m_prev - m_new)
        p = jnp.exp2(s - jnp.concatenate([m_new] * (tk // LANES), axis=1))
        pv = jnp.dot(p.astype(jnp.bfloat16), v, preferred_element_type=jnp.float32)
        l_scr[hh, rows, :] = alpha * l_scr[hh, rows, :] + pv[:, LANES:]
        acc_scr[hh, rows, :] = alpha * acc_scr[hh, rows, :] + pv[:, :LANES]
        m_scr[hh, rows, :] = m_new

    def body(j, carry):
        for half in range(2):
            k, v = load_kv(2 * j + half)
            for r in range(n_blocks):
                for hh in range(2):
                    chain(hh, r, k, v, False)
        return carry

    assert n_blocks % 2 == 0
    n_before = qi * n_blocks
    lax.fori_loop(0, n_before // 2, body, 0)
    for d in range(n_blocks):
        k, v = load_kv(n_before + d)
        for r in range(d, n_blocks):
            for hh in range(2):
                chain(hh, r, k, v, r == d)
    o0 = acc_scr[0] * (1.0 / l_scr[0])
    o1 = acc_scr[1] * (1.0 / l_scr[1])
    o_ref[...] = jnp.where(first, o0, o1).astype(o_ref.dtype)


def _fox_attention(q, qx, k, kx, v, *, batch, seq):
    tq = FOX_Q_TILE
    as_seq = lambda a: a.reshape(batch, seq, a.shape[-1])
    q_spec = pl.BlockSpec((None, tq, LANES), lambda b, p, i: (b, i, p))
    qx_spec = pl.BlockSpec((None, tq, LANES), lambda b, p, i: (b, i, 0))
    kv_spec = pl.BlockSpec((None, seq, LANES), lambda b, p, i: (b, 0, p))
    kx_spec = pl.BlockSpec((None, seq, LANES), lambda b, p, i: (b, 0, 0))
    out = pl.pallas_call(
        _fox_kernel,
        grid=(batch, N_PAIRS, seq // tq),
        in_specs=[q_spec, qx_spec, kv_spec, kx_spec, kv_spec],
        out_specs=q_spec,
        out_shape=jax.ShapeDtypeStruct((batch, seq, D_MIX), jnp.bfloat16),
        scratch_shapes=[pltpu.VMEM((2, tq, LANES), jnp.float32),
                        pltpu.VMEM((2, tq, LANES), jnp.float32),
                        pltpu.VMEM((2, tq, LANES), jnp.float32)],
        compiler_params=_cparams(3),
        name="fox_attention",
    )(as_seq(q), as_seq(qx), as_seq(k), as_seq(kx), as_seq(v))
    return out.reshape(batch * seq, D_MIX)


def _band_bias(n_q, entry_q, entry_k):
    a = np.arange(2 * n_q)[:, None] % n_q
    c = np.arange(2 * n_q)[None, :]
    dist = entry_q(a) - entry_k(c)
    ok = (dist >= 0) & (dist <= DIL_BLOCK)
    planes = [ok, ok & (c >= n_q)]
    return jnp.asarray(np.where(np.stack(planes), 0.0, NEG), jnp.float32)


def _dilated_masks():
    blk = DIL_BLOCK
    b16 = _band_bias(blk, lambda a: a + blk, lambda c: c)
    run4 = 32
    b4 = _band_bias(blk, lambda a: blk + 4 * (a % run4) + a // run4,
                    lambda c: (c // blk) * blk + 4 * ((c % blk) % run4) + (c % blk) // run4)
    run1 = blk // DIL_RES
    b1 = _band_bias(blk, lambda a: blk + DIL_RES * (a % run1) + a // run1,
                    lambda c: (c // blk) * blk + DIL_RES * ((c % blk) % run1) + (c % blk) // run1)
    return b16, b4, b1


def _dilated_kernel(q_ref, kp_ref, kc_ref, vp_ref, vc_ref, b16_ref, b4_ref, b1_ref,
                    o_ref, m_scr, l_scr, acc_scr):
    first = _lane_first_half()
    seq_start = jnp.where(pl.program_id(1) > 0, 0, 1)

    def gather(ref, slices):
        return jnp.concatenate([ref[s, :] for s in slices], axis=0)

    def chain(q_slices, key_parts, bias, first_branch):
        q = gather(q_ref, q_slices)
        zero = jnp.zeros_like(q)
        q2 = jnp.concatenate([jnp.where(first, q, zero), jnp.where(first, zero, q)], axis=0)
        k2 = jnp.concatenate([gather(kr, sl) for kr, _, sl in key_parts], axis=0)
        v2 = jnp.concatenate([gather(vr, sl) for _, vr, sl in key_parts], axis=0)
        n_q, n_k = q.shape[0], k2.shape[0]
        v2 = jnp.concatenate([v2, jnp.ones((n_k, LANES), jnp.bfloat16)], axis=1)
        s = lax.dot_general(q2, k2, _NT, preferred_element_type=jnp.float32) + bias
        m_cur = jnp.max(s, axis=-1, keepdims=True)
        if first_branch:
            m_new = jnp.broadcast_to(m_cur, (2 * n_q, LANES))
        else:
            m_prev = jnp.concatenate([gather(m_scr.at[hh], q_slices) for hh in range(2)],
                                     axis=0)
            m_new = jnp.maximum(m_prev, m_cur)
            alpha = jnp.exp2(m_prev - m_new)
        p = jnp.exp2(s - jnp.concatenate([m_new] * (n_k // LANES), axis=1))
        pv = jnp.dot(p.astype(jnp.bfloat16), v2, preferred_element_type=jnp.float32)
        num, den = pv[:, :LANES], pv[:, LANES:]
        if not first_branch:
            l_prev = jnp.concatenate([gather(l_scr.at[hh], q_slices) for hh in range(2)],
                                     axis=0)
            a_prev = jnp.concatenate([gather(acc_scr.at[hh], q_slices) for hh in range(2)],
                                     axis=0)
            den = alpha * l_prev + den
            num = alpha * a_prev + num
        for hh in range(2):
            off = hh * n_q
            for sl in q_slices:
                n = sl.stop - sl.start
                m_scr[hh, sl, :] = m_new[off:off + n]
                l_scr[hh, sl, :] = den[off:off + n]
                acc_scr[hh, sl, :] = num[off:off + n]
                off += n

    def run(tile, r, lo, n):
        start = tile * ROW_TILE + r * DIL_RUN + lo
        return slice(start, start + n)

    def block(cur, prv, prev_chunk, bias_ref, first_branch):
        if prev_chunk:
            parts = [(kp_ref, vp_ref, prv), (kc_ref, vc_ref, cur)]
            bias = bias_ref[seq_start]
        else:
            parts = [(kc_ref, vc_ref, prv), (kc_ref, vc_ref, cur)]
            bias = bias_ref[0]
        chain(cur, parts, bias, first_branch)

    last = TILES_PER_CHUNK - 1
    for r in range(DIL_RES):
        rows = [run(t, r, 0, DIL_RUN) for t in range(TILES_PER_CHUNK)]
        block(rows, rows, True, b16_ref, True)
    for r4 in range(4):
        for t in range(TILES_PER_CHUNK):
            cur = [run(t, r4 + 4 * u, 0, DIL_RUN) for u in range(4)]
            prv = [run(t - 1 if t else last, r4 + 4 * u, 0, DIL_RUN) for u in range(4)]
            block(cur, prv, t == 0, b4_ref, False)
    quarter = DIL_BLOCK // DIL_RES
    per_tile = DIL_RUN // quarter
    for t in range(TILES_PER_CHUNK):
        for qt in range(per_tile):
            cur = [run(t, r, qt * quarter, quarter) for r in range(DIL_RES)]
            if qt:
                prv = [run(t, r, (qt - 1) * quarter, quarter) for r in range(DIL_RES)]
            else:
                prv = [run(t - 1 if t else last, r, DIL_RUN - quarter, quarter)
                       for r in range(DIL_RES)]
            block(cur, prv, t == 0 and qt == 0, b1_ref, False)

    o0 = acc_scr[0] * (1.0 / l_scr[0])
    o1 = acc_scr[1] * (1.0 / l_scr[1])
    o_ref[...] = jnp.where(first, o0, o1).astype(o_ref.dtype)


def _dilated_attention(q, k, v, *, batch, seq):
    n_chunks = seq // DIL_CHUNK
    flat = lambda a: a.reshape(batch * n_chunks, DIL_CHUNK, D_MIX)
    masks = _dilated_masks()
    cur = lambda b, c, p: (b * n_chunks + c, 0, p)
    prev = lambda b, c, p: (b * n_chunks + jnp.maximum(c - 1, 0), 0, p)
    spec = lambda f: pl.BlockSpec((None, DIL_CHUNK, LANES), f)
    mask_specs = [pl.BlockSpec(m.shape, lambda b, c, p: (0, 0, 0)) for m in masks]
    out = pl.pallas_call(
        _dilated_kernel,
        grid=(batch, n_chunks, N_PAIRS),
        in_specs=[spec(cur), spec(prev), spec(cur), spec(prev), spec(cur)] + mask_specs,
        out_specs=spec(cur),
        out_shape=jax.ShapeDtypeStruct((batch * n_chunks, DIL_CHUNK, D_MIX), jnp.bfloat16),
        scratch_shapes=[pltpu.VMEM((2, DIL_CHUNK, LANES), jnp.float32)] * 3,
        compiler_params=_cparams(3),
        name="dilated_attention",
    )(flat(q), flat(k), flat(k), flat(v), flat(v), *masks)
    return out.reshape(batch * seq, D_MIX)


def _mem_attention(qm_ref, km_ref, vm_ref):
    first = _lane_first_half()
    qm = qm_ref[...]
    ones = jnp.ones((km_ref.shape[0], LANES), jnp.bfloat16)
    pairs = []
    for p in range(D_MEMQ // LANES):
        cols = slice(p * LANES, (p + 1) * LANES)
        qp = qm[:, cols]
        n = qp.shape[0]
        zero = jnp.zeros_like(qp)
        q2 = jnp.concatenate([jnp.where(first, qp, zero), jnp.where(first, zero, qp)],
                             axis=0)
        v2 = jnp.concatenate([vm_ref[:, cols], ones], axis=1)
        s = lax.dot_general(q2, km_ref[:, cols], _NT, preferred_element_type=jnp.float32)
        e = jnp.exp2(s - jnp.max(s, axis=-1, keepdims=True))
        pv = jnp.dot(e.astype(jnp.bfloat16), v2, preferred_element_type=jnp.float32)
        o = pv[:, :LANES] * (1.0 / pv[:, LANES:])
        pairs.append(jnp.where(first, o[:n], o[n:]).astype(jnp.bfloat16))
    return jnp.concatenate(pairs, axis=-1)


def _layer_tail_kernel(*refs, tiles_per_seq, final_norm, gather_in, scatter_out):
    (h_ref, mix_ref, qm_ref, km_ref, vm_ref, wo_ref, g_ref, wup_ref, cw_ref, cb_ref,
     wdown_ref, gfin_ref, out_ref, carry_ref, act_ref) = refs
    tm = FFN_TILE
    halo = 8
    n_chunks = D_FF // FF_CHUNK
    i = pl.program_id(0)

    @pl.when(i % tiles_per_seq == 0)
    def _():
        carry_ref[...] = jnp.zeros(carry_ref.shape, jnp.float32)

    mem_out = _mem_attention(qm_ref, km_ref, vm_ref)
    y = jnp.dot(mix_ref[...], wo_ref[:D_MIX, :].astype(jnp.bfloat16),
                preferred_element_type=jnp.float32)
    y = y + jnp.dot(mem_out, wo_ref[D_MIX:, :].astype(jnp.bfloat16),
                    preferred_element_type=jnp.float32)
    n_tiles = tm // ROW_TILE
    if gather_in:
        h = jnp.concatenate([h_ref[t * DIL_RUN:(t + 1) * DIL_RUN, r, :]
                             for t in range(n_tiles) for r in range(DIL_RES)], axis=0) + y
    else:
        h = h_ref[...] + y

    xn = _rms(h, g_ref[...]).astype(jnp.bfloat16)
    run_row = lax.broadcasted_iota(jnp.int32, (DIL_RUN, FF_CHUNK), 0)

    def history(u, cols):
        def shifted(run, before):
            return jnp.where(run_row == 0, before, pltpu.roll(run, 1, 0))

        last15 = carry_ref[halo - 1:halo, cols]
        last14 = carry_ref[2 * halo - 1:2 * halo, cols]
        back1, back2 = [], []
        for t in range(n_tiles):
            ut = u[t * ROW_TILE:(t + 1) * ROW_TILE]
            run14 = ut[14 * DIL_RUN:15 * DIL_RUN]
            run15 = ut[15 * DIL_RUN:]
            s15, s14 = shifted(run15, last15), shifted(run14, last14)
            back1 += [s15, ut[:15 * DIL_RUN]]
            back2 += [s14, s15, ut[:14 * DIL_RUN]]
            last15, last14 = run15[DIL_RUN - 1:], run14[DIL_RUN - 1:]
        carry_ref[0:halo, cols] = u[tm - halo:tm, :]
        carry_ref[halo:2 * halo, cols] = u[tm - DIL_RUN - halo:tm - DIL_RUN, :]
        return jnp.concatenate(back1, axis=0), jnp.concatenate(back2, axis=0)

    def conv(u, cols):
        back1, back2 = history(u, cols)
        w = cw_ref[:, cols]
        c = cb_ref[:, cols] + w[0:1, :] * back2
        c = c + w[1:2, :] * back1
        return c + w[2:3, :] * u

    def val_cols(j):
        return slice(j * FF_CHUNK, (j + 1) * FF_CHUNK)

    def gate_cols(j):
        return slice(D_FF + j * FF_CHUNK, D_FF + (j + 1) * FF_CHUNK)

    def up(j):
        return (jnp.dot(xn, wup_ref[:, val_cols(j)], preferred_element_type=jnp.float32),
                jnp.dot(xn, wup_ref[:, gate_cols(j)], preferred_element_type=jnp.float32))

    u = up(0)
    for j in range(n_chunks):
        u_next = up(j + 1) if j + 1 < n_chunks else None
        val = conv(u[0], val_cols(j))
        gate = conv(u[1], gate_cols(j))
        act_ref[:, val_cols(j)] = (gate * (1.0 / (1.0 + jnp.exp(-gate))) * val
                                   ).astype(jnp.bfloat16)
        u = u_next
    y = h + jnp.dot(act_ref[...], wdown_ref[...], preferred_element_type=jnp.float32)
    if final_norm:
        y = _rms(y, gfin_ref[...])
    if scatter_out:
        for r in range(DIL_RES):
            out_ref[:, r, :] = jnp.concatenate(
                [y[t * ROW_TILE + r * DIL_RUN:t * ROW_TILE + (r + 1) * DIL_RUN]
                 for t in range(n_tiles)], axis=0)
    else:
        out_ref[...] = y


def _layer_tail(h, mix, qm, km, vm, w_out, g, w_up, conv_w, conv_b, w_down, g_final, layer, *,
                seq, mem_tokens, final_norm, gather_in, scatter_out):
    t = h.shape[0]
    tm = FFN_TILE
    tiles_per_seq = seq // tm
    row = lambda i: (i, 0)
    once = dict(pipeline_mode=pl.Buffered(1))
    mem_spec = pl.BlockSpec((None, mem_tokens, D_MEMQ),
                            lambda i: (layer, i // tiles_per_seq, 0))
    plain_spec = pl.BlockSpec((tm, D_MODEL), row)
    view_shape = (t // DIL_RES, DIL_RES, D_MODEL)
    view_spec = pl.BlockSpec((tm // DIL_RES, DIL_RES, D_MODEL), lambda i: (i, 0, 0))
    scratch = [pltpu.VMEM((16, 2 * D_FF), jnp.float32),
               pltpu.VMEM((tm, D_FF), jnp.bfloat16)]
    if gather_in:
        h = h.reshape(view_shape)
    out = pl.pallas_call(
        functools.partial(_layer_tail_kernel, tiles_per_seq=tiles_per_seq, final_norm=final_norm,
                          gather_in=gather_in, scatter_out=scatter_out),
        grid=(t // tm,),
        in_specs=[view_spec if gather_in else plain_spec,
                  pl.BlockSpec((tm, D_MIX), row),
                  pl.BlockSpec((tm, D_MEMQ), row),
                  mem_spec, mem_spec,
                  _layer_spec(w_out, layer, **once),
                  _layer_spec(g, layer),
                  _layer_spec(w_up, layer, **once),
                  _layer_spec(conv_w, layer),
                  _layer_spec(conv_b, layer),
                  _layer_spec(w_down, layer, **once),
                  pl.BlockSpec((1, D_MODEL), lambda i: (0, 0))],
        out_specs=view_spec if scatter_out else plain_spec,
        out_shape=jax.ShapeDtypeStruct(view_shape if scatter_out else (t, D_MODEL), jnp.float32),
        scratch_shapes=scratch,
        compiler_params=pltpu.CompilerParams(dimension_semantics=("arbitrary",),
                                             vmem_limit_bytes=TAIL_VMEM_LIMIT),
        name="layer_tail" + ("_in" if gather_in else "") + ("_out" if scatter_out else ""),
    )(h, mix, qm, km, vm, w_out, g, w_up, conv_w, conv_b, w_down, g_final)
    return out.reshape(t, D_MODEL)


def _rope_tables(seq):
    def residue_major(tab):
        tab = tab.reshape(seq // ROW_TILE, DIL_RUN, DIL_RES, LANES)
        return tab.transpose(0, 2, 1, 3).reshape(seq, LANES)

    with jax.ensure_compile_time_eval():
        inv = 1.0 / (ROPE_THETA ** (jnp.arange(0, HEAD_DIM, 2, dtype=jnp.float32) / HEAD_DIM))
        ang = jnp.arange(seq, dtype=jnp.float32)[:, None] * inv[None, :]
        cos, sin = jnp.cos(ang), jnp.sin(ang)
        reps = LANES // HEAD_DIM
        cos_t = jnp.tile(jnp.concatenate([cos, cos], axis=-1), (1, reps))
        sin_t = jnp.tile(jnp.concatenate([-sin, sin], axis=-1), (1, reps))
        return residue_major(cos_t), residue_major(sin_t)


def kernel(x, mem, norm_mix, norm_mem, norm_ffn, w_in_fox, b_forget, w_in_dil,
           w_mem_kv, w_out, w_up, conv_w, conv_b, w_down, norm_final):
    batch, seq, _ = x.shape
    assert seq % DIL_CHUNK == 0 and seq % FOX_Q_TILE == 0 and FOX_K_TILE == ROW_TILE
    mem_tokens = mem.shape[1]
    depth = norm_mix.shape[0]
    bf = lambda a: a.astype(jnp.bfloat16)
    row3 = lambda a: a.reshape(a.shape[0], 1, a.shape[1])
    h = x.reshape(batch * seq, D_MODEL)
    mem2d = mem.reshape(batch * mem_tokens, D_MODEL)
    rope_tabs = _rope_tables(seq)
    g_mix, g_ffn, g_final = row3(norm_mix), row3(norm_ffn), norm_final.reshape(1, D_MODEL)

    w_up_b, w_down_b = bf(w_up), bf(w_down)
    conv_b3 = row3(conv_b)
    n_fox = w_in_fox.shape[0]
    w_fox = w_in_fox[:, :, :3 * D_MIX]
    tail = w_in_fox[:, :, 3 * D_MIX:]
    forget_params = (
        bf(tail[:, :, N_MIX_HEADS:]),
        bf(_forget_lanes(tail[:, :, :N_MIX_HEADS].reshape(n_fox * D_MODEL, N_MIX_HEADS))
           ).reshape(n_fox, D_MODEL, LANES),
        _forget_lanes(b_forget).reshape(n_fox, 1, LANES))
    km, vm = _mem_kv(mem2d, row3(norm_mem), w_mem_kv, mem_tokens=mem_tokens)

    for layer in range(depth):
        kind, slot = layer % 2, layer // 2
        if kind == 0:
            q, k, v, qm, qx, kx = _in_proj(h, g_mix, w_fox, slot, layer,
                                           forget_params=forget_params, seq=seq,
                                           gather=(layer == 0))
            mix = _fox_attention(q, qx, k, kx, v, batch=batch, seq=seq)
        else:
            q, k, v, qm = _in_proj(h, g_mix, w_in_dil, slot, layer, rope_tabs=rope_tabs, seq=seq,
                                   gather=(layer == 0))
            mix = _dilated_attention(q, k, v, batch=batch, seq=seq)
        h = _layer_tail(h, mix, qm, km, vm, w_out, g_ffn, w_up_b, conv_w, conv_b3, w_down_b,
                        g_final, layer, seq=seq, mem_tokens=mem_tokens,
                        final_norm=(layer == depth - 1), gather_in=(layer == 0),
                        scatter_out=(layer == depth - 1))
    return h.reshape(batch, seq, D_MODEL)
```

```python
import functools

import numpy as np

import jax
import jax.numpy as jnp
from jax import lax
from jax.experimental import pallas as pl
from jax.experimental.pallas import tpu as pltpu

D_MODEL = 1024
HEAD_DIM = 64
N_MEM_HEADS = 4
N_MIX_HEADS = 12
D_MIX = N_MIX_HEADS * HEAD_DIM
D_MEMQ = N_MEM_HEADS * HEAD_DIM
D_FF = 2816
CONV_WIDTH = 3
ROPE_THETA = 10000.0
DIL_BLOCK = 128
NORM_EPS = 1e-6
NEG = -1e30
SCALE = HEAD_DIM ** -0.5
LOG2E = 1.4426950408889634

LANES = 128
N_PAIRS = D_MIX // LANES
VMEM_LIMIT = 48 * 1024 * 1024
TAIL_VMEM_LIMIT = 56 * 1024 * 1024

ROW_TILE = 512
FFN_TILE = 1024
FOX_Q_TILE = 2048
FOX_K_TILE = 512
FF_CHUNK = 256
DECAY_PIECES = 3
DECAY_LANES_PER_HEAD = 2 * DECAY_PIECES
PIECE_LANES = 16
DIL_CHUNK = 2048
DIL_RES = 16
DIL_RUN = ROW_TILE // DIL_RES
TILES_PER_CHUNK = DIL_CHUNK // ROW_TILE

_NT = (((1,), (1,)), ((), ()))


def _cparams(n_axes):
    return pltpu.CompilerParams(
        dimension_semantics=("arbitrary",) * n_axes,
        vmem_limit_bytes=VMEM_LIMIT)


def _rms(x, g):
    y = x * lax.rsqrt(jnp.mean(x * x, axis=-1, keepdims=True) + NORM_EPS)
    return y * g


def _lane_first_half():
    lane = lax.broadcasted_iota(jnp.int32, (1, LANES), 1)
    return lane < HEAD_DIM


def _split3(x):
    hi = x.astype(jnp.bfloat16)
    rem = x - hi.astype(jnp.float32)
    mid = rem.astype(jnp.bfloat16)
    lo = (rem - mid.astype(jnp.float32)).astype(jnp.bfloat16)
    return hi, mid, lo


def _in_proj_kernel(*refs, rope, forget, gather, tiles_per_seq):
    x_ref, g_ref, w_ref = refs[:3]
    pos = 3
    if forget:
        wm_ref, wf_ref, bf_ref, tri_ref, sel_ref, ones_ref = refs[pos:pos + 6]; pos += 6
    if rope:
        cos_ref, sin_ref = refs[pos:pos + 2]; pos += 2
    q_ref, k_ref, v_ref, qm_ref = refs[pos:pos + 4]; pos += 4
    if forget:
        qx_ref, kx_ref = refs[pos:pos + 2]; pos += 2
    if gather:
        xr_ref = refs[pos]; pos += 1
    if forget:
        carry_ref = refs[pos]

    if gather:
        x = jnp.concatenate([x_ref[:, r, :] for r in range(DIL_RES)], axis=0)
        xr_ref[...] = x
    else:
        x = x_ref[...]
    xn = _rms(x, g_ref[...]).astype(jnp.bfloat16)
    q_scale = SCALE * LOG2E

    def proj(index):
        width = D_MIX if index < 3 else D_MEMQ
        w = w_ref[:, index * D_MIX:index * D_MIX + width].astype(jnp.bfloat16)
        return jnp.dot(xn, w, preferred_element_type=jnp.float32)

    def rotate(t):
        lane = lax.broadcasted_iota(jnp.int32, (1, LANES), 1)
        first = (lane % HEAD_DIM) < (HEAD_DIM // 2)
        partner = jnp.where(first, pltpu.roll(t, LANES - HEAD_DIM // 2, 1),
                            pltpu.roll(t, HEAD_DIM // 2, 1))
        return t * cos_ref[...] + partner * sin_ref[...]

    if rope:
        q = proj(0)
        for p in range(N_PAIRS):
            cols = slice(p * LANES, (p + 1) * LANES)
            q_ref[:, cols] = (rotate(q[:, cols]) * q_scale).astype(jnp.bfloat16)
        k = proj(1)
        for p in range(N_PAIRS):
            cols = slice(p * LANES, (p + 1) * LANES)
            k_ref[:, cols] = rotate(k[:, cols]).astype(jnp.bfloat16)
        v_ref[...] = proj(2).astype(jnp.bfloat16)
        qm_ref[...] = (proj(3) * q_scale).astype(jnp.bfloat16)
        return

    @pl.when(pl.program_id(0) % tiles_per_seq == 0)
    def _():
        carry_ref[...] = jnp.zeros(carry_ref.shape, jnp.float32)

    f = jnp.dot(xn, wf_ref[...], preferred_element_type=jnp.float32) + bf_ref[...]
    q_ref[...] = (proj(0) * q_scale).astype(jnp.bfloat16)
    log_f = jnp.minimum(f, 0.0) - jnp.log1p(jnp.exp(-jnp.abs(f)))
    lane = lax.broadcasted_iota(jnp.int32, (1, LANES), 1)
    hi, mid, lo = _split3(log_f)
    packed = jnp.where(lane < PIECE_LANES, hi, jnp.where(lane < 2 * PIECE_LANES, mid, lo))
    c3 = carry_ref[0:1, :] + jnp.dot(tri_ref[...], packed, preferred_element_type=jnp.float32)
    carry_ref[0:1, :] = c3[c3.shape[0] - 1:, :]
    c = (c3 + pltpu.roll(c3, LANES - PIECE_LANES, 1)) + pltpu.roll(c3, LANES - 2 * PIECE_LANES, 1)
    k_ref[...] = proj(1).astype(jnp.bfloat16)
    hi, mid, lo = _split3(c * LOG2E)
    packed = jnp.where(lane < PIECE_LANES, hi,
                       jnp.where(lane < 2 * PIECE_LANES, pltpu.roll(mid, PIECE_LANES, 1),
                                 pltpu.roll(lo, 2 * PIECE_LANES, 1)))
    x_decay = ones_ref[...] + jnp.dot(packed, sel_ref[...], preferred_element_type=jnp.float32)
    qx_ref[...] = x_decay[:, :LANES].astype(jnp.bfloat16)
    kx_ref[...] = x_decay[:, LANES:].astype(jnp.bfloat16)
    v_ref[...] = proj(2).astype(jnp.bfloat16)
    qm_ref[...] = (jnp.dot(xn, wm_ref[...], preferred_element_type=jnp.float32) * q_scale
                   ).astype(jnp.bfloat16)


def _decay_selectors():
    sel = np.zeros((LANES, 2 * LANES), np.float32)
    ones = np.zeros((1, 2 * LANES), np.float32)
    for h in range(N_MIX_HEADS):
        base = h * DECAY_LANES_PER_HEAD
        for piece in range(DECAY_PIECES):
            sel[piece * PIECE_LANES + h, base + piece] = 1.0
            sel[piece * PIECE_LANES + h, LANES + base + DECAY_PIECES + piece] = -1.0
            ones[0, base + DECAY_PIECES + piece] = 1.0
            ones[0, LANES + base + piece] = 1.0
    return jnp.asarray(sel, jnp.bfloat16), jnp.asarray(ones)


def _tile_positions():
    rows = np.arange(ROW_TILE)
    return DIL_RES * (rows % DIL_RUN) + rows // DIL_RUN


def _forget_lanes(a):
    a = jnp.pad(a, ((0, 0), (0, PIECE_LANES - N_MIX_HEADS)))
    a = jnp.tile(a, (1, DECAY_PIECES))
    return jnp.pad(a, ((0, 0), (0, LANES - DECAY_PIECES * PIECE_LANES)))


def _layer_spec(stacked, index, **kwargs):
    return pl.BlockSpec((None,) + stacked.shape[1:], lambda *_: (index, 0, 0), **kwargs)


def _in_proj(h, g, w, slot, layer, forget_params=None, rope_tabs=None, *, seq, gather):
    t = h.shape[0]
    tm = ROW_TILE
    forget = forget_params is not None
    rope = rope_tabs is not None
    row = lambda i: (i, 0)
    fixed = lambda i: (0, 0)
    x_spec = pl.BlockSpec((tm, D_MODEL), row)
    if gather:
        h = h.reshape(t // DIL_RES, DIL_RES, D_MODEL)
        x_spec = pl.BlockSpec((DIL_RUN, DIL_RES, D_MODEL), lambda i: (i, 0, 0))
    tiles_per_seq = seq // tm
    once = dict(pipeline_mode=pl.Buffered(1))
    in_specs = [x_spec, _layer_spec(g, layer), _layer_spec(w, slot, **once)]
    args = [h, g, w]
    out_shape = [jax.ShapeDtypeStruct((t, D_MIX), jnp.bfloat16)] * 3
    out_shape += [jax.ShapeDtypeStruct((t, D_MEMQ), jnp.bfloat16)]
    out_specs = [pl.BlockSpec((tm, D_MIX), row)] * 3
    out_specs += [pl.BlockSpec((tm, D_MEMQ), row)]
    scratch = []
    if forget:
        w_m, w_f, b_f = forget_params
        sel, ones = _decay_selectors()
        pos = _tile_positions()
        tri = jnp.asarray(pos[None, :] <= pos[:, None], jnp.bfloat16)
        in_specs += [_layer_spec(w_m, slot), _layer_spec(w_f, slot), _layer_spec(b_f, slot),
                     pl.BlockSpec((tm, tm), fixed),
                     pl.BlockSpec((LANES, 2 * LANES), fixed),
                     pl.BlockSpec((1, 2 * LANES), fixed)]
        args += [w_m, w_f, b_f, tri, sel, ones]
        out_shape += [jax.ShapeDtypeStruct((t, LANES), jnp.bfloat16)] * 2
        out_specs += [pl.BlockSpec((tm, LANES), row)] * 2
        scratch.append(pltpu.VMEM((8, LANES), jnp.float32))
    if gather:
        out_shape += [jax.ShapeDtypeStruct((t, D_MODEL), jnp.float32)]
        out_specs += [pl.BlockSpec((tm, D_MODEL), row)]
    if rope:
        tab = lambda i: (i % tiles_per_seq, 0)
        in_specs += [pl.BlockSpec((tm, LANES), tab), pl.BlockSpec((tm, LANES), tab)]
        args += list(rope_tabs)
    return pl.pallas_call(
        functools.partial(_in_proj_kernel, rope=rope, forget=forget, gather=gather,
                          tiles_per_seq=tiles_per_seq),
        grid=(t // tm,),
        in_specs=in_specs, out_specs=out_specs, out_shape=out_shape,
        scratch_shapes=scratch,
        compiler_params=_cparams(1),
        name="in_proj_rope" if rope else "in_proj_fox",
    )(*args)


def _mem_kv_kernel(x_ref, g_ref, w_ref, km_ref, vm_ref):
    xn = _rms(x_ref[...], g_ref[...]).astype(jnp.bfloat16)
    kv = jnp.dot(xn, w_ref[...].astype(jnp.bfloat16), preferred_element_type=jnp.float32)
    km_ref[...] = kv[:, :D_MEMQ].astype(jnp.bfloat16)
    vm_ref[...] = kv[:, D_MEMQ:].astype(jnp.bfloat16)


def _mem_kv(mem2d, g, w, *, mem_tokens):
    rows = mem2d.shape[0]
    depth = w.shape[0]
    per_layer = lambda shape: pl.BlockSpec((None,) + shape, lambda l, b: (l, 0, 0))
    out_spec = pl.BlockSpec((None, mem_tokens, D_MEMQ), lambda l, b: (l, b, 0))
    return pl.pallas_call(
        _mem_kv_kernel,
        grid=(depth, rows // mem_tokens),
        in_specs=[pl.BlockSpec((mem_tokens, D_MODEL), lambda l, b: (b, 0)),
                  per_layer((1, D_MODEL)),
                  per_layer((D_MODEL, 2 * D_MEMQ))],
        out_specs=[out_spec, out_spec],
        out_shape=[jax.ShapeDtypeStruct((depth, rows, D_MEMQ), jnp.bfloat16)] * 2,
        compiler_params=_cparams(2),
        name="mem_kv",
    )(mem2d, g, w)


def _fox_kernel(q_ref, qx_ref, k_ref, kx_ref, v_ref, o_ref, m_scr, l_scr, acc_scr):
    tq, tk = FOX_Q_TILE, FOX_K_TILE
    n_blocks = tq // tk
    pair = pl.program_id(1)
    qi = pl.program_id(2)
    lane = lax.broadcasted_iota(jnp.int32, (1, LANES), 1)
    first = lane < HEAD_DIM
    head_lanes = (first, jnp.logical_not(first))
    q = q_ref[...]
    qx = qx_ref[...]
    zero = jnp.zeros_like(q)
    q_heads = []
    for hh in range(2):
        lo = (2 * pair + hh) * DECAY_LANES_PER_HEAD
        own_decay = (lane >= lo) & (lane < lo + DECAY_LANES_PER_HEAD)
        q_heads.append(jnp.concatenate([jnp.where(head_lanes[hh], q, zero),
                                        jnp.where(own_decay, qx, zero)], axis=1))

    m_scr[...] = jnp.full(m_scr.shape, NEG, jnp.float32)
    l_scr[...] = jnp.zeros(l_scr.shape, jnp.float32)
    acc_scr[...] = jnp.zeros(acc_scr.shape, jnp.float32)
    ones = jnp.ones((tk, LANES), jnp.bfloat16)

    def load_kv(j):
        start = pl.multiple_of(j * tk, tk)
        k = jnp.concatenate([k_ref[pl.ds(start, tk), :], kx_ref[pl.ds(start, tk), :]],
                            axis=1)
        v = jnp.concatenate([v_ref[pl.ds(start, tk), :], ones], axis=1)
        return k, v

    def chain(hh, r, k, v, diagonal):
        rows = slice(r * tk, (r + 1) * tk)
        s = lax.dot_general(q_heads[hh][rows], k, _NT, preferred_element_type=jnp.float32)
        if diagonal:
            row = lax.broadcasted_iota(jnp.int32, (tk, 1), 0)
            col = lax.broadcasted_iota(jnp.int32, (1, tk), 1)
            row_pos = DIL_RES * (row % DIL_RUN) + row // DIL_RUN
            col_pos = DIL_RES * (col % DIL_RUN) + col // DIL_RUN
            s = jnp.where(col_pos <= row_pos, s, NEG)
        m_prev = m_scr[hh, rows, :]
        m_new = jnp.maximum(m_prev, jnp.max(s, axis=-1, keepdims=True))
        alpha = jnp.exp2(m_prev - m_new)
        p = jnp.exp2(s - jnp.concatenate([m_new] * (tk // LANES), axis=1))
        pv = jnp.dot(p.astype(jnp.bfloat16), v, preferred_element_type=jnp.float32)
        l_scr[hh, rows, :] = alpha * l_scr[hh, rows, :] + pv[:, LANES:]
        acc_scr[hh, rows, :] = alpha * acc_scr[hh, rows, :] + pv[:, :LANES]
        m_scr[hh, rows, :] = m_new

    def body(j, carry):
        for half in range(2):
            k, v = load_kv(2 * j + half)
            for r in range(n_blocks):
                for hh in range(2):
                    chain(hh, r, k, v, False)
        return carry

    assert n_blocks % 2 == 0
    n_before = qi * n_blocks
    lax.fori_loop(0, n_before // 2, body, 0)
    for d in range(n_blocks):
        k, v = load_kv(n_before + d)
        for r in range(d, n_blocks):
            for hh in range(2):
                chain(hh, r, k, v, r == d)
    o0 = acc_scr[0] * (1.0 / l_scr[0])
    o1 = acc_scr[1] * (1.0 / l_scr[1])
    o_ref[...] = jnp.where(first, o0, o1).astype(o_ref.dtype)


def _fox_attention(q, qx, k, kx, v, *, batch, seq):
    tq = FOX_Q_TILE
    as_seq = lambda a: a.reshape(batch, seq, a.shape[-1])
    q_spec = pl.BlockSpec((None, tq, LANES), lambda b, p, i: (b, i, p))
    qx_spec = pl.BlockSpec((None, tq, LANES), lambda b, p, i: (b, i, 0))
    kv_spec = pl.BlockSpec((None, seq, LANES), lambda b, p, i: (b, 0, p))
    kx_spec = pl.BlockSpec((None, seq, LANES), lambda b, p, i: (b, 0, 0))
    out = pl.pallas_call(
        _fox_kernel,
        grid=(batch, N_PAIRS, seq // tq),
        in_specs=[q_spec, qx_spec, kv_spec, kx_spec, kv_spec],
        out_specs=q_spec,
        out_shape=jax.ShapeDtypeStruct((batch, seq, D_MIX), jnp.bfloat16),
        scratch_shapes=[pltpu.VMEM((2, tq, LANES), jnp.float32),
                        pltpu.VMEM((2, tq, LANES), jnp.float32),
                        pltpu.VMEM((2, tq, LANES), jnp.float32)],
        compiler_params=_cparams(3),
        name="fox_attention",
    )(as_seq(q), as_seq(qx), as_seq(k), as_seq(kx), as_seq(v))
    return out.reshape(batch * seq, D_MIX)


def _band_bias(n_q, entry_q, entry_k):
    a = np.arange(2 * n_q)[:, None] % n_q
    c = np.arange(2 * n_q)[None, :]
    dist = entry_q(a) - entry_k(c)
    ok = (dist >= 0) & (dist <= DIL_BLOCK)
    planes = [ok, ok & (c >= n_q)]
    return jnp.asarray(np.where(np.stack(planes), 0.0, NEG), jnp.float32)


def _dilated_masks():
    blk = DIL_BLOCK
    b16 = _band_bias(blk, lambda a: a + blk, lambda c: c)
    run4 = 32
    b4 = _band_bias(blk, lambda a: blk + 4 * (a % run4) + a // run4,
                    lambda c: (c // blk) * blk + 4 * ((c % blk) % run4) + (c % blk) // run4)
    run1 = blk // DIL_RES
    b1 = _band_bias(blk, lambda a: blk + DIL_RES * (a % run1) + a // run1,
                    lambda c: (c // blk) * blk + DIL_RES * ((c % blk) % run1) + (c % blk) // run1)
    return b16, b4, b1


def _dilated_kernel(q_ref, kp_ref, kc_ref, vp_ref, vc_ref, b16_ref, b4_ref, b1_ref,
                    o_ref, m_scr, l_scr, acc_scr):
    first = _lane_first_half()
    seq_start = jnp.where(pl.program_id(1) > 0, 0, 1)

    def gather(ref, slices):
        return jnp.concatenate([ref[s, :] for s in slices], axis=0)

    def chain(q_slices, key_parts, bias, first_branch):
        q = gather(q_ref, q_slices)
        zero = jnp.zeros_like(q)
        q2 = jnp.concatenate([jnp.where(first, q, zero), jnp.where(first, zero, q)], axis=0)
        k2 = jnp.concatenate([gather(kr, sl) for kr, _, sl in key_parts], axis=0)
        v2 = jnp.concatenate([gather(vr, sl) for _, vr, sl in key_parts], axis=0)
        n_q, n_k = q.shape[0], k2.shape[0]
        v2 = jnp.concatenate([v2, jnp.ones((n_k, LANES), jnp.bfloat16)], axis=1)
        s = lax.dot_general(q2, k2, _NT, preferred_element_type=jnp.float32) + bias
        m_cur = jnp.max(s, axis=-1, keepdims=True)
        if first_branch:
            m_new = jnp.broadcast_to(m_cur, (2 * n_q, LANES))
        else:
            m_prev = jnp.concatenate([gather(m_scr.at[hh], q_slices) for hh in range(2)],
                                     axis=0)
            m_new = jnp.maximum(m_prev, m_cur)
            alpha = jnp.exp2(m_prev - m_new)
        p = jnp.exp2(s - jnp.concatenate([m_new] * (n_k // LANES), axis=1))
        pv = jnp.dot(p.astype(jnp.bfloat16), v2, preferred_element_type=jnp.float32)
        num, den = pv[:, :LANES], pv[:, LANES:]
        if not first_branch:
            l_prev = jnp.concatenate([gather(l_scr.at[hh], q_slices) for hh in range(2)],
                                     axis=0)
            a_prev = jnp.concatenate([gather(acc_scr.at[hh], q_slices) for hh in range(2)],
                                     axis=0)
            den = alpha * l_prev + den
            num = alpha * a_prev + num
        for hh in range(2):
            off = hh * n_q
            for sl in q_slices:
                n = sl.stop - sl.start
                m_scr[hh, sl, :] = m_new[off:off + n]
                l_scr[hh, sl, :] = den[off:off + n]
                acc_scr[hh, sl, :] = num[off:off + n]
                off += n

    def run(tile, r, lo, n):
        start = tile * ROW_TILE + r * DIL_RUN + lo
        return slice(start, start + n)

    def block(cur, prv, prev_chunk, bias_ref, first_branch):
        if prev_chunk:
            parts = [(kp_ref, vp_ref, prv), (kc_ref, vc_ref, cur)]
            bias = bias_ref[seq_start]
        else:
            parts = [(kc_ref, vc_ref, prv), (kc_ref, vc_ref, cur)]
            bias = bias_ref[0]
        chain(cur, parts, bias, first_branch)

    last = TILES_PER_CHUNK - 1
    for r in range(DIL_RES):
        rows = [run(t, r, 0, DIL_RUN) for t in range(TILES_PER_CHUNK)]
        block(rows, rows, True, b16_ref, True)
    for r4 in range(4):
        for t in range(TILES_PER_CHUNK):
            cur = [run(t, r4 + 4 * u, 0, DIL_RUN) for u in range(4)]
            prv = [run(t - 1 if t else last, r4 + 4 * u, 0, DIL_RUN) for u in range(4)]
            block(cur, prv, t == 0, b4_ref, False)
    quarter = DIL_BLOCK // DIL_RES
    per_tile = DIL_RUN // quarter
    for t in range(TILES_PER_CHUNK):
        for qt in range(per_tile):
            cur = [run(t, r, qt * quarter, quarter) for r in range(DIL_RES)]
            if qt:
                prv = [run(t, r, (qt - 1) * quarter, quarter) for r in range(DIL_RES)]
            else:
                prv = [run(t - 1 if t else last, r, DIL_RUN - quarter, quarter)
                       for r in range(DIL_RES)]
            block(cur, prv, t == 0 and qt == 0, b1_ref, False)

    o0 = acc_scr[0] * (1.0 / l_scr[0])
    o1 = acc_scr[1] * (1.0 / l_scr[1])
    o_ref[...] = jnp.where(first, o0, o1).astype(o_ref.dtype)


def _dilated_attention(q, k, v, *, batch, seq):
    n_chunks = seq // DIL_CHUNK
    flat = lambda a: a.reshape(batch * n_chunks, DIL_CHUNK, D_MIX)
    masks = _dilated_masks()
    cur = lambda b, c, p: (b * n_chunks + c, 0, p)
    prev = lambda b, c, p: (b * n_chunks + jnp.maximum(c - 1, 0), 0, p)
    spec = lambda f: pl.BlockSpec((None, DIL_CHUNK, LANES), f)
    mask_specs = [pl.BlockSpec(m.shape, lambda b, c, p: (0, 0, 0)) for m in masks]
    out = pl.pallas_call(
        _dilated_kernel,
        grid=(batch, n_chunks, N_PAIRS),
        in_specs=[spec(cur), spec(prev), spec(cur), spec(prev), spec(cur)] + mask_specs,
        out_specs=spec(cur),
        out_shape=jax.ShapeDtypeStruct((batch * n_chunks, DIL_CHUNK, D_MIX), jnp.bfloat16),
        scratch_shapes=[pltpu.VMEM((2, DIL_CHUNK, LANES), jnp.float32)] * 3,
        compiler_params=_cparams(3),
        name="dilated_attention",
    )(flat(q), flat(k), flat(k), flat(v), flat(v), *masks)
    return out.reshape(batch * seq, D_MIX)


def _mem_attention(qm_ref, km_ref, vm_ref):
    first = _lane_first_half()
    qm = qm_ref[...]
    ones = jnp.ones((km_ref.shape[0], LANES), jnp.bfloat16)
    pairs = []
    for p in range(D_MEMQ // LANES):
        cols = slice(p * LANES, (p + 1) * LANES)
        qp = qm[:, cols]
        n = qp.shape[0]
        zero = jnp.zeros_like(qp)
        q2 = jnp.concatenate([jnp.where(first, qp, zero), jnp.where(first, zero, qp)],
                             axis=0)
        v2 = jnp.concatenate([vm_ref[:, cols], ones], axis=1)
        s = lax.dot_general(q2, km_ref[:, cols], _NT, preferred_element_type=jnp.float32)
        e = jnp.exp2(s - jnp.max(s, axis=-1, keepdims=True))
        pv = jnp.dot(e.astype(jnp.bfloat16), v2, preferred_element_type=jnp.float32)
        o = pv[:, :LANES] * (1.0 / pv[:, LANES:])
        pairs.append(jnp.where(first, o[:n], o[n:]).astype(jnp.bfloat16))
    return jnp.concatenate(pairs, axis=-1)


def _layer_tail_kernel(*refs, tiles_per_seq, final_norm, gather_in, scatter_out):
    (h_ref, mix_ref, qm_ref, km_ref, vm_ref, wo_ref, g_ref, wup_ref, cw_ref, cb_ref,
     wdown_ref, gfin_ref, out_ref, carry_ref, act_ref) = refs
    tm = FFN_TILE
    halo = 8
    n_chunks = D_FF // FF_CHUNK
    i = pl.program_id(0)

    @pl.when(i % tiles_per_seq == 0)
    def _():
        carry_ref[...] = jnp.zeros(carry_ref.shape, jnp.float32)

    mem_out = _mem_attention(qm_ref, km_ref, vm_ref)
    y = jnp.dot(mix_ref[...], wo_ref[:D_MIX, :].astype(jnp.bfloat16),
                preferred_element_type=jnp.float32)
    y = y + jnp.dot(mem_out, wo_ref[D_MIX:, :].astype(jnp.bfloat16),
                    preferred_element_type=jnp.float32)
    n_tiles = tm // ROW_TILE
    if gather_in:
        h = jnp.concatenate([h_ref[t * DIL_RUN:(t + 1) * DIL_RUN, r, :]
                             for t in range(n_tiles) for r in range(DIL_RES)], axis=0) + y
    else:
        h = h_ref[...] + y

    xn = _rms(h, g_ref[...]).astype(jnp.bfloat16)
    run_row = lax.broadcasted_iota(jnp.int32, (DIL_RUN, FF_CHUNK), 0)

    def history(u, cols):
        def shifted(run, before):
            return jnp.where(run_row == 0, before, pltpu.roll(run, 1, 0))

        last15 = carry_ref[halo - 1:halo, cols]
        last14 = carry_ref[2 * halo - 1:2 * halo, cols]
        back1, back2 = [], []
        for t in range(n_tiles):
            ut = u[t * ROW_TILE:(t + 1) * ROW_TILE]
            run14 = ut[14 * DIL_RUN:15 * DIL_RUN]
            run15 = ut[15 * DIL_RUN:]
            s15, s14 = shifted(run15, last15), shifted(run14, last14)
            back1 += [s15, ut[:15 * DIL_RUN]]
            back2 += [s14, s15, ut[:14 * DIL_RUN]]
            last15, last14 = run15[DIL_RUN - 1:], run14[DIL_RUN - 1:]
        carry_ref[0:halo, cols] = u[tm - halo:tm, :]
        carry_ref[halo:2 * halo, cols] = u[tm - DIL_RUN - halo:tm - DIL_RUN, :]
        return jnp.concatenate(back1, axis=0), jnp.concatenate(back2, axis=0)

    def conv(u, cols):
        back1, back2 = history(u, cols)
        w = cw_ref[:, cols]
        c = cb_ref[:, cols] + w[0:1, :] * back2
        c = c + w[1:2, :] * back1
        return c + w[2:3, :] * u

    def val_cols(j):
        return slice(j * FF_CHUNK, (j + 1) * FF_CHUNK)

    def gate_cols(j):
        return slice(D_FF + j * FF_CHUNK, D_FF + (j + 1) * FF_CHUNK)

    def up(j):
        return (jnp.dot(xn, wup_ref[:, val_cols(j)], preferred_element_type=jnp.float32),
                jnp.dot(xn, wup_ref[:, gate_cols(j)], preferred_element_type=jnp.float32))

    u = up(0)
    for j in range(n_chunks):
        u_next = up(j + 1) if j + 1 < n_chunks else None
        val = conv(u[0], val_cols(j))
        gate = conv(u[1], gate_cols(j))
        act_ref[:, val_cols(j)] = (gate * (1.0 / (1.0 + jnp.exp(-gate))) * val
                                   ).astype(jnp.bfloat16)
        u = u_next
    y = h + jnp.dot(act_ref[...], wdown_ref[...], preferred_element_type=jnp.float32)
    if final_norm:
        y = _rms(y, gfin_ref[...])
    if scatter_out:
        for r in range(DIL_RES):
            out_ref[:, r, :] = jnp.concatenate(
                [y[t * ROW_TILE + r * DIL_RUN:t * ROW_TILE + (r + 1) * DIL_RUN]
                 for t in range(n_tiles)], axis=0)
    else:
        out_ref[...] = y


def _layer_tail(h, mix, qm, km, vm, w_out, g, w_up, conv_w, conv_b, w_down, g_final, layer, *,
                seq, mem_tokens, final_norm, gather_in, scatter_out):
    t = h.shape[0]
    tm = FFN_TILE
    tiles_per_seq = seq // tm
    row = lambda i: (i, 0)
    once = dict(pipeline_mode=pl.Buffered(1))
    mem_spec = pl.BlockSpec((None, mem_tokens, D_MEMQ),
                            lambda i: (layer, i // tiles_per_seq, 0))
    plain_spec = pl.BlockSpec((tm, D_MODEL), row)
    view_shape = (t // DIL_RES, DIL_RES, D_MODEL)
    view_spec = pl.BlockSpec((tm // DIL_RES, DIL_RES, D_MODEL), lambda i: (i, 0, 0))
    scratch = [pltpu.VMEM((16, 2 * D_FF), jnp.float32),
               pltpu.VMEM((tm, D_FF), jnp.bfloat16)]
    if gather_in:
        h = h.reshape(view_shape)
    out = pl.pallas_call(
        functools.partial(_layer_tail_kernel, tiles_per_seq=tiles_per_seq, final_norm=final_norm,
                          gather_in=gather_in, scatter_out=scatter_out),
        grid=(t // tm,),
        in_specs=[view_spec if gather_in else plain_spec,
                  pl.BlockSpec((tm, D_MIX), row),
                  pl.BlockSpec((tm, D_MEMQ), row),
                  mem_spec, mem_spec,
                  _layer_spec(w_out, layer, **once),
                  _layer_spec(g, layer),
                  _layer_spec(w_up, layer, **once),
                  _layer_spec(conv_w, layer),
                  _layer_spec(conv_b, layer),
                  _layer_spec(w_down, layer, **once),
                  pl.BlockSpec((1, D_MODEL), lambda i: (0, 0))],
        out_specs=view_spec if scatter_out else plain_spec,
        out_shape=jax.ShapeDtypeStruct(view_shape if scatter_out else (t, D_MODEL), jnp.float32),
        scratch_shapes=scratch,
        compiler_params=pltpu.CompilerParams(dimension_semantics=("arbitrary",),
                                             vmem_limit_bytes=TAIL_VMEM_LIMIT),
        name="layer_tail" + ("_in" if gather_in else "") + ("_out" if scatter_out else ""),
    )(h, mix, qm, km, vm, w_out, g, w_up, conv_w, conv_b, w_down, g_final)
    return out.reshape(t, D_MODEL)


def _rope_tables(seq):
    def residue_major(tab):
        tab = tab.reshape(seq // ROW_TILE, DIL_RUN, DIL_RES, LANES)
        return tab.transpose(0, 2, 1, 3).reshape(seq, LANES)

    with jax.ensure_compile_time_eval():
        inv = 1.0 / (ROPE_THETA ** (jnp.arange(0, HEAD_DIM, 2, dtype=jnp.float32) / HEAD_DIM))
        ang = jnp.arange(seq, dtype=jnp.float32)[:, None] * inv[None, :]
        cos, sin = jnp.cos(ang), jnp.sin(ang)
        reps = LANES // HEAD_DIM
        cos_t = jnp.tile(jnp.concatenate([cos, cos], axis=-1), (1, reps))
        sin_t = jnp.tile(jnp.concatenate([-sin, sin], axis=-1), (1, reps))
        return residue_major(cos_t), residue_major(sin_t)


def kernel(x, mem, norm_mix, norm_mem, norm_ffn, w_in_fox, b_forget, w_in_dil,
           w_mem_kv, w_out, w_up, conv_w, conv_b, w_down, norm_final):
    batch, seq, _ = x.shape
    assert seq % DIL_CHUNK == 0 and seq % FOX_Q_TILE == 0 and FOX_K_TILE == ROW_TILE
    mem_tokens = mem.shape[1]
    depth = norm_mix.shape[0]
    bf = lambda a: a.astype(jnp.bfloat16)
    row3 = lambda a: a.reshape(a.shape[0], 1, a.shape[1])
    h = x.reshape(batch * seq, D_MODEL)
    mem2d = mem.reshape(batch * mem_tokens, D_MODEL)
    rope_tabs = _rope_tables(seq)
    g_mix, g_ffn, g_final = row3(norm_mix), row3(norm_ffn), norm_final.reshape(1, D_MODEL)

    w_up_b, w_down_b = bf(w_up), bf(w_down)
    conv_b3 = row3(conv_b)
    n_fox = w_in_fox.shape[0]
    w_fox = w_in_fox[:, :, :3 * D_MIX]
    tail = w_in_fox[:, :, 3 * D_MIX:]
    forget_params = (
        bf(tail[:, :, N_MIX_HEADS:]),
        bf(_forget_lanes(tail[:, :, :N_MIX_HEADS].reshape(n_fox * D_MODEL, N_MIX_HEADS))
           ).reshape(n_fox, D_MODEL, LANES),
        _forget_lanes(b_forget).reshape(n_fox, 1, LANES))
    km, vm = _mem_kv(mem2d, row3(norm_mem), w_mem_kv, mem_tokens=mem_tokens)

    for layer in range(depth):
        kind, slot = layer % 2, layer // 2
        first = layer == 0
        if kind == 0:
            outs = _in_proj(h, g_mix, w_fox, slot, layer, forget_params=forget_params,
                            seq=seq, gather=first)
            q, k, v, qm, qx, kx = outs[:6]
            mix = _fox_attention(q, qx, k, kx, v, batch=batch, seq=seq)
        else:
            outs = _in_proj(h, g_mix, w_in_dil, slot, layer, rope_tabs=rope_tabs, seq=seq,
                            gather=first)
            q, k, v, qm = outs[:4]
            mix = _dilated_attention(q, k, v, batch=batch, seq=seq)
        if first:
            h = outs[-1]
        h = _layer_tail(h, mix, qm, km, vm, w_out, g_ffn, w_up_b, conv_w, conv_b3, w_down_b,
                        g_final, layer, seq=seq, mem_tokens=mem_tokens,
                        final_norm=(layer == depth - 1), gather_in=False,
                        scatter_out=(layer == depth - 1))
    return h.reshape(batch, seq, D_MODEL)
```

```python
import functools

import numpy as np

import jax
import jax.numpy as jnp
from jax import lax
from jax.experimental import pallas as pl
from jax.experimental.pallas import tpu as pltpu

D_MODEL = 1024
HEAD_DIM = 64
N_MEM_HEADS = 4
N_MIX_HEADS = 12
D_MIX = N_MIX_HEADS * HEAD_DIM
D_MEMQ = N_MEM_HEADS * HEAD_DIM
D_FF = 2816
CONV_WIDTH = 3
ROPE_THETA = 10000.0
DIL_BLOCK = 128
NORM_EPS = 1e-6
NEG = -1e30
SCALE = HEAD_DIM ** -0.5
LOG2E = 1.4426950408889634

LANES = 128
N_PAIRS = D_MIX // LANES
VMEM_LIMIT = 48 * 1024 * 1024
TAIL_VMEM_LIMIT = 56 * 1024 * 1024

ROW_TILE = 512
FFN_TILE = 1024
FOX_Q_TILE = 2048
FOX_K_TILE = 512
FF_CHUNK = 256
DECAY_PIECES = 3
DECAY_LANES_PER_HEAD = 2 * DECAY_PIECES
PIECE_LANES = 16
DIL_CHUNK = 4096
DIL_RES = 16
DIL_RUN = ROW_TILE // DIL_RES
TILES_PER_CHUNK = DIL_CHUNK // ROW_TILE

_NT = (((1,), (1,)), ((), ()))


def _cparams(n_axes):
    return pltpu.CompilerParams(
        dimension_semantics=("arbitrary",) * n_axes,
        vmem_limit_bytes=VMEM_LIMIT)


def _rms(x, g):
    y = x * lax.rsqrt(jnp.mean(x * x, axis=-1, keepdims=True) + NORM_EPS)
    return y * g


def _lane_first_half():
    lane = lax.broadcasted_iota(jnp.int32, (1, LANES), 1)
    return lane < HEAD_DIM


def _split3(x):
    hi = x.astype(jnp.bfloat16)
    rem = x - hi.astype(jnp.float32)
    mid = rem.astype(jnp.bfloat16)
    lo = (rem - mid.astype(jnp.float32)).astype(jnp.bfloat16)
    return hi, mid, lo


def _in_proj_kernel(*refs, rope, forget, gather, tiles_per_seq):
    x_ref, g_ref, w_ref = refs[:3]
    pos = 3
    if forget:
        wm_ref, wf_ref, bf_ref, tri_ref, sel_ref, ones_ref = refs[pos:pos + 6]; pos += 6
    if rope:
        cos_ref, sin_ref = refs[pos:pos + 2]; pos += 2
    q_ref, k_ref, v_ref, qm_ref = refs[pos:pos + 4]; pos += 4
    if forget:
        qx_ref, kx_ref = refs[pos:pos + 2]; pos += 2
    if gather:
        xr_ref = refs[pos]; pos += 1
    if forget:
        carry_ref = refs[pos]

    if gather:
        x = jnp.concatenate([x_ref[:, r, :] for r in range(DIL_RES)], axis=0)
        xr_ref[...] = x
    else:
        x = x_ref[...]
    xn = _rms(x, g_ref[...]).astype(jnp.bfloat16)
    q_scale = SCALE * LOG2E

    def proj(index):
        width = D_MIX if index < 3 else D_MEMQ
        w = w_ref[:, index * D_MIX:index * D_MIX + width].astype(jnp.bfloat16)
        return jnp.dot(xn, w, preferred_element_type=jnp.float32)

    def rotate(t):
        lane = lax.broadcasted_iota(jnp.int32, (1, LANES), 1)
        first = (lane % HEAD_DIM) < (HEAD_DIM // 2)
        partner = jnp.where(first, pltpu.roll(t, LANES - HEAD_DIM // 2, 1),
                            pltpu.roll(t, HEAD_DIM // 2, 1))
        return t * cos_ref[...] + partner * sin_ref[...]

    if rope:
        q = proj(0)
        for p in range(N_PAIRS):
            cols = slice(p * LANES, (p + 1) * LANES)
            q_ref[:, cols] = (rotate(q[:, cols]) * q_scale).astype(jnp.bfloat16)
        k = proj(1)
        for p in range(N_PAIRS):
            cols = slice(p * LANES, (p + 1) * LANES)
            k_ref[:, cols] = rotate(k[:, cols]).astype(jnp.bfloat16)
        v_ref[...] = proj(2).astype(jnp.bfloat16)
        qm_ref[...] = (proj(3) * q_scale).astype(jnp.bfloat16)
        return

    @pl.when(pl.program_id(0) % tiles_per_seq == 0)
    def _():
        carry_ref[...] = jnp.zeros(carry_ref.shape, jnp.float32)

    f = jnp.dot(xn, wf_ref[...], preferred_element_type=jnp.float32) + bf_ref[...]
    q_ref[...] = (proj(0) * q_scale).astype(jnp.bfloat16)
    log_f = jnp.minimum(f, 0.0) - jnp.log1p(jnp.exp(-jnp.abs(f)))
    lane = lax.broadcasted_iota(jnp.int32, (1, LANES), 1)
    hi, mid, lo = _split3(log_f)
    packed = jnp.where(lane < PIECE_LANES, hi, jnp.where(lane < 2 * PIECE_LANES, mid, lo))
    c3 = carry_ref[0:1, :] + jnp.dot(tri_ref[...], packed, preferred_element_type=jnp.float32)
    carry_ref[0:1, :] = c3[c3.shape[0] - 1:, :]
    c = (c3 + pltpu.roll(c3, LANES - PIECE_LANES, 1)) + pltpu.roll(c3, LANES - 2 * PIECE_LANES, 1)
    k_ref[...] = proj(1).astype(jnp.bfloat16)
    hi, mid, lo = _split3(c * LOG2E)
    packed = jnp.where(lane < PIECE_LANES, hi,
                       jnp.where(lane < 2 * PIECE_LANES, pltpu.roll(mid, PIECE_LANES, 1),
                                 pltpu.roll(lo, 2 * PIECE_LANES, 1)))
    x_decay = ones_ref[...] + jnp.dot(packed, sel_ref[...], preferred_element_type=jnp.float32)
    qx_ref[...] = x_decay[:, :LANES].astype(jnp.bfloat16)
    kx_ref[...] = x_decay[:, LANES:].astype(jnp.bfloat16)
    v_ref[...] = proj(2).astype(jnp.bfloat16)
    qm_ref[...] = (jnp.dot(xn, wm_ref[...], preferred_element_type=jnp.float32) * q_scale
                   ).astype(jnp.bfloat16)


def _decay_selectors():
    sel = np.zeros((LANES, 2 * LANES), np.float32)
    ones = np.zeros((1, 2 * LANES), np.float32)
    for h in range(N_MIX_HEADS):
        base = h * DECAY_LANES_PER_HEAD
        for piece in range(DECAY_PIECES):
            sel[piece * PIECE_LANES + h, base + piece] = 1.0
            sel[piece * PIECE_LANES + h, LANES + base + DECAY_PIECES + piece] = -1.0
            ones[0, base + DECAY_PIECES + piece] = 1.0
            ones[0, LANES + base + piece] = 1.0
    return jnp.asarray(sel, jnp.bfloat16), jnp.asarray(ones)


def _tile_positions():
    rows = np.arange(ROW_TILE)
    return DIL_RES * (rows % DIL_RUN) + rows // DIL_RUN


def _forget_lanes(a):
    a = jnp.pad(a, ((0, 0), (0, PIECE_LANES - N_MIX_HEADS)))
    a = jnp.tile(a, (1, DECAY_PIECES))
    return jnp.pad(a, ((0, 0), (0, LANES - DECAY_PIECES * PIECE_LANES)))


def _layer_spec(stacked, index, **kwargs):
    return pl.BlockSpec((None,) + stacked.shape[1:], lambda *_: (index, 0, 0), **kwargs)


def _in_proj(h, g, w, slot, layer, forget_params=None, rope_tabs=None, *, seq, gather):
    t = h.shape[0]
    tm = ROW_TILE
    forget = forget_params is not None
    rope = rope_tabs is not None
    row = lambda i: (i, 0)
    fixed = lambda i: (0, 0)
    x_spec = pl.BlockSpec((tm, D_MODEL), row)
    if gather:
        h = h.reshape(t // DIL_RES, DIL_RES, D_MODEL)
        x_spec = pl.BlockSpec((DIL_RUN, DIL_RES, D_MODEL), lambda i: (i, 0, 0))
    tiles_per_seq = seq // tm
    once = dict(pipeline_mode=pl.Buffered(1))
    in_specs = [x_spec, _layer_spec(g, layer), _layer_spec(w, slot, **once)]
    args = [h, g, w]
    out_shape = [jax.ShapeDtypeStruct((t, D_MIX), jnp.bfloat16)] * 3
    out_shape += [jax.ShapeDtypeStruct((t, D_MEMQ), jnp.bfloat16)]
    out_specs = [pl.BlockSpec((tm, D_MIX), row)] * 3
    out_specs += [pl.BlockSpec((tm, D_MEMQ), row)]
    scratch = []
    if forget:
        w_m, w_f, b_f = forget_params
        sel, ones = _decay_selectors()
        pos = _tile_positions()
        tri = jnp.asarray(pos[None, :] <= pos[:, None], jnp.bfloat16)
        in_specs += [_layer_spec(w_m, slot), _layer_spec(w_f, slot), _layer_spec(b_f, slot),
                     pl.BlockSpec((tm, tm), fixed),
                     pl.BlockSpec((LANES, 2 * LANES), fixed),
                     pl.BlockSpec((1, 2 * LANES), fixed)]
        args += [w_m, w_f, b_f, tri, sel, ones]
        out_shape += [jax.ShapeDtypeStruct((t, LANES), jnp.bfloat16)] * 2
        out_specs += [pl.BlockSpec((tm, LANES), row)] * 2
        scratch.append(pltpu.VMEM((8, LANES), jnp.float32))
    if gather:
        out_shape += [jax.ShapeDtypeStruct((t, D_MODEL), jnp.float32)]
        out_specs += [pl.BlockSpec((tm, D_MODEL), row)]
    if rope:
        tab = lambda i: (i % tiles_per_seq, 0)
        in_specs += [pl.BlockSpec((tm, LANES), tab), pl.BlockSpec((tm, LANES), tab)]
        args += list(rope_tabs)
    return pl.pallas_call(
        functools.partial(_in_proj_kernel, rope=rope, forget=forget, gather=gather,
                          tiles_per_seq=tiles_per_seq),
        grid=(t // tm,),
        in_specs=in_specs, out_specs=out_specs, out_shape=out_shape,
        scratch_shapes=scratch,
        compiler_params=_cparams(1),
        name="in_proj_rope" if rope else "in_proj_fox",
    )(*args)


def _mem_kv_kernel(x_ref, g_ref, w_ref, km_ref, vm_ref):
    xn = _rms(x_ref[...], g_ref[...]).astype(jnp.bfloat16)
    kv = jnp.dot(xn, w_ref[...].astype(jnp.bfloat16), preferred_element_type=jnp.float32)
    km_ref[...] = kv[:, :D_MEMQ].astype(jnp.bfloat16)
    vm_ref[...] = kv[:, D_MEMQ:].astype(jnp.bfloat16)


def _mem_kv(mem2d, g, w, *, mem_tokens):
    rows = mem2d.shape[0]
    depth = w.shape[0]
    per_layer = lambda shape: pl.BlockSpec((None,) + shape, lambda l, b: (l, 0, 0))
    out_spec = pl.BlockSpec((None, mem_tokens, D_MEMQ), lambda l, b: (l, b, 0))
    return pl.pallas_call(
        _mem_kv_kernel,
        grid=(depth, rows // mem_tokens),
        in_specs=[pl.BlockSpec((mem_tokens, D_MODEL), lambda l, b: (b, 0)),
                  per_layer((1, D_MODEL)),
                  per_layer((D_MODEL, 2 * D_MEMQ))],
        out_specs=[out_spec, out_spec],
        out_shape=[jax.ShapeDtypeStruct((depth, rows, D_MEMQ), jnp.bfloat16)] * 2,
        compiler_params=_cparams(2),
        name="mem_kv",
    )(mem2d, g, w)


def _fox_kernel(q_ref, qx_ref, k_ref, kx_ref, v_ref, o_ref, m_scr, l_scr, acc_scr):
    tq, tk = FOX_Q_TILE, FOX_K_TILE
    n_blocks = tq // tk
    pair = pl.program_id(1)
    qi = pl.program_id(2)
    lane = lax.broadcasted_iota(jnp.int32, (1, LANES), 1)
    first = lane < HEAD_DIM
    head_lanes = (first, jnp.logical_not(first))
    q = q_ref[...]
    qx = qx_ref[...]
    zero = jnp.zeros_like(q)
    q_heads = []
    for hh in range(2):
        lo = (2 * pair + hh) * DECAY_LANES_PER_HEAD
        own_decay = (lane >= lo) & (lane < lo + DECAY_LANES_PER_HEAD)
        q_heads.append(jnp.concatenate([jnp.where(head_lanes[hh], q, zero),
                                        jnp.where(own_decay, qx, zero)], axis=1))

    m_scr[...] = jnp.full(m_scr.shape, NEG, jnp.float32)
    l_scr[...] = jnp.zeros(l_scr.shape, jnp.float32)
    acc_scr[...] = jnp.zeros(acc_scr.shape, jnp.float32)
    ones = jnp.ones((tk, LANES), jnp.bfloat16)

    def load_kv(j):
        start = pl.multiple_of(j * tk, tk)
        k = jnp.concatenate([k_ref[pl.ds(start, tk), :], kx_ref[pl.ds(start, tk), :]],
                            axis=1)
        v = jnp.concatenate([v_ref[pl.ds(start, tk), :], ones], axis=1)
        return k, v

    def chain(hh, r, k, v, diagonal):
        rows = slice(r * tk, (r + 1) * tk)
        s = lax.dot_general(q_heads[hh][rows], k, _NT, preferred_element_type=jnp.float32)
        if diagonal:
            row = lax.broadcasted_iota(jnp.int32, (tk, 1), 0)
            col = lax.broadcasted_iota(jnp.int32, (1, tk), 1)
            row_pos = DIL_RES * (row % DIL_RUN) + row // DIL_RUN
            col_pos = DIL_RES * (col % DIL_RUN) + col // DIL_RUN
            s = jnp.where(col_pos <= row_pos, s, NEG)
        m_prev = m_scr[hh, rows, :]
        m_new = jnp.maximum(m_prev, jnp.max(s, axis=-1, keepdims=True))
        alpha = jnp.exp2(m_prev - m_new)
        p = jnp.exp2(s - jnp.concatenate([m_new] * (tk // LANES), axis=1))
        pv = jnp.dot(p.astype(jnp.bfloat16), v, preferred_element_type=jnp.float32)
        l_scr[hh, rows, :] = alpha * l_scr[hh, rows, :] + pv[:, LANES:]
        acc_scr[hh, rows, :] = alpha * acc_scr[hh, rows, :] + pv[:, :LANES]
        m_scr[hh, rows, :] = m_new

    def body(j, carry):
        for half in range(2):
            k, v = load_kv(2 * j + half)
            for r in range(n_blocks):
                for hh in range(2):
                    chain(hh, r, k, v, False)
        return carry

    assert n_blocks % 2 == 0
    n_before = qi * n_blocks
    lax.fori_loop(0, n_before // 2, body, 0)
    for d in range(n_blocks):
        k, v = load_kv(n_before + d)
        for r in range(d, n_blocks):
            for hh in range(2):
                chain(hh, r, k, v, r == d)
    o0 = acc_scr[0] * (1.0 / l_scr[0])
    o1 = acc_scr[1] * (1.0 / l_scr[1])
    o_ref[...] = jnp.where(first, o0, o1).astype(o_ref.dtype)


def _fox_attention(q, qx, k, kx, v, *, batch, seq):
    tq = FOX_Q_TILE
    as_seq = lambda a: a.reshape(batch, seq, a.shape[-1])
    q_spec = pl.BlockSpec((None, tq, LANES), lambda b, p, i: (b, i, p))
    qx_spec = pl.BlockSpec((None, tq, LANES), lambda b, p, i: (b, i, 0))
    kv_spec = pl.BlockSpec((None, seq, LANES), lambda b, p, i: (b, 0, p))
    kx_spec = pl.BlockSpec((None, seq, LANES), lambda b, p, i: (b, 0, 0))
    out = pl.pallas_call(
        _fox_kernel,
        grid=(batch, N_PAIRS, seq // tq),
        in_specs=[q_spec, qx_spec, kv_spec, kx_spec, kv_spec],
        out_specs=q_spec,
        out_shape=jax.ShapeDtypeStruct((batch, seq, D_MIX), jnp.bfloat16),
        scratch_shapes=[pltpu.VMEM((2, tq, LANES), jnp.float32),
                        pltpu.VMEM((2, tq, LANES), jnp.float32),
                        pltpu.VMEM((2, tq, LANES), jnp.float32)],
        compiler_params=_cparams(3),
        name="fox_attention",
    )(as_seq(q), as_seq(qx), as_seq(k), as_seq(kx), as_seq(v))
    return out.reshape(batch * seq, D_MIX)


def _band_bias(n_q, entry_q, entry_k):
    a = np.arange(2 * n_q)[:, None] % n_q
    c = np.arange(2 * n_q)[None, :]
    dist = entry_q(a) - entry_k(c)
    ok = (dist >= 0) & (dist <= DIL_BLOCK)
    planes = [ok, ok & (c >= n_q)]
    return jnp.asarray(np.where(np.stack(planes), 0.0, NEG), jnp.float32)


def _dilated_masks():
    blk = DIL_BLOCK
    b16 = _band_bias(blk, lambda a: a + blk, lambda c: c)
    run4 = 32
    b4 = _band_bias(blk, lambda a: blk + 4 * (a % run4) + a // run4,
                    lambda c: (c // blk) * blk + 4 * ((c % blk) % run4) + (c % blk) // run4)
    run1 = blk // DIL_RES
    b1 = _band_bias(blk, lambda a: blk + DIL_RES * (a % run1) + a // run1,
                    lambda c: (c // blk) * blk + DIL_RES * ((c % blk) % run1) + (c % blk) // run1)
    return b16, b4, b1


def _dilated_kernel(q_ref, kp_ref, kc_ref, vp_ref, vc_ref, b16_ref, b4_ref, b1_ref,
                    o_ref, m_scr, l_scr, acc_scr):
    first = _lane_first_half()
    seq_start = jnp.where(pl.program_id(1) > 0, 0, 1)

    def gather(ref, slices):
        return jnp.concatenate([ref[s, :] for s in slices], axis=0)

    def chain(q_slices, key_parts, bias, first_branch):
        q = gather(q_ref, q_slices)
        zero = jnp.zeros_like(q)
        q2 = jnp.concatenate([jnp.where(first, q, zero), jnp.where(first, zero, q)], axis=0)
        k2 = jnp.concatenate([gather(kr, sl) for kr, _, sl in key_parts], axis=0)
        v2 = jnp.concatenate([gather(vr, sl) for _, vr, sl in key_parts], axis=0)
        n_q, n_k = q.shape[0], k2.shape[0]
        v2 = jnp.concatenate([v2, jnp.ones((n_k, LANES), jnp.bfloat16)], axis=1)
        s = lax.dot_general(q2, k2, _NT, preferred_element_type=jnp.float32) + bias
        m_cur = jnp.max(s, axis=-1, keepdims=True)
        if first_branch:
            m_new = jnp.broadcast_to(m_cur, (2 * n_q, LANES))
        else:
            m_prev = jnp.concatenate([gather(m_scr.at[hh], q_slices) for hh in range(2)],
                                     axis=0)
            m_new = jnp.maximum(m_prev, m_cur)
            alpha = jnp.exp2(m_prev - m_new)
        p = jnp.exp2(s - jnp.concatenate([m_new] * (n_k // LANES), axis=1))
        pv = jnp.dot(p.astype(jnp.bfloat16), v2, preferred_element_type=jnp.float32)
        num, den = pv[:, :LANES], pv[:, LANES:]
        if not first_branch:
            l_prev = jnp.concatenate([gather(l_scr.at[hh], q_slices) for hh in range(2)],
                                     axis=0)
            a_prev = jnp.concatenate([gather(acc_scr.at[hh], q_slices) for hh in range(2)],
                                     axis=0)
            den = alpha * l_prev + den
            num = alpha * a_prev + num
        for hh in range(2):
            off = hh * n_q
            for sl in q_slices:
                n = sl.stop - sl.start
                m_scr[hh, sl, :] = m_new[off:off + n]
                l_scr[hh, sl, :] = den[off:off + n]
                acc_scr[hh, sl, :] = num[off:off + n]
                off += n

    def run(tile, r, lo, n):
        start = tile * ROW_TILE + r * DIL_RUN + lo
        return slice(start, start + n)

    def block(cur, prv, prev_chunk, bias_ref, first_branch):
        if prev_chunk:
            parts = [(kp_ref, vp_ref, prv), (kc_ref, vc_ref, cur)]
            bias = bias_ref[seq_start]
        else:
            parts = [(kc_ref, vc_ref, prv), (kc_ref, vc_ref, cur)]
            bias = bias_ref[0]
        chain(cur, parts, bias, first_branch)

    last = TILES_PER_CHUNK - 1
    span = DIL_BLOCK * DIL_RES // ROW_TILE
    for r in range(DIL_RES):
        for first_tile in range(0, TILES_PER_CHUNK, span):
            cur = [run(first_tile + t, r, 0, DIL_RUN) for t in range(span)]
            before = first_tile - span if first_tile else TILES_PER_CHUNK - span
            prv = [run(before + t, r, 0, DIL_RUN) for t in range(span)]
            block(cur, prv, first_tile == 0, b16_ref, True)
    for r4 in range(4):
        for t in range(TILES_PER_CHUNK):
            cur = [run(t, r4 + 4 * u, 0, DIL_RUN) for u in range(4)]
            prv = [run(t - 1 if t else last, r4 + 4 * u, 0, DIL_RUN) for u in range(4)]
            block(cur, prv, t == 0, b4_ref, False)
    quarter = DIL_BLOCK // DIL_RES
    per_tile = DIL_RUN // quarter
    for t in range(TILES_PER_CHUNK):
        for qt in range(per_tile):
            cur = [run(t, r, qt * quarter, quarter) for r in range(DIL_RES)]
            if qt:
                prv = [run(t, r, (qt - 1) * quarter, quarter) for r in range(DIL_RES)]
            else:
                prv = [run(t - 1 if t else last, r, DIL_RUN - quarter, quarter)
                       for r in range(DIL_RES)]
            block(cur, prv, t == 0 and qt == 0, b1_ref, False)

    o0 = acc_scr[0] * (1.0 / l_scr[0])
    o1 = acc_scr[1] * (1.0 / l_scr[1])
    o_ref[...] = jnp.where(first, o0, o1).astype(o_ref.dtype)


def _dilated_attention(q, k, v, *, batch, seq):
    n_chunks = seq // DIL_CHUNK
    flat = lambda a: a.reshape(batch * n_chunks, DIL_CHUNK, D_MIX)
    masks = _dilated_masks()
    cur = lambda b, c, p: (b * n_chunks + c, 0, p)
    prev = lambda b, c, p: (b * n_chunks + jnp.maximum(c - 1, 0), 0, p)
    spec = lambda f: pl.BlockSpec((None, DIL_CHUNK, LANES), f)
    mask_specs = [pl.BlockSpec(m.shape, lambda b, c, p: (0, 0, 0)) for m in masks]
    out = pl.pallas_call(
        _dilated_kernel,
        grid=(batch, n_chunks, N_PAIRS),
        in_specs=[spec(cur), spec(prev), spec(cur), spec(prev), spec(cur)] + mask_specs,
        out_specs=spec(cur),
        out_shape=jax.ShapeDtypeStruct((batch * n_chunks, DIL_CHUNK, D_MIX), jnp.bfloat16),
        scratch_shapes=[pltpu.VMEM((2, DIL_CHUNK, LANES), jnp.float32)] * 3,
        compiler_params=_cparams(3),
        name="dilated_attention",
    )(flat(q), flat(k), flat(k), flat(v), flat(v), *masks)
    return out.reshape(batch * seq, D_MIX)


def _mem_attention(qm_ref, km_ref, vm_ref):
    first = _lane_first_half()
    qm = qm_ref[...]
    ones = jnp.ones((km_ref.shape[0], LANES), jnp.bfloat16)
    pairs = []
    for p in range(D_MEMQ // LANES):
        cols = slice(p * LANES, (p + 1) * LANES)
        qp = qm[:, cols]
        n = qp.shape[0]
        zero = jnp.zeros_like(qp)
        q2 = jnp.concatenate([jnp.where(first, qp, zero), jnp.where(first, zero, qp)],
                             axis=0)
        v2 = jnp.concatenate([vm_ref[:, cols], ones], axis=1)
        s = lax.dot_general(q2, km_ref[:, cols], _NT, preferred_element_type=jnp.float32)
        e = jnp.exp2(s - jnp.max(s, axis=-1, keepdims=True))
        pv = jnp.dot(e.astype(jnp.bfloat16), v2, preferred_element_type=jnp.float32)
        o = pv[:, :LANES] * (1.0 / pv[:, LANES:])
        pairs.append(jnp.where(first, o[:n], o[n:]).astype(jnp.bfloat16))
    return jnp.concatenate(pairs, axis=-1)


def _layer_tail_kernel(*refs, tiles_per_seq, final_norm, gather_in, scatter_out):
    (h_ref, mix_ref, qm_ref, km_ref, vm_ref, wo_ref, g_ref, wup_ref, cw_ref, cb_ref,
     wdown_ref, gfin_ref, out_ref, carry_ref, act_ref) = refs
    tm = FFN_TILE
    halo = 8
    n_chunks = D_FF // FF_CHUNK
    i = pl.program_id(0)

    @pl.when(i % tiles_per_seq == 0)
    def _():
        carry_ref[...] = jnp.zeros(carry_ref.shape, jnp.float32)

    mem_out = _mem_attention(qm_ref, km_ref, vm_ref)
    y = jnp.dot(mix_ref[...], wo_ref[:D_MIX, :].astype(jnp.bfloat16),
                preferred_element_type=jnp.float32)
    y = y + jnp.dot(mem_out, wo_ref[D_MIX:, :].astype(jnp.bfloat16),
                    preferred_element_type=jnp.float32)
    n_tiles = tm // ROW_TILE
    if gather_in:
        h = jnp.concatenate([h_ref[t * DIL_RUN:(t + 1) * DIL_RUN, r, :]
                             for t in range(n_tiles) for r in range(DIL_RES)], axis=0) + y
    else:
        h = h_ref[...] + y

    xn = _rms(h, g_ref[...]).astype(jnp.bfloat16)
    run_row = lax.broadcasted_iota(jnp.int32, (DIL_RUN, FF_CHUNK), 0)

    def history(u, cols):
        def shifted(run, before):
            return jnp.where(run_row == 0, before, pltpu.roll(run, 1, 0))

        last15 = carry_ref[halo - 1:halo, cols]
        last14 = carry_ref[2 * halo - 1:2 * halo, cols]
        back1, back2 = [], []
        for t in range(n_tiles):
            ut = u[t * ROW_TILE:(t + 1) * ROW_TILE]
            run14 = ut[14 * DIL_RUN:15 * DIL_RUN]
            run15 = ut[15 * DIL_RUN:]
            s15, s14 = shifted(run15, last15), shifted(run14, last14)
            back1 += [s15, ut[:15 * DIL_RUN]]
            back2 += [s14, s15, ut[:14 * DIL_RUN]]
            last15, last14 = run15[DIL_RUN - 1:], run14[DIL_RUN - 1:]
        carry_ref[0:halo, cols] = u[tm - halo:tm, :]
        carry_ref[halo:2 * halo, cols] = u[tm - DIL_RUN - halo:tm - DIL_RUN, :]
        return jnp.concatenate(back1, axis=0), jnp.concatenate(back2, axis=0)

    def conv(u, cols):
        back1, back2 = history(u, cols)
        w = cw_ref[:, cols]
        c = cb_ref[:, cols] + w[0:1, :] * back2
        c = c + w[1:2, :] * back1
        return c + w[2:3, :] * u

    def val_cols(j):
        return slice(j * FF_CHUNK, (j + 1) * FF_CHUNK)

    def gate_cols(j):
        return slice(D_FF + j * FF_CHUNK, D_FF + (j + 1) * FF_CHUNK)

    def up(j):
        return (jnp.dot(xn, wup_ref[:, val_cols(j)], preferred_element_type=jnp.float32),
                jnp.dot(xn, wup_ref[:, gate_cols(j)], preferred_element_type=jnp.float32))

    u = up(0)
    for j in range(n_chunks):
        u_next = up(j + 1) if j + 1 < n_chunks else None
        val = conv(u[0], val_cols(j))
        gate = conv(u[1], gate_cols(j))
        act_ref[:, val_cols(j)] = (gate * (1.0 / (1.0 + jnp.exp(-gate))) * val
                                   ).astype(jnp.bfloat16)
        u = u_next
    y = h + jnp.dot(act_ref[...], wdown_ref[...], preferred_element_type=jnp.float32)
    if final_norm:
        y = _rms(y, gfin_ref[...])
    if scatter_out:
        for r in range(DIL_RES):
            out_ref[:, r, :] = jnp.concatenate(
                [y[t * ROW_TILE + r * DIL_RUN:t * ROW_TILE + (r + 1) * DIL_RUN]
                 for t in range(n_tiles)], axis=0)
    else:
        out_ref[...] = y


def _layer_tail(h, mix, qm, km, vm, w_out, g, w_up, conv_w, conv_b, w_down, g_final, layer, *,
                seq, mem_tokens, final_norm, gather_in, scatter_out):
    t = h.shape[0]
    tm = FFN_TILE
    tiles_per_seq = seq // tm
    row = lambda i: (i, 0)
    once = dict(pipeline_mode=pl.Buffered(1))
    mem_spec = pl.BlockSpec((None, mem_tokens, D_MEMQ),
                            lambda i: (layer, i // tiles_per_seq, 0))
    plain_spec = pl.BlockSpec((tm, D_MODEL), row)
    view_shape = (t // DIL_RES, DIL_RES, D_MODEL)
    view_spec = pl.BlockSpec((tm // DIL_RES, DIL_RES, D_MODEL), lambda i: (i, 0, 0))
    scratch = [pltpu.VMEM((16, 2 * D_FF), jnp.float32),
               pltpu.VMEM((tm, D_FF), jnp.bfloat16)]
    if gather_in:
        h = h.reshape(view_shape)
    out = pl.pallas_call(
        functools.partial(_layer_tail_kernel, tiles_per_seq=tiles_per_seq, final_norm=final_norm,
                          gather_in=gather_in, scatter_out=scatter_out),
        grid=(t // tm,),
        in_specs=[view_spec if gather_in else plain_spec,
                  pl.BlockSpec((tm, D_MIX), row),
                  pl.BlockSpec((tm, D_MEMQ), row),
                  mem_spec, mem_spec,
                  _layer_spec(w_out, layer, **once),
                  _layer_spec(g, layer),
                  _layer_spec(w_up, layer, **once),
                  _layer_spec(conv_w, layer),
                  _layer_spec(conv_b, layer),
                  _layer_spec(w_down, layer, **once),
                  pl.BlockSpec((1, D_MODEL), lambda i: (0, 0))],
        out_specs=view_spec if scatter_out else plain_spec,
        out_shape=jax.ShapeDtypeStruct(view_shape if scatter_out else (t, D_MODEL), jnp.float32),
        scratch_shapes=scratch,
        compiler_params=pltpu.CompilerParams(dimension_semantics=("arbitrary",),
                                             vmem_limit_bytes=TAIL_VMEM_LIMIT),
        name="layer_tail" + ("_in" if gather_in else "") + ("_out" if scatter_out else ""),
    )(h, mix, qm, km, vm, w_out, g, w_up, conv_w, conv_b, w_down, g_final)
    return out.reshape(t, D_MODEL)


def _rope_tables(seq):
    def residue_major(tab):
        tab = tab.reshape(seq // ROW_TILE, DIL_RUN, DIL_RES, LANES)
        return tab.transpose(0, 2, 1, 3).reshape(seq, LANES)

    with jax.ensure_compile_time_eval():
        inv = 1.0 / (ROPE_THETA ** (jnp.arange(0, HEAD_DIM, 2, dtype=jnp.float32) / HEAD_DIM))
        ang = jnp.arange(seq, dtype=jnp.float32)[:, None] * inv[None, :]
        cos, sin = jnp.cos(ang), jnp.sin(ang)
        reps = LANES // HEAD_DIM
        cos_t = jnp.tile(jnp.concatenate([cos, cos], axis=-1), (1, reps))
        sin_t = jnp.tile(jnp.concatenate([-sin, sin], axis=-1), (1, reps))
        return residue_major(cos_t), residue_major(sin_t)


def kernel(x, mem, norm_mix, norm_mem, norm_ffn, w_in_fox, b_forget, w_in_dil,
           w_mem_kv, w_out, w_up, conv_w, conv_b, w_down, norm_final):
    batch, seq, _ = x.shape
    assert seq % DIL_CHUNK == 0 and seq % FOX_Q_TILE == 0 and FOX_K_TILE == ROW_TILE
    mem_tokens = mem.shape[1]
    depth = norm_mix.shape[0]
    bf = lambda a: a.astype(jnp.bfloat16)
    row3 = lambda a: a.reshape(a.shape[0], 1, a.shape[1])
    h = x.reshape(batch * seq, D_MODEL)
    mem2d = mem.reshape(batch * mem_tokens, D_MODEL)
    rope_tabs = _rope_tables(seq)
    g_mix, g_ffn, g_final = row3(norm_mix), row3(norm_ffn), norm_final.reshape(1, D_MODEL)

    w_up_b, w_down_b = bf(w_up), bf(w_down)
    conv_b3 = row3(conv_b)
    n_fox = w_in_fox.shape[0]
    w_fox = w_in_fox[:, :, :3 * D_MIX]
    tail = w_in_fox[:, :, 3 * D_MIX:]
    forget_params = (
        bf(tail[:, :, N_MIX_HEADS:]),
        bf(_forget_lanes(tail[:, :, :N_MIX_HEADS].reshape(n_fox * D_MODEL, N_MIX_HEADS))
           ).reshape(n_fox, D_MODEL, LANES),
        _forget_lanes(b_forget).reshape(n_fox, 1, LANES))
    km, vm = _mem_kv(mem2d, row3(norm_mem), w_mem_kv, mem_tokens=mem_tokens)

    for layer in range(depth):
        kind, slot = layer % 2, layer // 2
        first = layer == 0
        if kind == 0:
            outs = _in_proj(h, g_mix, w_fox, slot, layer, forget_params=forget_params,
                            seq=seq, gather=first)
            q, k, v, qm, qx, kx = outs[:6]
            mix = _fox_attention(q, qx, k, kx, v, batch=batch, seq=seq)
        else:
            outs = _in_proj(h, g_mix, w_in_dil, slot, layer, rope_tabs=rope_tabs, seq=seq,
                            gather=first)
            q, k, v, qm = outs[:4]
            mix = _dilated_attention(q, k, v, batch=batch, seq=seq)
        if first:
            h = outs[-1]
        h = _layer_tail(h, mix, qm, km, vm, w_out, g_ffn, w_up_b, conv_w, conv_b3, w_down_b,
                        g_final, layer, seq=seq, mem_tokens=mem_tokens,
                        final_norm=(layer == depth - 1), gather_in=False,
                        scatter_out=(layer == depth - 1))
    return h.reshape(batch, seq, D_MODEL)
```
